```python
import math
import jax, jax.numpy as jnp
from jax import lax
import numpy as np

D_MODEL = 1024
BATCH = 32
SEQ = 256
DEPTH = 2
DEC_BATCH = 4
DEC_SEQ = 4096
PAST_LEN = 256

GRID_W = 64
N_BRANCH = 3
D_ATT = D_MODEL // 2
ATT_HEAD_DIM = 64
ATT_HEADS = D_ATT // ATT_HEAD_DIM
WIN_ROWS = 8
WIN_COLS = 16
Q_BLOCK = 128
D_S5 = D_MODEL // 2
S5_GROUP_CH = 16
S5_GROUPS = D_S5 // S5_GROUP_CH
S5_STATE = 64
D_RWKV = D_MODEL // 2
RWKV_HEAD = 64
RWKV_HEADS = D_RWKV // RWKV_HEAD
W_LORA = 64
A_LORA = 64
G_LORA = 128
FF_DENSE = 2816
N_EXPERTS = 8
TOP_K = 2
FF_EXPERT = 1024
N_DENSE = (DEPTH + 1) // 2
N_MOE = DEPTH // 2
EPS = 1e-6
GN_EPS = 64e-5
D_IN = 3 * D_ATT + D_S5 + 3 * D_RWKV + W_LORA + A_LORA + G_LORA + N_BRANCH * D_MODEL

kernel_name = 'hybrid_s5_rwkv7_natten_diffusion_step'

F32 = jnp.float32


def _rmsnorm(x, g):
    x32 = x.astype(F32)
    y = x32 * lax.rsqrt(jnp.mean(x32 * x32, axis=-1, keepdims=True) + EPS)
    return (y * g.astype(F32)).astype(x.dtype)


def _modulation(cvec, w_ada, b_ada):
    m = jax.nn.silu(cvec) @ w_ada + b_ada
    return [t[:, None, :] for t in jnp.split(m, 6, axis=-1)]


def _split_proj(proj):
    sizes = (D_ATT, D_ATT, D_ATT, D_S5, D_RWKV, D_RWKV, D_RWKV, W_LORA, A_LORA, G_LORA,
             N_BRANCH * D_MODEL)
    idx = np.cumsum(sizes)[:-1].tolist()
    return jnp.split(proj, idx, axis=-1)


def _swiglu(h, w1, w3, w2):
    return (jax.nn.silu(h @ w1) * (h @ w3)) @ w2


def _ctx_attention(q, k, v):
    b, n, _ = q.shape
    qh = q.reshape(b, n // Q_BLOCK, Q_BLOCK, ATT_HEADS, ATT_HEAD_DIM).transpose(1, 0, 2, 3, 4)
    kh = k.reshape(b, n, ATT_HEADS, ATT_HEAD_DIM)
    vh = v.reshape(b, n, ATT_HEADS, ATT_HEAD_DIM)
    scale = ATT_HEAD_DIM ** -0.5

    def block(qb):
        s = jnp.einsum('bqhd,bkhd->bhqk', qb, kh).astype(F32) * scale
        p = jax.nn.softmax(s, axis=-1).astype(v.dtype)
        return jnp.einsum('bhqk,bkhd->bqhd', p, vh)

    o = lax.map(block, qh)
    return o.transpose(1, 0, 2, 3, 4).reshape(b, n, D_ATT)


def _na_latent(q, k, v, k_ctx, v_ctx, rpb):
    b, n, _ = q.shape
    rows = n // GRID_W
    wr = min(WIN_ROWS, rows)
    scale = ATT_HEAD_DIM ** -0.5
    grid = lambda t: t.reshape(b, rows, GRID_W, ATT_HEADS, ATT_HEAD_DIM)
    qg, kg, vg = grid(q), grid(k), grid(v)
    cols = jnp.arange(GRID_W)
    col_start = jnp.clip(cols - WIN_COLS // 2, 0, GRID_W - WIN_COLS)
    col_idx = col_start[:, None] + jnp.arange(WIN_COLS)[None, :]
    col_off = col_idx - cols[:, None] + (WIN_COLS - 1)
    n_loc = wr * WIN_COLS

    def one_row(r):
        rs = jnp.clip(r - wr // 2, 0, rows - wr)
        q_r = lax.dynamic_index_in_dim(qg, r, axis=1, keepdims=False)
        k_rows = lax.dynamic_slice_in_dim(kg, rs, wr, axis=1)
        v_rows = lax.dynamic_slice_in_dim(vg, rs, wr, axis=1)
        k_win = k_rows[:, :, col_idx]
        v_win = v_rows[:, :, col_idx]
        row_off = rs + jnp.arange(wr) - r + (WIN_ROWS - 1)
        bias = rpb[:, row_off[None, :, None], col_off[:, None, :]]
        s_loc = jnp.einsum('bqhd,biqjhd->bhqij', q_r, k_win).astype(F32) * scale + bias.astype(F32)
        s_ctx = jnp.einsum('bqhd,bkhd->bhqk', q_r, k_ctx).astype(F32) * scale
        s = jnp.concatenate([s_loc.reshape(b, ATT_HEADS, GRID_W, n_loc), s_ctx], axis=-1)
        p = jax.nn.softmax(s, axis=-1).astype(v.dtype)
        p_loc = p[..., :n_loc].reshape(b, ATT_HEADS, GRID_W, wr, WIN_COLS)
        p_ctx = p[..., n_loc:]
        return (jnp.einsum('bhqij,biqjhd->bqhd', p_loc, v_win)
                + jnp.einsum('bhqk,bkhd->bqhd', p_ctx, v_ctx))

    o = lax.map(one_row, jnp.arange(rows))
    return jnp.moveaxis(o, 0, 1).reshape(b, n, D_ATT)


def _linear_combine(e1, e2):
    a1, b1 = e1
    a2, b2 = e2
    return a2 * a1, a2 * b1 + b2


def _s5_direction(ug, lam_re, lam_im, log_step, b_re, b_im, c_re, c_im, x0, reverse):
    lam = lax.complex(lam_re.astype(F32), lam_im.astype(F32))
    step = jnp.exp(log_step.astype(F32))[:, None]
    lam_bar = jnp.exp(lam * step)
    b_bar = ((lam_bar - 1.0) / lam)[..., None] * lax.complex(b_re.astype(F32), b_im.astype(F32))
    cmat = lax.complex(c_re.astype(F32), c_im.astype(F32))
    bu = jnp.einsum('gpc,blgc->blgp', b_bar, ug.astype(jnp.complex64))
    edge = -1 if reverse else 0
    bu = bu.at[:, edge].add(lam_bar * x0)
    a = jnp.broadcast_to(lam_bar, bu.shape)
    _, xs = lax.associative_scan(_linear_combine, (a, bu), reverse=reverse, axis=1)
    y = jnp.einsum('gcp,blgp->blgc', cmat, xs).real
    return y, xs[:, edge]


def _s5_mixer(u, P, l, x0):
    b, n, _ = u.shape
    ug = u.astype(F32).reshape(b, n, S5_GROUPS, S5_GROUP_CH)
    y = P['s5_d'][l].astype(F32).reshape(S5_GROUPS, S5_GROUP_CH) * ug
    finals = []
    for d in range(2):
        y_d, x_d = _s5_direction(ug, P['s5_lam_re'][l, d], P['s5_lam_im'][l, d], P['s5_log_step'][l, d],
                                 P['s5_b_re'][l, d], P['s5_b_im'][l, d], P['s5_c_re'][l, d],
                                 P['s5_c_im'][l, d], x0[:, d], reverse=(d == 1))
        y = y + y_d
        finals.append(x_d)
    y = jax.nn.gelu(y.reshape(b, n, D_S5)).astype(u.dtype)
    out = y * jax.nn.sigmoid(y @ P['s5_w_glu'][l])
    return out, jnp.stack(finals, axis=1)


def _centred_conv(x, w):
    xp = jnp.pad(x, ((0, 0), (1, 1), (0, 0)))
    return xp[:, :-2] * w[0] + xp[:, 1:-1] * w[1] + xp[:, 2:] * w[2]


def _rwkv_scan(r, w, k, v, kk, a, s0, reverse):
    def step(S, inp):
        r_t, w_t, k_t, v_t, kk_t, a_t = inp
        sa = jnp.einsum('bhij,bhj->bhi', S, -kk_t)
        S = (S * w_t[:, :, None, :] + sa[..., :, None] * (kk_t * a_t)[:, :, None, :]
             + v_t[..., :, None] * k_t[:, :, None, :])
        return S, jnp.einsum('bhij,bhj->bhi', S, r_t)

    xs = tuple(jnp.moveaxis(t, 1, 0) for t in (r, w, k, v, kk, a))
    S, ys = lax.scan(step, s0, xs, reverse=reverse)
    return jnp.moveaxis(ys, 0, 1), S


def _rwkv_mixer(r, k, v, wd, ad, gd, P, l, s0):
    b, n, _ = r.shape
    dt = r.dtype
    rkv = _centred_conv(jnp.concatenate([r, k, v], axis=-1), P['rwkv_conv'][l]).astype(F32)
    r, k, v = jnp.split(rkv, 3, axis=-1)
    wd, ad, gd = wd.astype(F32), ad.astype(F32), gd.astype(F32)
    heads = lambda t: t.reshape(b, n, RWKV_HEADS, RWKV_HEAD)
    g = jax.nn.sigmoid(gd) @ P['rwkv_g_up'][l].astype(F32)
    kk = heads(k * P['rwkv_k_k'][l].astype(F32))
    kk = kk * lax.rsqrt(jnp.sum(kk * kk, axis=-1, keepdims=True) + 1e-12)
    k_a = P['rwkv_k_a'][l].astype(F32)
    rh, vh = heads(r), heads(v)
    y = jnp.zeros_like(rh)
    finals = []
    for d in range(2):
        lora_w = jnp.tanh(wd) @ P['rwkv_w_up'][l, d].astype(F32)
        log_w = -jax.nn.softplus(-(P['rwkv_w0'][l, d].astype(F32) + lora_w)) - 0.5
        decay = jnp.exp(-jnp.exp(log_w))
        a = jax.nn.sigmoid(P['rwkv_a0'][l, d].astype(F32) + ad @ P['rwkv_a_up'][l, d].astype(F32))
        k_d = k * (1.0 + (a - 1.0) * k_a)
        y_d, s_d = _rwkv_scan(rh, heads(decay), heads(k_d), vh, kk, heads(a),
                              s0[:, d].astype(F32), reverse=(d == 1))
        y = y + y_d
        finals.append(s_d)
    mu = jnp.mean(y, axis=-1, keepdims=True)
    var = jnp.mean(jnp.square(y - mu), axis=-1, keepdims=True)
    yn = ((y - mu) * lax.rsqrt(var + GN_EPS)).reshape(b, n, D_RWKV)
    yn = yn * P['rwkv_ln_w'][l].astype(F32) + P['rwkv_ln_b'][l].astype(F32)
    bonus = jnp.sum(rh * heads(k) * P['rwkv_u'][l].astype(F32), axis=-1, keepdims=True) * vh
    out = (yn + bonus.reshape(b, n, D_RWKV)) * g
    return out.astype(dt), jnp.stack(finals, axis=1)


def _mixer(h, P, l, cache):
    b, n, _ = h.shape
    q, k, v, u, r, kr, vr, wd, ad, gd, gates = _split_proj(h @ P['w_in'][l])
    if cache is None:
        k_c = k.reshape(b, n, ATT_HEADS, ATT_HEAD_DIM)
        v_c = v.reshape(b, n, ATT_HEADS, ATT_HEAD_DIM)
        o_att = _ctx_attention(q, k, v)
        s5_0 = jnp.zeros((b, 2, S5_GROUPS, S5_STATE), jnp.complex64)
        rw_0 = jnp.zeros((b, 2, RWKV_HEADS, RWKV_HEAD, RWKV_HEAD), F32)
    else:
        k_c, v_c, s5_0, rw_0 = cache
        o_att = _na_latent(q, k, v, k_c, v_c, P['att_rpb'][l])
    o_s5, s5_fin = _s5_mixer(u, P, l, s5_0)
    o_rw, rw_fin = _rwkv_mixer(r, kr, vr, wd, ad, gd, P, l, rw_0)
    g_s5, g_rw, g_att = jnp.split(jax.nn.sigmoid(gates), N_BRANCH, axis=-1)
    merged = (g_s5 * (o_s5 @ P['w_br_s5'][l]) + g_rw * (o_rw @ P['w_br_rwkv'][l])
              + g_att * (o_att @ P['w_br_att'][l]))
    return merged @ P['w_mix_out'][l], (k_c, v_c, s5_fin, rw_fin)


def _ffn(h, P, l):
    i = l // 2
    if l % 2 == 0:
        return _swiglu(h, P['dense_w1'][i], P['dense_w3'][i], P['dense_w2'][i])
    logits = (h @ P['moe_router_w'][i]).astype(F32) + P['moe_router_b'][i].astype(F32)
    probs = jax.nn.softmax(logits, axis=-1)
    top_p, top_i = lax.top_k(probs, TOP_K)
    top_p = top_p / jnp.sum(top_p, axis=-1, keepdims=True)
    combine = jnp.sum(jax.nn.one_hot(top_i, N_EXPERTS, dtype=F32) * top_p[..., None], axis=-2)
    out = jnp.zeros_like(h)
    for e in range(N_EXPERTS):
        out = out + combine[..., e:e + 1].astype(h.dtype) * _swiglu(
            h, P['moe_w1'][i, e], P['moe_w3'][i, e], P['moe_w2'][i, e])
    return out


def _layer(x, mod, P, l, cache):
    sh1, sc1, g1, sh2, sc2, g2 = mod
    h = _rmsnorm(x, P['norm_pre_mix'][l]) * (1.0 + sc1) + sh1
    m, ctx_out = _mixer(h, P, l, cache)
    x = x + g1 * _rmsnorm(m, P['norm_post_mix'][l])
    h = _rmsnorm(x, P['norm_pre_ffn'][l]) * (1.0 + sc2) + sh2
    x = x + g2 * _rmsnorm(_ffn(h, P, l), P['norm_post_ffn'][l])
    return x, ctx_out


def setup_inputs(seed: int = 0) -> dict:
    key = jax.random.key(seed)
    keys = iter(jax.random.split(key, 64))

    def nrm(shape, scale=1.0):
        return jax.random.normal(next(keys), shape, F32) * scale

    def gain(shape):
        return 1.0 + nrm(shape, 0.05)

    d = D_MODEL
    return {
        'x_prompt': nrm((BATCH, SEQ, d)),
        'x_sample': nrm((DEC_BATCH, DEC_SEQ, d)),
        'cache_k': nrm((DEC_BATCH, DEPTH, PAST_LEN, ATT_HEADS, ATT_HEAD_DIM)),
        'cache_v': nrm((DEC_BATCH, DEPTH, PAST_LEN, ATT_HEADS, ATT_HEAD_DIM)),
        'state_s5_re': nrm((DEC_BATCH, DEPTH, 2, S5_GROUPS, S5_STATE), 0.1),
        'state_s5_im': nrm((DEC_BATCH, DEPTH, 2, S5_GROUPS, S5_STATE), 0.1),
        'state_rwkv': nrm((DEC_BATCH, DEPTH, 2, RWKV_HEADS, RWKV_HEAD, RWKV_HEAD), 0.5),
        'c': nrm((DEC_BATCH, d)),
        'c_ctx': nrm((d,)),
        'w_ada': nrm((DEPTH, d, 6 * d), 0.5 * d ** -0.5),
        'b_ada': nrm((DEPTH, 6 * d), 0.01),
        'norm_pre_mix': gain((DEPTH, d)),
        'norm_post_mix': gain((DEPTH, d)),
        'norm_pre_ffn': gain((DEPTH, d)),
        'norm_post_ffn': gain((DEPTH, d)),
        'w_in': nrm((DEPTH, d, D_IN), d ** -0.5),
        's5_lam_re': -0.5 + nrm((DEPTH, 2, S5_GROUPS, S5_STATE), 0.01),
        's5_lam_im': jnp.pi * jnp.arange(S5_STATE, dtype=F32) + nrm((DEPTH, 2, S5_GROUPS, S5_STATE), 0.01),
        's5_log_step': jax.random.uniform(next(keys), (DEPTH, 2, S5_GROUPS), F32,
                                          math.log(1e-3), math.log(1e-1)),
        's5_b_re': nrm((DEPTH, 2, S5_GROUPS, S5_STATE, S5_GROUP_CH), (2 * S5_GROUP_CH) ** -0.5),
        's5_b_im': nrm((DEPTH, 2, S5_GROUPS, S5_STATE, S5_GROUP_CH), (2 * S5_GROUP_CH) ** -0.5),
        's5_c_re': nrm((DEPTH, 2, S5_GROUPS, S5_GROUP_CH, S5_STATE), (2 * S5_STATE) ** -0.5),
        's5_c_im': nrm((DEPTH, 2, S5_GROUPS, S5_GROUP_CH, S5_STATE), (2 * S5_STATE) ** -0.5),
        's5_d': nrm((DEPTH, D_S5)),
        's5_w_glu': nrm((DEPTH, D_S5, D_S5), D_S5 ** -0.5),
        'rwkv_conv': nrm((DEPTH, 3, 3 * D_RWKV), 3 ** -0.5),
        'rwkv_w0': jax.random.uniform(next(keys), (DEPTH, 2, D_RWKV), F32, -6.0, 1.0),
        'rwkv_w_up': nrm((DEPTH, 2, W_LORA, D_RWKV), 0.1),
        'rwkv_a0': nrm((DEPTH, 2, D_RWKV), 0.5),
        'rwkv_a_up': nrm((DEPTH, 2, A_LORA, D_RWKV), 0.5 * A_LORA ** -0.5),
        'rwkv_g_up': nrm((DEPTH, G_LORA, D_RWKV), G_LORA ** -0.5),
        'rwkv_k_k': 0.85 + nrm((DEPTH, D_RWKV), 0.05),
        'rwkv_k_a': gain((DEPTH, D_RWKV)),
        'rwkv_u': nrm((DEPTH, RWKV_HEADS, RWKV_HEAD), 0.3),
        'rwkv_ln_w': gain((DEPTH, D_RWKV)),
        'rwkv_ln_b': nrm((DEPTH, D_RWKV), 0.01),
        'att_rpb': nrm((DEPTH, ATT_HEADS, 2 * WIN_ROWS - 1, 2 * WIN_COLS - 1), 0.1),
        'w_br_s5': nrm((DEPTH, D_S5, d), D_S5 ** -0.5),
        'w_br_rwkv': nrm((DEPTH, D_RWKV, d), D_RWKV ** -0.5),
        'w_br_att': nrm((DEPTH, D_ATT, d), D_ATT ** -0.5),
        'w_mix_out': nrm((DEPTH, d, d), d ** -0.5),
        'dense_w1': nrm((N_DENSE, d, FF_DENSE), d ** -0.5),
        'dense_w3': nrm((N_DENSE, d, FF_DENSE), d ** -0.5),
        'dense_w2': nrm((N_DENSE, FF_DENSE, d), FF_DENSE ** -0.5),
        'moe_router_w': nrm((N_MOE, d, N_EXPERTS), d ** -0.5),
        'moe_router_b': nrm((N_MOE, N_EXPERTS), 0.01),
        'moe_w1': nrm((N_MOE, N_EXPERTS, d, FF_EXPERT), d ** -0.5),
        'moe_w3': nrm((N_MOE, N_EXPERTS, d, FF_EXPERT), d ** -0.5),
        'moe_w2': nrm((N_MOE, N_EXPERTS, FF_EXPERT, d), FF_EXPERT ** -0.5),
    }


def reference(x_prompt, x_sample, cache_k, cache_v, state_s5_re, state_s5_im, state_rwkv, c, c_ctx,
              w_ada, b_ada, norm_pre_mix, norm_post_mix, norm_pre_ffn, norm_post_ffn, w_in,
              s5_lam_re, s5_lam_im, s5_log_step, s5_b_re, s5_b_im, s5_c_re, s5_c_im, s5_d, s5_w_glu,
              rwkv_conv, rwkv_w0, rwkv_w_up, rwkv_a0, rwkv_a_up, rwkv_g_up, rwkv_k_k, rwkv_k_a,
              rwkv_u, rwkv_ln_w, rwkv_ln_b, att_rpb, w_br_s5, w_br_rwkv, w_br_att, w_mix_out,
              dense_w1, dense_w3, dense_w2, moe_router_w, moe_router_b, moe_w1, moe_w3, moe_w2):
    P = dict(norm_pre_mix=norm_pre_mix, norm_post_mix=norm_post_mix, norm_pre_ffn=norm_pre_ffn,
             norm_post_ffn=norm_post_ffn, w_in=w_in, s5_lam_re=s5_lam_re, s5_lam_im=s5_lam_im,
             s5_log_step=s5_log_step, s5_b_re=s5_b_re, s5_b_im=s5_b_im, s5_c_re=s5_c_re,
             s5_c_im=s5_c_im, s5_d=s5_d, s5_w_glu=s5_w_glu, rwkv_conv=rwkv_conv, rwkv_w0=rwkv_w0,
             rwkv_w_up=rwkv_w_up, rwkv_a0=rwkv_a0, rwkv_a_up=rwkv_a_up, rwkv_g_up=rwkv_g_up,
             rwkv_k_k=rwkv_k_k, rwkv_k_a=rwkv_k_a, rwkv_u=rwkv_u, rwkv_ln_w=rwkv_ln_w,
             rwkv_ln_b=rwkv_ln_b, att_rpb=att_rpb, w_br_s5=w_br_s5, w_br_rwkv=w_br_rwkv,
             w_br_att=w_br_att, w_mix_out=w_mix_out, dense_w1=dense_w1, dense_w3=dense_w3,
             dense_w2=dense_w2, moe_router_w=moe_router_w, moe_router_b=moe_router_b,
             moe_w1=moe_w1, moe_w3=moe_w3, moe_w2=moe_w2)

    xp = x_prompt
    ks, vs, s5s, rws = [], [], [], []
    for l in range(DEPTH):
        mod = _modulation(c_ctx[None, :], w_ada[l], b_ada[l])
        xp, (k_c, v_c, s5_f, rw_f) = _layer(xp, mod, P, l, None)
        ks.append(k_c)
        vs.append(v_c)
        s5s.append(s5_f)
        rws.append(rw_f)
    new_k = jnp.stack(ks, axis=1)
    new_v = jnp.stack(vs, axis=1)
    new_s5 = jnp.stack(s5s, axis=1)
    new_s5_re = jnp.real(new_s5)
    new_s5_im = jnp.imag(new_s5)
    new_rwkv = jnp.stack(rws, axis=1)

    xs = x_sample
    for l in range(DEPTH):
        mod = _modulation(c, w_ada[l], b_ada[l])
        cache = (cache_k[:, l], cache_v[:, l],
                 lax.complex(state_s5_re[:, l].astype(F32), state_s5_im[:, l].astype(F32)),
                 state_rwkv[:, l].astype(F32))
        xs, _ = _layer(xs, mod, P, l, cache)

    return (xp, xs, new_k, new_v, new_s5_re, new_s5_im, new_rwkv)
```

```python
import functools
import math

import numpy as np
import jax
import jax.numpy as jnp
from jax import lax
from jax.experimental import pallas as pl
from jax.experimental.pallas import tpu as pltpu

F32 = jnp.float32
BF16 = jnp.bfloat16
HIGHEST = lax.Precision.HIGHEST

D = 1024
DEPTH = 2
GRID_W = 64
DH = 64
HEADS = 8
DB = 512
WIN_ROWS = 8
WIN_COLS = 16
S5_CH = 16
S5_G = 32
S5_P = 64
S5_J = 16
FF_DENSE = 2816
N_EXP = 8
FF_EXP = 1024
EPS = 1e-6
GN_EPS = 64e-5
NEG = -1e30

C_GATE = 0
C_RKV = 3072
C_QKV = 4608
C_U = 6144
C_LORA = 6656
D_IN = 6912
TN_PROJ = 768

LANES = 128
RW_L = 64
NA_RB = 4
NA_KR = 12
VMEM_LIMIT = 56 * 1024 * 1024


def _cparams(n_axes, vmem=VMEM_LIMIT):
    return pltpu.CompilerParams(dimension_semantics=("arbitrary",) * n_axes,
                                vmem_limit_bytes=vmem)


def _dot(a, b):
    return jnp.dot(a, b, preferred_element_type=F32)


def _dot_nt(a, b):
    return lax.dot_general(a, b, (((1,), (1,)), ((), ())), preferred_element_type=F32)


def _dot_tn(a, b):
    return lax.dot_general(a, b, (((0,), (0,)), ((), ())), preferred_element_type=F32)


def _split2(x):
    hi = x.astype(BF16)
    lo = (x - hi.astype(F32)).astype(BF16)
    return hi, lo


def _split3(x):
    x1 = x.astype(BF16)
    r1 = x - x1.astype(F32)
    x2 = r1.astype(BF16)
    x3 = (r1 - x2.astype(F32)).astype(BF16)
    return x1, x2, x3


def _dot_rhs_exact(x, m):
    x1, x2, x3 = _split3(x)
    return _dot(x1, m) + _dot(x2, m) + _dot(x3, m)


def _dot_lhs_exact(m, x):
    x1, x2, x3 = _split3(x)
    return _dot(m, x1) + _dot(m, x2) + _dot(m, x3)


def _mm3(a, b):
    a1, a2 = _split2(a)
    b1, b2 = _split2(b)
    return _dot(a1, b1) + _dot(a1, b2) + _dot(a2, b1)


def _mm(a, b):
    return _dot(a.astype(BF16), b.astype(BF16))


def _sigmoid(x):
    return 1.0 / (1.0 + jnp.exp(-x))


def _silu(x):
    return x * _sigmoid(x)


def _softplus(x):
    return jnp.maximum(x, 0.0) + jnp.log(1.0 + jnp.exp(-jnp.abs(x)))


def _rms(x, g):
    return x * lax.rsqrt(jnp.mean(x * x, axis=-1, keepdims=True) + EPS) * g


def _head_block_matrix(width, value):
    r = lax.broadcasted_iota(jnp.int32, (width, width), 0) // DH
    c = lax.broadcasted_iota(jnp.int32, (width, width), 1) // DH
    return jnp.where(r == c, value, 0.0).astype(BF16)


def _mod_kernel(c_ref, w_ref, b_ref, o_ref):
    s = _silu(c_ref[...]).astype(BF16)
    o_ref[...] = _dot(s, w_ref[...].astype(BF16)) + b_ref[...]


def _modulation(cvec, w_ada, b_ada):
    tn = 1536
    out = pl.pallas_call(
        _mod_kernel,
        grid=(DEPTH, 6 * D // tn),
        in_specs=[pl.BlockSpec((8, D), lambda l, j: (0, 0)),
                  pl.BlockSpec((None, D, tn), lambda l, j: (l, 0, j)),
                  pl.BlockSpec((None, 1, tn), lambda l, j: (l, 0, j))],
        out_specs=pl.BlockSpec((None, 8, tn), lambda l, j: (l, 0, j)),
        out_shape=jax.ShapeDtypeStruct((DEPTH, 8, 6 * D), F32),
        compiler_params=_cparams(2),
    )(cvec, w_ada, b_ada.reshape(DEPTH, 1, 6 * D))
    return out.reshape(DEPTH, 8, 1, 6 * D)


def _mod_spec(part, row_of_tile, n_grid):
    if n_grid == 1:
        return pl.BlockSpec((None, 1, D), lambda i: (row_of_tile(i), 0, part))
    return pl.BlockSpec((None, 1, D), lambda i, j: (row_of_tile(i), 0, part))


def _proj_kernel(x_ref, g_ref, sc_ref, sh_ref, w_ref, o_ref, oq_ref, h_scr):
    j = pl.program_id(1)

    @pl.when(j == 0)
    def _():
        h = _rms(x_ref[...], g_ref[...]) * (1.0 + sc_ref[...]) + sh_ref[...]
        h_scr[...] = h.astype(BF16)

    acc = _dot(h_scr[...], w_ref[...])
    o_ref[...] = acc

    @pl.when((j >= C_QKV // TN_PROJ) & (j < C_U // TN_PROJ))
    def _():
        oq_ref[...] = acc.astype(BF16)


def _projection(x, mod_l, row_of_tile, g_pre, w_in_b, tm):
    t = x.shape[0]
    jq = C_QKV // TN_PROJ
    return pl.pallas_call(
        _proj_kernel,
        grid=(t // tm, D_IN // TN_PROJ),
        in_specs=[pl.BlockSpec((tm, D), lambda i, j: (i, 0)),
                  pl.BlockSpec((1, D), lambda i, j: (0, 0)),
                  _mod_spec(1, row_of_tile, 2),
                  _mod_spec(0, row_of_tile, 2),
                  pl.BlockSpec((D, TN_PROJ), lambda i, j: (0, j))],
        out_specs=[pl.BlockSpec((tm, TN_PROJ), lambda i, j: (i, j)),
                   pl.BlockSpec((tm, TN_PROJ), lambda i, j: (i, jnp.clip(j - jq, 0, 1)))],
        out_shape=[jax.ShapeDtypeStruct((t, D_IN), F32),
                   jax.ShapeDtypeStruct((t, 3 * DB), BF16)],
        scratch_shapes=[pltpu.VMEM((tm, D), BF16)],
        compiler_params=_cparams(2),
    )(x, g_pre, mod_l, mod_l, w_in_b)


def _pair_masks():
    lane = lax.broadcasted_iota(jnp.int32, (1, LANES), 1)
    return lane < DH


def _softmax_pv(parts, scale):
    m = None
    for s, _ in parts:
        mx = jnp.max(s, axis=-1, keepdims=True)
        m = mx if m is None else jnp.maximum(m, mx)
    es = [jnp.exp(s - m) for s, _ in parts]
    den = None
    for e in es:
        sm = jnp.sum(e, axis=-1, keepdims=True)
        den = sm if den is None else den + sm
    inv = 1.0 / den
    out = None
    for e, (_, v) in zip(es, parts):
        o = _dot((e * inv).astype(BF16), v)
        out = o if out is None else out + o
    return out


def _ctx_att_kernel(q_ref, k_ref, v_ref, o_ref):
    scale = DH ** -0.5
    m0 = _pair_masks()
    for p in range(HEADS // 2):
        sl = slice(p * LANES, (p + 1) * LANES)
        qp, kp, vp = q_ref[:, sl], k_ref[:, sl], v_ref[:, sl]
        o_pair = None
        for hh in range(2):
            msk = m0 if hh == 0 else jnp.logical_not(m0)
            qm = jnp.where(msk, qp, jnp.zeros_like(qp))
            s = _dot_nt(qm, kp) * scale
            o = _softmax_pv([(s, vp)], scale)
            o_pair = o if o_pair is None else jnp.where(m0, o_pair, o)
        o_ref[:, sl] = o_pair


def _ctx_attention(qkv, b, n):
    return pl.pallas_call(
        _ctx_att_kernel,
        grid=(b,),
        in_specs=[pl.BlockSpec((n, DB), lambda i: (i, 0)),
                  pl.BlockSpec((n, DB), lambda i: (i, 1)),
                  pl.BlockSpec((n, DB), lambda i: (i, 2))],
        out_specs=pl.BlockSpec((n, DB), lambda i: (i, 0)),
        out_shape=jax.ShapeDtypeStruct((b * n, DB), F32),
        compiler_params=_cparams(1),
    )(qkv, qkv, qkv)


def _na_key_start(rb, rows):
    return jnp.clip(rb * NA_RB - WIN_ROWS // 2, 0, rows - NA_KR)


def _na_kernel(q_ref, k_ref, v_ref, kc_ref, vc_ref, bias_ref, o_ref, *, rows):
    scale = DH ** -0.5
    m0 = _pair_masks()
    rb = pl.program_id(1)
    start = pl.multiple_of(_na_key_start(rb, rows) * GRID_W, GRID_W)
    nk = NA_KR * GRID_W
    for p in range(HEADS // 2):
        sl = slice(p * LANES, (p + 1) * LANES)
        qp = q_ref[:, sl]
        kp = k_ref[pl.ds(start, nk), sl]
        vp = v_ref[pl.ds(start, nk), sl]
        kcp, vcp = kc_ref[:, sl], vc_ref[:, sl]
        o_pair = None
        for hh in range(2):
            msk = m0 if hh == 0 else jnp.logical_not(m0)
            qm = jnp.where(msk, qp, jnp.zeros_like(qp))
            s_loc = _dot_nt(qm, kp) * scale + bias_ref[2 * p + hh]
            s_ctx = _dot_nt(qm, kcp) * scale
            o = _softmax_pv([(s_loc, vp), (s_ctx, vcp)], scale)
            o_pair = o if o_pair is None else jnp.where(m0, o_pair, o)
        o_ref[:, sl] = o_pair


def _na_bias_tables(rpb, rows):
    n_rb = rows // NA_RB
    wr = min(WIN_ROWS, rows)
    tables = []
    for rb in (0, 1, n_rb - 1):
        u0 = int(np.clip(rb * NA_RB - WIN_ROWS // 2, 0, rows - NA_KR))
        qi = np.arange(NA_RB * GRID_W)
        r = rb * NA_RB + qi // GRID_W
        qc = qi % GRID_W
        rs = np.clip(r - wr // 2, 0, rows - wr)
        cs = np.clip(qc - WIN_COLS // 2, 0, GRID_W - WIN_COLS)
        kj = np.arange(NA_KR * GRID_W)
        krow = u0 + kj // GRID_W
        kcol = kj % GRID_W
        valid = ((krow[None, :] >= rs[:, None]) & (krow[None, :] < rs[:, None] + wr)
                 & (kcol[None, :] >= cs[:, None]) & (kcol[None, :] < cs[:, None] + WIN_COLS))
        ri = np.clip(krow[None, :] - r[:, None] + (WIN_ROWS - 1), 0, 2 * WIN_ROWS - 2)
        ci = np.clip(kcol[None, :] - qc[:, None] + (WIN_COLS - 1), 0, 2 * WIN_COLS - 2)
        tables.append(jnp.where(valid[None], rpb[:, ri, ci], NEG))
    return jnp.stack(tables, axis=0).astype(F32)


def _na_attention(qkv, k_ctx, v_ctx, bias, b, n):
    rows = n // GRID_W
    n_rb = rows // NA_RB
    tq = NA_RB * GRID_W
    lc = k_ctx.shape[0] // b

    def variant(j):
        return jnp.where(j == 0, 0, jnp.where(j == n_rb - 1, 2, 1))

    return pl.pallas_call(
        functools.partial(_na_kernel, rows=rows),
        grid=(b, n_rb),
        in_specs=[pl.BlockSpec((tq, DB), lambda i, j: (i * n_rb + j, 0)),
                  pl.BlockSpec((n, DB), lambda i, j: (i, 1)),
                  pl.BlockSpec((n, DB), lambda i, j: (i, 2)),
                  pl.BlockSpec((lc, DB), lambda i, j: (i, 0)),
                  pl.BlockSpec((lc, DB), lambda i, j: (i, 0)),
                  pl.BlockSpec((None, HEADS, tq, NA_KR * GRID_W), lambda i, j: (variant(j), 0, 0, 0))],
        out_specs=pl.BlockSpec((tq, DB), lambda i, j: (i * n_rb + j, 0)),
        out_shape=jax.ShapeDtypeStruct((b * n, DB), F32),
        compiler_params=_cparams(2),
    )(qkv, qkv, qkv, k_ctx, v_ctx, bias)


def _s5_params(lam_re, lam_im, log_step, b_re, b_im, c_re, c_im):
    hp = dict(precision=HIGHEST)
    step = jnp.exp(log_step)[:, None]
    lam = lax.complex(lam_re, lam_im)
    lam_bar = jnp.exp(lam * step)
    b_bar = ((lam_bar - 1.0) / lam)[..., None] * lax.complex(b_re, b_im)
    cmat = lax.complex(c_re, c_im)
    dd = jnp.arange(S5_J + 1, dtype=F32)[:, None, None]
    lam_pow = jnp.exp(dd * (lam * step)[None])
    bx = lam_pow[:S5_J][::-1][:, :, :, None] * b_bar[None]
    bx = jnp.transpose(bx, (1, 0, 3, 2)).reshape(S5_G, S5_J * S5_CH, S5_P)
    kd = jnp.einsum('gcp,dgp,gpk->gdck', cmat, lam_pow[:S5_J], b_bar, **hp).real
    tau = np.arange(S5_J)
    dmat = tau[None, :] - tau[:, None]
    toep = kd[:, np.clip(dmat, 0, S5_J - 1)]
    toep = jnp.where((dmat >= 0)[None, :, :, None, None], toep, 0.0)
    toep = jnp.transpose(toep, (0, 1, 4, 2, 3)).reshape(S5_G, S5_J * S5_CH, S5_J * S5_CH)
    cm = cmat[:, None] * lam_pow[1:][:, :, None, :].transpose(1, 0, 2, 3)
    cx = jnp.transpose(cm, (0, 3, 1, 2)).reshape(S5_G, S5_P, S5_J * S5_CH)
    lam_j = lam_pow[S5_J]
    b1 = jnp.pad(jnp.transpose(b_bar, (0, 2, 1)), ((0, 0), (0, (S5_J - 1) * S5_CH), (0, 0)))
    return dict(bx_re=bx.real, bx_im=bx.imag, toep=toep, cx_re=cx.real, cx_im=-cx.imag,
                lj_re=lam_j.real[:, None, :], lj_im=lam_j.imag[:, None, :],
                l1_re=lam_bar.real[:, None, :], l1_im=lam_bar.imag[:, None, :],
                b1_re=b1.real, b1_im=b1.imag)


def _s5_kernel(u_ref, bre_ref, bim_ref, toep_ref, cre_ref, cim_ref, ljre_ref, ljim_ref,
               l1re_ref, l1im_ref, b1re_ref, b1im_ref, x0re_ref, x0im_ref,
               y_ref, fre_ref, fim_ref, zre_scr, zim_scr, *, nsub, bp):
    u = u_ref[...]
    zre_scr[...] = jnp.dot(u, bre_ref[...], precision=HIGHEST, preferred_element_type=F32)
    zim_scr[...] = jnp.dot(u, bim_ref[...], precision=HIGHEST, preferred_element_type=F32)
    u_first = u_ref[0:bp, :]
    x0re, x0im = x0re_ref[...], x0im_ref[...]
    fre_ref[...] = (l1re_ref[...] * x0re - l1im_ref[...] * x0im
                    + jnp.dot(u_first, b1re_ref[...], precision=HIGHEST, preferred_element_type=F32))
    fim_ref[...] = (l1re_ref[...] * x0im + l1im_ref[...] * x0re
                    + jnp.dot(u_first, b1im_ref[...], precision=HIGHEST, preferred_element_type=F32))
    lre = jnp.broadcast_to(ljre_ref[...], (bp, S5_P))
    lim = jnp.broadcast_to(ljim_ref[...], (bp, S5_P))

    def step(j, carry):
        xre, xim = carry
        rows = pl.ds(pl.multiple_of(j * bp, bp), bp)
        zre, zim = zre_scr[rows, :], zim_scr[rows, :]
        zre_scr[rows, :] = xre
        zim_scr[rows, :] = xim
        return (lre * xre - lim * xim + zre, lre * xim + lim * xre + zim)

    lax.fori_loop(0, nsub, step, (x0re, x0im))
    ub = u.astype(BF16)
    y_ref[...] = (_dot(ub, toep_ref[...].astype(BF16))
                  + _mm(zre_scr[...], cre_ref[...]) + _mm(zim_scr[...], cim_ref[...]))


def _s5_scan(u2, prm, x0re, x0im, nsub, bp):
    r = nsub * bp
    jc = S5_J * S5_CH
    gspec = lambda shp: pl.BlockSpec((None, None) + shp, lambda d, g: (d, g, 0, 0))
    return pl.pallas_call(
        functools.partial(_s5_kernel, nsub=nsub, bp=bp),
        grid=(2, S5_G),
        in_specs=[gspec((r, jc)), gspec((jc, S5_P)), gspec((jc, S5_P)), gspec((jc, jc)),
                  gspec((S5_P, jc)), gspec((S5_P, jc)), gspec((1, S5_P)), gspec((1, S5_P)),
                  gspec((1, S5_P)), gspec((1, S5_P)), gspec((jc, S5_P)), gspec((jc, S5_P)),
                  gspec((bp, S5_P)), gspec((bp, S5_P))],
        out_specs=[gspec((r, jc)), gspec((bp, S5_P)), gspec((bp, S5_P))],
        out_shape=[jax.ShapeDtypeStruct((2, S5_G, r, jc), F32),
                   jax.ShapeDtypeStruct((2, S5_G, bp, S5_P), F32),
                   jax.ShapeDtypeStruct((2, S5_G, bp, S5_P), F32)],
        scratch_shapes=[pltpu.VMEM((r, S5_P), F32), pltpu.VMEM((r, S5_P), F32)],
        compiler_params=_cparams(2),
    )(u2, prm['bx_re'], prm['bx_im'], prm['toep'], prm['cx_re'], prm['cx_im'],
      prm['lj_re'], prm['lj_im'], prm['l1_re'], prm['l1_im'], prm['b1_re'], prm['b1_im'], x0re, x0im)


def _s5_post_kernel(u_ref, yf_ref, yb_ref, d_ref, w_ref, o_ref):
    y = jax.nn.gelu(d_ref[...] * u_ref[...] + yf_ref[...] + yb_ref[...])
    o_ref[...] = y * _sigmoid(_dot(y.astype(BF16), w_ref[...]))


def _s5_mixer(proj, b, n, sp, s5_d, w_glu_b, x0re, x0im, tm):
    t = b * n
    nsub = n // S5_J
    bp = max(b, 8)
    u = proj[:, C_U:C_U + DB].reshape(b, nsub, S5_J, S5_G, S5_CH)
    u = jnp.stack([u, u[:, ::-1, ::-1]], axis=0)
    u = jnp.transpose(u, (0, 4, 2, 1, 3, 5))
    if bp != b:
        u = jnp.pad(u, ((0, 0), (0, 0), (0, 0), (0, bp - b), (0, 0), (0, 0)))
        x0re = jnp.pad(x0re, ((0, 0), (0, 0), (0, bp - b), (0, 0)))
        x0im = jnp.pad(x0im, ((0, 0), (0, 0), (0, bp - b), (0, 0)))
    u2 = u.reshape(2, S5_G, nsub * bp, S5_J * S5_CH)
    y2, fre, fim = _s5_scan(u2, sp, x0re, x0im, nsub, bp)
    y2 = y2.reshape(2, S5_G, nsub, bp, S5_J, S5_CH)[:, :, :, :b]
    y2 = jnp.transpose(y2, (0, 3, 2, 4, 1, 5))
    yf = y2[0].reshape(t, DB)
    yb = y2[1][:, ::-1, ::-1].reshape(t, DB)
    o = pl.pallas_call(
        _s5_post_kernel,
        grid=(t // tm,),
        in_specs=[pl.BlockSpec((tm, DB), lambda i: (i, C_U // DB)),
                  pl.BlockSpec((tm, DB), lambda i: (i, 0)),
                  pl.BlockSpec((tm, DB), lambda i: (i, 0)),
                  pl.BlockSpec((1, DB), lambda i: (0, 0)),
                  pl.BlockSpec((DB, DB), lambda i: (0, 0))],
        out_specs=pl.BlockSpec((tm, DB), lambda i: (i, 0)),
        out_shape=jax.ShapeDtypeStruct((t, DB), F32),
        compiler_params=_cparams(1),
    )(proj, yf, yb, s5_d, w_glu_b)
    return o, fre[:, :, :b], fim[:, :, :b]


def _conv_kernel(x_ref, prev_ref, next_ref, w_ref, o_ref, *, tiles_per_seq):
    i = pl.program_id(0)
    x = x_ref[...]
    tm = x.shape[0]
    row = lax.broadcasted_iota(jnp.int32, (tm, 1), 0)
    first = (i % tiles_per_seq) == 0
    last = (i % tiles_per_seq) == tiles_per_seq - 1
    prev_row = jnp.where(first, 0.0, prev_ref[7:8, :])
    next_row = jnp.where(last, 0.0, next_ref[0:1, :])
    x_dn = jnp.where(row == 0, prev_row, pltpu.roll(x, 1, axis=0))
    x_up = jnp.where(row == tm - 1, next_row, pltpu.roll(x, tm - 1, axis=0))
    o_ref[...] = x_dn * w_ref[0:1, :] + x * w_ref[1:2, :] + x_up * w_ref[2:3, :]


def _rwkv_conv(proj, conv_w, n, tm):
    t = proj.shape[0]
    w3 = 3 * DB
    cb = C_RKV // w3
    nb8 = t // 8
    return pl.pallas_call(
        functools.partial(_conv_kernel, tiles_per_seq=n // tm),
        grid=(t // tm,),
        in_specs=[pl.BlockSpec((tm, w3), lambda i: (i, cb)),
                  pl.BlockSpec((8, w3), lambda i: (jnp.maximum(i * (tm // 8) - 1, 0), cb)),
                  pl.BlockSpec((8, w3), lambda i: (jnp.minimum((i + 1) * (tm // 8), nb8 - 1), cb)),
                  pl.BlockSpec((3, w3), lambda i: (0, 0))],
        out_specs=pl.BlockSpec((tm, w3), lambda i: (i, 0)),
        out_shape=jax.ShapeDtypeStruct((t, w3), F32),
        compiler_params=_cparams(1),
    )(proj, proj, proj, conv_w)


def _rwkv_chunk_kernel(rc_ref, kc_ref, vc_ref, lora_ref, w0_ref, wup_ref, a0_ref, aup_ref,
                       kk_ref, ka_ref, s0_ref, y_ref, sfin_ref, z_scr, *, nc):
    d = pl.program_id(1)
    ci = pl.program_id(2)
    L = RW_L

    @pl.when(ci == 0)
    def _():
        z_scr[...] = s0_ref[...]

    rc, kc, vc = rc_ref[...], kc_ref[...], vc_ref[...]
    wd = lora_ref[:, 0:64]
    ad = lora_ref[:, 64:128]
    lora_w = jnp.dot(jnp.tanh(wd), wup_ref[...], precision=HIGHEST, preferred_element_type=F32)
    log_w = -_softplus(-(w0_ref[...] + lora_w)) - 0.5
    lw = -jnp.exp(log_w)
    a = _sigmoid(a0_ref[...] + jnp.dot(ad, aup_ref[...], precision=HIGHEST,
                                       preferred_element_type=F32))
    kd = kc * (1.0 + (a - 1.0) * ka_ref[...])
    kk = kc * kk_ref[...]
    kk = kk * lax.rsqrt(_dot_rhs_exact(kk * kk, _head_block_matrix(DB, 1.0)) + 1e-12)
    alpha = -kk
    beta = kk * a

    sgn = 1 - 2 * d
    tt = lax.broadcasted_iota(jnp.int32, (L, L), 0)
    ss = lax.broadcasted_iota(jnp.int32, (L, L), 1)
    tri = jnp.where((tt - ss) * sgn >= 0, 1.0, 0.0).astype(BF16)
    c = _dot_lhs_exact(tri, lw)
    c_ex = c - lw
    ctot = jnp.sum(lw, axis=0, keepdims=True)
    mid = 0.5 * ctot
    e_in = jnp.exp(c - mid)
    e_ex = jnp.exp(c_ex - mid)
    e_out = jnp.exp(mid - c)
    e_mid = jnp.exp(mid)
    al_t = alpha * e_ex
    r_t = rc * e_in
    be_t = beta * e_out
    k_t = kd * e_out
    a0s = al_t * e_mid
    r0s = r_t * e_mid
    bps = be_t * e_mid
    kps = k_t * e_mid
    p_l = e_mid * e_mid

    m0 = _pair_masks()
    t2 = lax.broadcasted_iota(jnp.int32, (2 * L, 2 * L), 0) % L
    s2 = lax.broadcasted_iota(jnp.int32, (2 * L, 2 * L), 1) % L
    strict = (t2 - s2) * sgn > 0
    incl = (t2 - s2) * sgn >= 0
    eye = (lax.broadcasted_iota(jnp.int32, (LANES, LANES), 0)
           == lax.broadcasted_iota(jnp.int32, (LANES, LANES), 1))
    eye_f = jnp.where(eye, 1.0, 0.0)
    zeros = jnp.zeros((LANES, LANES), F32)

    for p in range(HEADS // 2):
        sl = slice(p * LANES, (p + 1) * LANES)

        def stack(x):
            xp = x[:, sl]
            return jnp.concatenate([jnp.where(m0, xp, 0.0), jnp.where(m0, 0.0, xp)], axis=0)

        lhs = jnp.concatenate([stack(al_t), stack(r_t)], axis=0).astype(BF16)
        rhs = jnp.concatenate([stack(be_t), stack(k_t)], axis=0).astype(BF16)
        raw = _dot_nt(lhs, rhs)
        amat = jnp.where(strict, raw[:LANES, :LANES], 0.0)
        bmat = jnp.where(strict, raw[:LANES, LANES:], 0.0)
        qb = jnp.where(incl, raw[LANES:, :LANES], 0.0)
        qk = jnp.where(incl, raw[LANES:, LANES:], 0.0)

        smat = eye_f + amat
        pw = _mm3(amat, amat)
        for _ in range(4):
            x = _mm3(pw, jnp.concatenate([pw, smat], axis=1))
            smat = smat + x[:, LANES:]
            pw = x[:, :LANES]
        tmat = smat + _mm3(pw, smat)

        vs = stack(vc)
        bv = _mm(bmat, vs)
        wu = _mm3(tmat, jnp.concatenate([stack(a0s), bv], axis=1))
        rhs2 = jnp.concatenate([wu, jnp.concatenate([zeros, vs], axis=1)], axis=0)
        out_a = _mm(jnp.concatenate([qb, qk], axis=1), rhs2)
        gmat = stack(r0s) + out_a[:, :LANES]
        y0 = out_a[:, LANES:]
        lhs_b = jnp.concatenate([stack(bps), stack(kps)], axis=0).astype(BF16)
        out_b = _dot_tn(lhs_b, rhs2.astype(BF16))
        mz = jnp.where(eye, p_l[:, sl], 0.0) + out_b[:, :LANES]
        nz = out_b[:, LANES:]

        hz = _mm3(jnp.concatenate([gmat, mz], axis=0), z_scr[p])
        y = y0 + hz[:LANES]
        z_scr[p] = hz[LANES:] + nz
        y_ref[:, sl] = y[:L] + y[L:]

    @pl.when(ci == nc - 1)
    def _():
        sfin_ref[...] = z_scr[...]


def _rwkv_post_kernel(yf_ref, yb_ref, rkv_ref, gd_ref, gup_ref, u_ref, lnw_ref, lnb_ref, o_ref):
    y = yf_ref[...] + yb_ref[...]
    avg = _head_block_matrix(DB, 1.0 / DH)
    mu = _dot_rhs_exact(y, avg)
    yc = y - mu
    var = _dot_rhs_exact(yc * yc, avg)
    yn = yc * lax.rsqrt(var + GN_EPS) * lnw_ref[...] + lnb_ref[...]
    rc, kc, vc = rkv_ref[:, 0:DB], rkv_ref[:, DB:2 * DB], rkv_ref[:, 2 * DB:3 * DB]
    bonus = _dot_rhs_exact(rc * kc * u_ref[...], _head_block_matrix(DB, 1.0)) * vc
    g = _dot(_sigmoid(gd_ref[:, 128:256]).astype(BF16), gup_ref[...])
    o_ref[...] = (yn + bonus) * g


def _rwkv_mixer(proj, b, n, conv_w, prm, s0_pairs, tm):
    t = b * n
    nc = n // RW_L
    rkv = _rwkv_conv(proj, conv_w, n, min(tm, n))

    def chunk_row(i, dd, ci):
        return i * nc + jnp.where(dd == 0, ci, nc - 1 - ci)

    tok = lambda col: pl.BlockSpec((RW_L, DB), lambda i, dd, ci: (chunk_row(i, dd, ci), col))
    dirp = lambda r: pl.BlockSpec((None, r, DB), lambda i, dd, ci: (dd, 0, 0))
    shared = pl.BlockSpec((1, DB), lambda i, dd, ci: (0, 0))
    y2, sfin = pl.pallas_call(
        functools.partial(_rwkv_chunk_kernel, nc=nc),
        grid=(b, 2, nc),
        in_specs=[tok(0), tok(1), tok(2),
                  pl.BlockSpec((RW_L, 256), lambda i, dd, ci: (chunk_row(i, dd, ci), C_LORA // 256)),
                  dirp(1), dirp(64), dirp(1), dirp(64), shared, shared,
                  pl.BlockSpec((None, None, 4, LANES, LANES), lambda i, dd, ci: (i, dd, 0, 0, 0))],
        out_specs=[pl.BlockSpec((None, RW_L, DB), lambda i, dd, ci: (dd, chunk_row(i, dd, ci), 0)),
                   pl.BlockSpec((None, None, 4, LANES, LANES), lambda i, dd, ci: (i, dd, 0, 0, 0))],
        out_shape=[jax.ShapeDtypeStruct((2, t, DB), F32),
                   jax.ShapeDtypeStruct((b, 2, 4, LANES, LANES), F32)],
        scratch_shapes=[pltpu.VMEM((4, LANES, LANES), F32)],
        compiler_params=_cparams(3),
    )(rkv, rkv, rkv, proj, prm['w0'], prm['w_up'], prm['a0'], prm['a_up'], prm['k_k'], prm['k_a'],
      s0_pairs)
    o = pl.pallas_call(
        _rwkv_post_kernel,
        grid=(t // tm,),
        in_specs=[pl.BlockSpec((None, tm, DB), lambda i: (0, i, 0)),
                  pl.BlockSpec((None, tm, DB), lambda i: (1, i, 0)),
                  pl.BlockSpec((tm, 3 * DB), lambda i: (i, 0)),
                  pl.BlockSpec((tm, 256), lambda i: (i, C_LORA // 256)),
                  pl.BlockSpec((128, DB), lambda i: (0, 0)),
                  pl.BlockSpec((1, DB), lambda i: (0, 0)),
                  pl.BlockSpec((1, DB), lambda i: (0, 0)),
                  pl.BlockSpec((1, DB), lambda i: (0, 0))],
        out_specs=pl.BlockSpec((tm, DB), lambda i: (i, 0)),
        out_shape=jax.ShapeDtypeStruct((t, DB), F32),
        compiler_params=_cparams(1),
    )(y2, y2, rkv, proj, prm['g_up'], prm['u'], prm['ln_w'], prm['ln_b'])
    return o, sfin


def _state_to_pairs(s):
    b = s.shape[0]
    st = jnp.swapaxes(s, -1, -2).reshape(b, 2, HEADS // 2, 2, DH, DH)
    z = jnp.zeros((b, 2, HEADS // 2, 2, DH, 2, DH), F32)
    z = z.at[:, :, :, 0, :, 0, :].set(st[:, :, :, 0])
    z = z.at[:, :, :, 1, :, 1, :].set(st[:, :, :, 1])
    return z.reshape(b, 2, HEADS // 2, LANES, LANES)


def _pairs_to_state(z):
    b = z.shape[0]
    z = z.reshape(b, 2, HEADS // 2, 2, DH, 2, DH)
    st = jnp.stack([z[:, :, :, 0, :, 0, :], z[:, :, :, 1, :, 1, :]], axis=3)
    return jnp.swapaxes(st.reshape(b, 2, HEADS, DH, DH), -1, -2)


def _merge_kernel(os5_ref, orw_ref, oatt_ref, gs5_ref, grw_ref, gatt_ref, x_ref, ws5_ref, wrw_ref,
                  watt_ref, wmix_ref, gpost_ref, g1_ref, o_ref):
    merged = (_sigmoid(gs5_ref[...]) * _dot(os5_ref[...].astype(BF16), ws5_ref[...])
              + _sigmoid(grw_ref[...]) * _dot(orw_ref[...].astype(BF16), wrw_ref[...])
              + _sigmoid(gatt_ref[...]) * _dot(oatt_ref[...].astype(BF16), watt_ref[...]))
    m = _dot(merged.astype(BF16), wmix_ref[...])
    o_ref[...] = x_ref[...] + g1_ref[...] * _rms(m, gpost_ref[...])


def _merge(o_s5, o_rw, o_att, proj, x, wts, mod_l, row_of_tile, g_post, tm):
    t = x.shape[0]
    tok = lambda w, c: pl.BlockSpec((tm, w), lambda i: (i, c))
    full = lambda r, c: pl.BlockSpec((r, c), lambda i: (0, 0))
    return pl.pallas_call(
        _merge_kernel,
        grid=(t // tm,),
        in_specs=[tok(DB, 0), tok(DB, 0), tok(DB, 0), tok(D, 0), tok(D, 1), tok(D, 2), tok(D, 0),
                  full(DB, D), full(DB, D), full(DB, D), full(D, D), full(1, D),
                  _mod_spec(2, row_of_tile, 1)],
        out_specs=tok(D, 0),
        out_shape=jax.ShapeDtypeStruct((t, D), F32),
        compiler_params=_cparams(1),
    )(o_s5, o_rw, o_att, proj, proj, proj, x, wts['br_s5'], wts['br_rw'], wts['br_att'],
      wts['mix'], g_post, mod_l)


def _ffn_dense_kernel(x_ref, gpre_ref, sc_ref, sh_ref, w1_ref, w3_ref, w2_ref, gpost_ref, g2_ref,
                      o_ref, h_scr, acc_scr):
    f = pl.program_id(1)

    @pl.when(f == 0)
    def _():
        h = _rms(x_ref[...], gpre_ref[...]) * (1.0 + sc_ref[...]) + sh_ref[...]
        h_scr[...] = h.astype(BF16)
        acc_scr[...] = jnp.zeros_like(acc_scr)

    h = h_scr[...]
    hid = _silu(_dot(h, w1_ref[...])) * _dot(h, w3_ref[...])
    acc_scr[...] += _dot(hid.astype(BF16), w2_ref[...])

    @pl.when(f == pl.num_programs(1) - 1)
    def _():
        o_ref[...] = x_ref[...] + g2_ref[...] * _rms(acc_scr[...], gpost_ref[...])


def _ffn_dense(x, mod_l, row_of_tile, g_pre, g_post, w1, w3, w2, tm, tf):
    t = x.shape[0]
    return pl.pallas_call(
        _ffn_dense_kernel,
        grid=(t // tm, FF_DENSE // tf),
        in_specs=[pl.BlockSpec((tm, D), lambda i, f: (i, 0)),
                  pl.BlockSpec((1, D), lambda i, f: (0, 0)),
                  _mod_spec(4, row_of_tile, 2), _mod_spec(3, row_of_tile, 2),
                  pl.BlockSpec((D, tf), lambda i, f: (0, f)),
                  pl.BlockSpec((D, tf), lambda i, f: (0, f)),
                  pl.BlockSpec((tf, D), lambda i, f: (f, 0)),
                  pl.BlockSpec((1, D), lambda i, f: (0, 0)),
                  _mod_spec(5, row_of_tile, 2)],
        out_specs=pl.BlockSpec((tm, D), lambda i, f: (i, 0)),
        out_shape=jax.ShapeDtypeStruct((t, D), F32),
        scratch_shapes=[pltpu.VMEM((tm, D), BF16), pltpu.VMEM((tm, D), F32)],
        compiler_params=_cparams(2),
    )(x, g_pre, mod_l, mod_l, w1, w3, w2, g_post, mod_l)


def _ffn_moe_kernel(x_ref, gpre_ref, sc_ref, sh_ref, rw_ref, rb_ref, w1_ref, w3_ref, w2_ref,
                    gpost_ref, g2_ref, o_ref, h_scr, comb_scr, acc_scr):
    e = pl.program_id(1)
    lane = lax.broadcasted_iota(jnp.int32, (1, LANES), 1)

    @pl.when(e == 0)
    def _():
        h = _rms(x_ref[...], gpre_ref[...]) * (1.0 + sc_ref[...]) + sh_ref[...]
        h_scr[...] = h.astype(BF16)
        acc_scr[...] = jnp.zeros_like(acc_scr)
        logits = jnp.dot(h, rw_ref[...], precision=HIGHEST, preferred_element_type=F32) + rb_ref[...]
        ex = jnp.exp(logits - jnp.max(logits, axis=-1, keepdims=True))
        probs = ex / jnp.sum(ex, axis=-1, keepdims=True)
        p1 = jnp.max(probs, axis=-1, keepdims=True)
        i1 = jnp.min(jnp.where(probs == p1, lane, LANES), axis=-1, keepdims=True)
        rest = jnp.where(lane == i1, -1.0, probs)
        p2 = jnp.max(rest, axis=-1, keepdims=True)
        i2 = jnp.min(jnp.where(rest == p2, lane, LANES), axis=-1, keepdims=True)
        den = p1 + p2
        comb_scr[...] = jnp.where(lane == i1, p1 / den, 0.0) + jnp.where(lane == i2, p2 / den, 0.0)

    h = h_scr[...]
    cw = jnp.sum(jnp.where(lane == e, comb_scr[...], 0.0), axis=-1, keepdims=True)
    hid = _silu(_dot(h, w1_ref[...])) * _dot(h, w3_ref[...])
    acc_scr[...] += cw * _dot(hid.astype(BF16), w2_ref[...])

    @pl.when(e == N_EXP - 1)
    def _():
        o_ref[...] = x_ref[...] + g2_ref[...] * _rms(acc_scr[...], gpost_ref[...])


def _ffn_moe(x, mod_l, row_of_tile, g_pre, g_post, rw, rb, w1, w3, w2, tm):
    t = x.shape[0]
    return pl.pallas_call(
        _ffn_moe_kernel,
        grid=(t // tm, N_EXP),
        in_specs=[pl.BlockSpec((tm, D), lambda i, e: (i, 0)),
                  pl.BlockSpec((1, D), lambda i, e: (0, 0)),
                  _mod_spec(4, row_of_tile, 2), _mod_spec(3, row_of_tile, 2),
                  pl.BlockSpec((D, LANES), lambda i, e: (0, 0)),
                  pl.BlockSpec((1, LANES), lambda i, e: (0, 0)),
                  pl.BlockSpec((None, D, FF_EXP), lambda i, e: (e, 0, 0)),
                  pl.BlockSpec((None, D, FF_EXP), lambda i, e: (e, 0, 0)),
                  pl.BlockSpec((None, FF_EXP, D), lambda i, e: (e, 0, 0)),
                  pl.BlockSpec((1, D), lambda i, e: (0, 0)),
                  _mod_spec(5, row_of_tile, 2)],
        out_specs=pl.BlockSpec((tm, D), lambda i, e: (i, 0)),
        out_shape=jax.ShapeDtypeStruct((t, D), F32),
        scratch_shapes=[pltpu.VMEM((tm, D), BF16), pltpu.VMEM((tm, LANES), F32),
                        pltpu.VMEM((tm, D), F32)],
        compiler_params=_cparams(2),
    )(x, g_pre, mod_l, mod_l, rw, rb, w1, w3, w2, g_post, mod_l)


def _reorder_w_in(w):
    qkv, u, rkv, lora, gates = (w[:, 0:1536], w[:, 1536:2048], w[:, 2048:3584], w[:, 3584:3840],
                                w[:, 3840:6912])
    return jnp.concatenate([gates, rkv, qkv, u, lora], axis=1)


def _layer(x, b, n, l, mod_l, row_of_tile, P, cache, tm):
    row = lambda a: a.reshape(1, -1)
    proj, qkv = _projection(x, mod_l, row_of_tile, row(P['norm_pre_mix'][l]), P['w_in_b'][l], tm)
    if cache is None:
        o_att = _ctx_attention(qkv, b, n)
        x0re = jnp.zeros((2, S5_G, b, S5_P), F32)
        x0im = x0re
        s0 = jnp.zeros((b, 2, HEADS // 2, LANES, LANES), F32)
    else:
        k_c, v_c, s5re, s5im, rw0 = cache
        bias = _na_bias_tables(P['att_rpb'][l], n // GRID_W)
        o_att = _na_attention(qkv, k_c.reshape(-1, DB).astype(BF16), v_c.reshape(-1, DB).astype(BF16),
                              bias, b, n)
        x0re = jnp.transpose(s5re, (1, 2, 0, 3))
        x0im = jnp.transpose(s5im, (1, 2, 0, 3))
        s0 = _state_to_pairs(rw0)
    o_s5, fre, fim = _s5_mixer(proj, b, n, P['s5'][l], row(P['s5_d'][l]), P['s5_w_glu_b'][l],
                               x0re, x0im, tm)
    o_rw, sfin = _rwkv_mixer(proj, b, n, P['rwkv_conv'][l], P['rwkv'][l], s0, tm)
    x = _merge(o_s5, o_rw, o_att, proj, x, P['merge'][l], mod_l, row_of_tile,
               row(P['norm_post_mix'][l]), tm)
    i = l // 2
    if l % 2 == 0:
        x = _ffn_dense(x, mod_l, row_of_tile, row(P['norm_pre_ffn'][l]), row(P['norm_post_ffn'][l]),
                       P['dense_w1_b'][i], P['dense_w3_b'][i], P['dense_w2_b'][i], tm, FF_DENSE // 2)
    else:
        x = _ffn_moe(x, mod_l, row_of_tile, row(P['norm_pre_ffn'][l]), row(P['norm_post_ffn'][l]),
                     P['moe_rw'][i], P['moe_rb'][i], P['moe_w1_b'][i], P['moe_w3_b'][i],
                     P['moe_w2_b'][i], tm)
    if cache is None:
        k_new = proj[:, C_QKV + DB:C_QKV + 2 * DB]
        v_new = proj[:, C_QKV + 2 * DB:C_QKV + 3 * DB]
        return x, (k_new, v_new, fre, fim, sfin)
    return x, None


def kernel(x_prompt, x_sample, cache_k, cache_v, state_s5_re, state_s5_im, state_rwkv, c, c_ctx,
           w_ada, b_ada, norm_pre_mix, norm_post_mix, norm_pre_ffn, norm_post_ffn, w_in,
           s5_lam_re, s5_lam_im, s5_log_step, s5_b_re, s5_b_im, s5_c_re, s5_c_im, s5_d, s5_w_glu,
           rwkv_conv, rwkv_w0, rwkv_w_up, rwkv_a0, rwkv_a_up, rwkv_g_up, rwkv_k_k, rwkv_k_a,
           rwkv_u, rwkv_ln_w, rwkv_ln_b, att_rpb, w_br_s5, w_br_rwkv, w_br_att, w_mix_out,
           dense_w1, dense_w3, dense_w2, moe_router_w, moe_router_b, moe_w1, moe_w3, moe_w2):
    bc, nc_, _ = x_prompt.shape
    bl, nl, _ = x_sample.shape
    tm_c = min(512, bc * nc_)
    tm_l = min(512, nl)

    P = dict(norm_pre_mix=norm_pre_mix, norm_post_mix=norm_post_mix, norm_pre_ffn=norm_pre_ffn,
             norm_post_ffn=norm_post_ffn, s5_d=s5_d, rwkv_conv=rwkv_conv, att_rpb=att_rpb)
    P['w_in_b'] = [_reorder_w_in(w_in[l]).astype(BF16) for l in range(DEPTH)]
    P['s5_w_glu_b'] = s5_w_glu.astype(BF16)
    P['dense_w1_b'], P['dense_w3_b'], P['dense_w2_b'] = (dense_w1.astype(BF16), dense_w3.astype(BF16),
                                                         dense_w2.astype(BF16))
    P['moe_w1_b'], P['moe_w3_b'], P['moe_w2_b'] = (moe_w1.astype(BF16), moe_w3.astype(BF16),
                                                   moe_w2.astype(BF16))
    n_moe = moe_router_w.shape[0]
    P['moe_rw'] = jnp.pad(moe_router_w, ((0, 0), (0, 0), (0, LANES - N_EXP)))
    P['moe_rb'] = jnp.pad(moe_router_b, ((0, 0), (0, LANES - N_EXP)),
                          constant_values=NEG).reshape(n_moe, 1, LANES)
    P['merge'] = [dict(br_s5=w_br_s5[l].astype(BF16), br_rw=w_br_rwkv[l].astype(BF16),
                       br_att=w_br_att[l].astype(BF16), mix=w_mix_out[l].astype(BF16))
                  for l in range(DEPTH)]
    P['s5'] = []
    P['rwkv'] = []
    for l in range(DEPTH):
        per_dir = [_s5_params(s5_lam_re[l, dd], s5_lam_im[l, dd], s5_log_step[l, dd], s5_b_re[l, dd],
                              s5_b_im[l, dd], s5_c_re[l, dd], s5_c_im[l, dd]) for dd in range(2)]
        P['s5'].append({k: jnp.stack([per_dir[0][k], per_dir[1][k]], axis=0) for k in per_dir[0]})
        P['rwkv'].append(dict(
            w0=rwkv_w0[l].reshape(2, 1, DB), w_up=rwkv_w_up[l], a0=rwkv_a0[l].reshape(2, 1, DB),
            a_up=rwkv_a_up[l], k_k=rwkv_k_k[l].reshape(1, DB), k_a=rwkv_k_a[l].reshape(1, DB),
            g_up=rwkv_g_up[l].astype(BF16), u=rwkv_u[l].reshape(1, DB),
            ln_w=rwkv_ln_w[l].reshape(1, DB), ln_b=rwkv_ln_b[l].reshape(1, DB)))

    cvec = jnp.zeros((8, D), F32).at[0].set(c_ctx).at[1:1 + bl].set(c)
    mod = _modulation(cvec, w_ada, b_ada)

    xp = x_prompt.reshape(bc * nc_, D)
    ks, vs, s5re, s5im, rws = [], [], [], [], []
    for l in range(DEPTH):
        xp, (k_n, v_n, fre, fim, sfin) = _layer(xp, bc, nc_, l, mod[l], lambda i: 0, P, None, tm_c)
        ks.append(k_n.reshape(bc, nc_, HEADS, DH))
        vs.append(v_n.reshape(bc, nc_, HEADS, DH))
        s5re.append(jnp.transpose(fre, (2, 0, 1, 3)))
        s5im.append(jnp.transpose(fim, (2, 0, 1, 3)))
        rws.append(_pairs_to_state(sfin))
    new_k = jnp.stack(ks, axis=1)
    new_v = jnp.stack(vs, axis=1)
    new_s5_re = jnp.stack(s5re, axis=1)
    new_s5_im = jnp.stack(s5im, axis=1)
    new_rwkv = jnp.stack(rws, axis=1)

    xs = x_sample.reshape(bl * nl, D)
    tiles_per_seq = nl // tm_l
    for l in range(DEPTH):
        cache = (cache_k[:, l], cache_v[:, l], state_s5_re[:, l], state_s5_im[:, l], state_rwkv[:, l])
        xs, _ = _layer(xs, bl, nl, l, mod[l], lambda i: 1 + i // tiles_per_seq, P, cache, tm_l)

    return (xp.reshape(bc, nc_, D), xs.reshape(bl, nl, D), new_k, new_v, new_s5_re, new_s5_im,
            new_rwkv)
```

```python
import functools
import math

import numpy as np
import jax
import jax.numpy as jnp
from jax import lax
from jax.experimental import pallas as pl
from jax.experimental.pallas import tpu as pltpu

F32 = jnp.float32
BF16 = jnp.bfloat16
HIGHEST = lax.Precision.HIGHEST

D = 1024
DEPTH = 2
GRID_W = 64
DH = 64
HEADS = 8
DB = 512
WIN_ROWS = 8
WIN_COLS = 16
S5_CH = 16
S5_G = 32
S5_P = 64
S5_J = 16
FF_DENSE = 2816
N_EXP = 8
FF_EXP = 1024
EPS = 1e-6
GN_EPS = 64e-5
NEG = -1e30

C_GATE = 0
C_RKV = 3072
C_QKV = 4608
C_U = 6144
C_LORA = 6656
D_IN = 6912
TN_PROJ = 768

LANES = 128
RW_L = 64
NA_RB = 4
NA_KR = 12
VMEM_LIMIT = 56 * 1024 * 1024


def _cparams(n_axes, vmem=VMEM_LIMIT):
    return pltpu.CompilerParams(dimension_semantics=("arbitrary",) * n_axes,
                                vmem_limit_bytes=vmem)


def _dot(a, b):
    return jnp.dot(a, b, preferred_element_type=F32)


def _dot_nt(a, b):
    return lax.dot_general(a, b, (((1,), (1,)), ((), ())), preferred_element_type=F32)


def _dot_tn(a, b):
    return lax.dot_general(a, b, (((0,), (0,)), ((), ())), preferred_element_type=F32)


def _split2(x):
    hi = x.astype(BF16)
    lo = (x - hi.astype(F32)).astype(BF16)
    return hi, lo


def _split3(x):
    x1 = x.astype(BF16)
    r1 = x - x1.astype(F32)
    x2 = r1.astype(BF16)
    x3 = (r1 - x2.astype(F32)).astype(BF16)
    return x1, x2, x3


def _dot_rhs_exact(x, m):
    x1, x2, x3 = _split3(x)
    return _dot(x1, m) + _dot(x2, m) + _dot(x3, m)


def _dot_lhs_exact(m, x):
    x1, x2, x3 = _split3(x)
    return _dot(m, x1) + _dot(m, x2) + _dot(m, x3)


def _mm3(a, b):
    a1, a2 = _split2(a)
    b1, b2 = _split2(b)
    return _dot(a1, b1) + _dot(a1, b2) + _dot(a2, b1)


def _mm(a, b):
    return _dot(a.astype(BF16), b.astype(BF16))


_mm_inv = _mm
_mm_state = _mm3


def _sigmoid(x):
    return 1.0 / (1.0 + jnp.exp(-x))


def _silu(x):
    return x * _sigmoid(x)


def _softplus(x):
    return jnp.maximum(x, 0.0) + jnp.log(1.0 + jnp.exp(-jnp.abs(x)))


def _rms(x, g):
    return x * lax.rsqrt(jnp.mean(x * x, axis=-1, keepdims=True) + EPS) * g


def _head_block_matrix(width, value):
    r = lax.broadcasted_iota(jnp.int32, (width, width), 0) // DH
    c = lax.broadcasted_iota(jnp.int32, (width, width), 1) // DH
    return jnp.where(r == c, value, 0.0).astype(BF16)


def _mod_kernel(c_ref, w_ref, b_ref, o_ref):
    s = _silu(c_ref[...]).astype(BF16)
    o_ref[...] = _dot(s, w_ref[...].astype(BF16)) + b_ref[...]


def _modulation(cvec, w_ada, b_ada):
    tn = 1536
    out = pl.pallas_call(
        _mod_kernel,
        grid=(DEPTH, 6 * D // tn),
        in_specs=[pl.BlockSpec((8, D), lambda l, j: (0, 0)),
                  pl.BlockSpec((None, D, tn), lambda l, j: (l, 0, j)),
                  pl.BlockSpec((None, 1, tn), lambda l, j: (l, 0, j))],
        out_specs=pl.BlockSpec((None, 8, tn), lambda l, j: (l, 0, j)),
        out_shape=jax.ShapeDtypeStruct((DEPTH, 8, 6 * D), F32),
        compiler_params=_cparams(2),
    )(cvec, w_ada, b_ada.reshape(DEPTH, 1, 6 * D))
    return out.reshape(DEPTH, 8, 1, 6 * D)


def _mod_spec(part, row_of_tile, n_grid):
    if n_grid == 1:
        return pl.BlockSpec((None, 1, D), lambda i: (row_of_tile(i), 0, part))
    return pl.BlockSpec((None, 1, D), lambda i, j: (row_of_tile(i), 0, part))


def _proj_kernel(x_ref, g_ref, sc_ref, sh_ref, w_ref, o_ref, oq_ref, h_scr):
    j = pl.program_id(1)

    @pl.when(j == 0)
    def _():
        h = _rms(x_ref[...], g_ref[...]) * (1.0 + sc_ref[...]) + sh_ref[...]
        h_scr[...] = h.astype(BF16)

    acc = _dot(h_scr[...], w_ref[...])
    o_ref[...] = acc

    @pl.when((j >= C_QKV // TN_PROJ) & (j < C_U // TN_PROJ))
    def _():
        oq_ref[...] = acc.astype(BF16)


def _projection(x, mod_l, row_of_tile, g_pre, w_in_b, tm):
    t = x.shape[0]
    jq = C_QKV // TN_PROJ
    return pl.pallas_call(
        _proj_kernel,
        grid=(t // tm, D_IN // TN_PROJ),
        in_specs=[pl.BlockSpec((tm, D), lambda i, j: (i, 0)),
                  pl.BlockSpec((1, D), lambda i, j: (0, 0)),
                  _mod_spec(1, row_of_tile, 2),
                  _mod_spec(0, row_of_tile, 2),
                  pl.BlockSpec((D, TN_PROJ), lambda i, j: (0, j))],
        out_specs=[pl.BlockSpec((tm, TN_PROJ), lambda i, j: (i, j)),
                   pl.BlockSpec((tm, TN_PROJ), lambda i, j: (i, jnp.clip(j - jq, 0, 1)))],
        out_shape=[jax.ShapeDtypeStruct((t, D_IN), F32),
                   jax.ShapeDtypeStruct((t, 3 * DB), BF16)],
        scratch_shapes=[pltpu.VMEM((tm, D), BF16)],
        compiler_params=_cparams(2),
    )(x, g_pre, mod_l, mod_l, w_in_b)


def _pair_masks():
    lane = lax.broadcasted_iota(jnp.int32, (1, LANES), 1)
    return lane < DH


def _softmax_pv(parts, scale):
    m = None
    for s, _ in parts:
        mx = jnp.max(s, axis=-1, keepdims=True)
        m = mx if m is None else jnp.maximum(m, mx)
    es = [jnp.exp(s - m) for s, _ in parts]
    den = None
    for e in es:
        sm = jnp.sum(e, axis=-1, keepdims=True)
        den = sm if den is None else den + sm
    inv = 1.0 / den
    out = None
    for e, (_, v) in zip(es, parts):
        o = _dot((e * inv).astype(BF16), v)
        out = o if out is None else out + o
    return out


def _ctx_att_kernel(q_ref, k_ref, v_ref, o_ref):
    scale = DH ** -0.5
    m0 = _pair_masks()
    for p in range(HEADS // 2):
        sl = slice(p * LANES, (p + 1) * LANES)
        qp, kp, vp = q_ref[:, sl], k_ref[:, sl], v_ref[:, sl]
        o_pair = None
        for hh in range(2):
            msk = m0 if hh == 0 else jnp.logical_not(m0)
            qm = jnp.where(msk, qp, jnp.zeros_like(qp))
            s = _dot_nt(qm, kp) * scale
            o = _softmax_pv([(s, vp)], scale)
            o_pair = o if o_pair is None else jnp.where(m0, o_pair, o)
        o_ref[:, sl] = o_pair


def _ctx_attention(qkv, b, n):
    return pl.pallas_call(
        _ctx_att_kernel,
        grid=(b,),
        in_specs=[pl.BlockSpec((n, DB), lambda i: (i, 0)),
                  pl.BlockSpec((n, DB), lambda i: (i, 1)),
                  pl.BlockSpec((n, DB), lambda i: (i, 2))],
        out_specs=pl.BlockSpec((n, DB), lambda i: (i, 0)),
        out_shape=jax.ShapeDtypeStruct((b * n, DB), F32),
        compiler_params=_cparams(1),
    )(qkv, qkv, qkv)


def _na_key_start(rb, rows):
    return jnp.clip(rb * NA_RB - WIN_ROWS // 2, 0, rows - NA_KR)


def _na_kernel(q_ref, k_ref, v_ref, kc_ref, vc_ref, bias_ref, o_ref, *, rows):
    scale = DH ** -0.5
    m0 = _pair_masks()
    rb = pl.program_id(1)
    start = pl.multiple_of(_na_key_start(rb, rows) * GRID_W, GRID_W)
    nk = NA_KR * GRID_W
    for p in range(HEADS // 2):
        sl = slice(p * LANES, (p + 1) * LANES)
        qp = q_ref[:, sl]
        kp = k_ref[pl.ds(start, nk), sl]
        vp = v_ref[pl.ds(start, nk), sl]
        kcp, vcp = kc_ref[:, sl], vc_ref[:, sl]
        o_pair = None
        for hh in range(2):
            msk = m0 if hh == 0 else jnp.logical_not(m0)
            qm = jnp.where(msk, qp, jnp.zeros_like(qp))
            s_loc = _dot_nt(qm, kp) * scale + bias_ref[2 * p + hh]
            s_ctx = _dot_nt(qm, kcp) * scale
            o = _softmax_pv([(s_loc, vp), (s_ctx, vcp)], scale)
            o_pair = o if o_pair is None else jnp.where(m0, o_pair, o)
        o_ref[:, sl] = o_pair


def _na_bias_tables(rpb, rows):
    n_rb = rows // NA_RB
    wr = min(WIN_ROWS, rows)
    nrel_r, nrel_c = 2 * WIN_ROWS - 1, 2 * WIN_COLS - 1
    qc = np.arange(GRID_W)[:, None]
    kc = np.arange(GRID_W)[None, :]
    cs = np.clip(qc - WIN_COLS // 2, 0, GRID_W - WIN_COLS)
    col_ok = (kc >= cs) & (kc < cs + WIN_COLS)
    col_rel = np.clip(kc - qc + (WIN_COLS - 1), 0, nrel_c - 1)
    col_hot = (np.arange(nrel_c)[:, None, None] == col_rel[None]) & col_ok[None]
    col_hot = jnp.asarray(col_hot.reshape(nrel_c, GRID_W * GRID_W), F32)
    row_hot = np.zeros((3, NA_RB, NA_KR, nrel_r), np.float32)
    row_ok = np.zeros((3, NA_RB, NA_KR), bool)
    for vi, rb in enumerate((0, 1, n_rb - 1)):
        u0 = int(np.clip(rb * NA_RB - WIN_ROWS // 2, 0, rows - NA_KR))
        for qr in range(NA_RB):
            r = rb * NA_RB + qr
            rs = int(np.clip(r - wr // 2, 0, rows - wr))
            for kr in range(NA_KR):
                krow = u0 + kr
                if rs <= krow < rs + wr:
                    row_ok[vi, qr, kr] = True
                    row_hot[vi, qr, kr, krow - r + (WIN_ROWS - 1)] = 1.0
    t1 = jnp.einsum('vqkr,hrc->vhqkc', jnp.asarray(row_hot), rpb, precision=HIGHEST)
    t2 = jnp.dot(t1.reshape(-1, nrel_c), col_hot, precision=HIGHEST)
    t2 = t2.reshape(3, HEADS, NA_RB, NA_KR, GRID_W, GRID_W)
    ok = jnp.asarray(row_ok[:, None, :, :, None, None] & col_ok[None, None, None, None])
    t2 = jnp.where(ok, t2, NEG)
    return jnp.transpose(t2, (0, 1, 2, 4, 3, 5)).reshape(3, HEADS, NA_RB * GRID_W, NA_KR * GRID_W)


def _na_attention(qkv, k_ctx, v_ctx, bias, b, n):
    rows = n // GRID_W
    n_rb = rows // NA_RB
    tq = NA_RB * GRID_W
    lc = k_ctx.shape[0] // b

    def variant(j):
        return jnp.where(j == 0, 0, jnp.where(j == n_rb - 1, 2, 1))

    return pl.pallas_call(
        functools.partial(_na_kernel, rows=rows),
        grid=(b, n_rb),
        in_specs=[pl.BlockSpec((tq, DB), lambda i, j: (i * n_rb + j, 0)),
                  pl.BlockSpec((n, DB), lambda i, j: (i, 1)),
                  pl.BlockSpec((n, DB), lambda i, j: (i, 2)),
                  pl.BlockSpec((lc, DB), lambda i, j: (i, 0)),
                  pl.BlockSpec((lc, DB), lambda i, j: (i, 0)),
                  pl.BlockSpec((None, HEADS, tq, NA_KR * GRID_W), lambda i, j: (variant(j), 0, 0, 0))],
        out_specs=pl.BlockSpec((tq, DB), lambda i, j: (i * n_rb + j, 0)),
        out_shape=jax.ShapeDtypeStruct((b * n, DB), F32),
        compiler_params=_cparams(2),
    )(qkv, qkv, qkv, k_ctx, v_ctx, bias)


def _s5_params(lam_re, lam_im, log_step, b_re, b_im, c_re, c_im):
    hp = dict(precision=HIGHEST)
    step = jnp.exp(log_step)[:, None]
    lam = lax.complex(lam_re, lam_im)
    lam_bar = jnp.exp(lam * step)
    b_bar = ((lam_bar - 1.0) / lam)[..., None] * lax.complex(b_re, b_im)
    cmat = lax.complex(c_re, c_im)
    dd = jnp.arange(S5_J + 1, dtype=F32)[:, None, None]
    lam_pow = jnp.exp(dd * (lam * step)[None])
    bx = lam_pow[:S5_J][::-1][:, :, :, None] * b_bar[None]
    bx = jnp.transpose(bx, (1, 0, 3, 2)).reshape(S5_G, S5_J * S5_CH, S5_P)
    kd = jnp.einsum('gcp,dgp,gpk->gdck', cmat, lam_pow[:S5_J], b_bar, **hp).real
    tau = np.arange(S5_J)
    dmat = tau[None, :] - tau[:, None]
    toep = kd[:, np.clip(dmat, 0, S5_J - 1)]
    toep = jnp.where((dmat >= 0)[None, :, :, None, None], toep, 0.0)
    toep = jnp.transpose(toep, (0, 1, 4, 2, 3)).reshape(S5_G, S5_J * S5_CH, S5_J * S5_CH)
    cm = cmat[:, None] * lam_pow[1:][:, :, None, :].transpose(1, 0, 2, 3)
    cx = jnp.transpose(cm, (0, 3, 1, 2)).reshape(S5_G, S5_P, S5_J * S5_CH)
    lam_j = lam_pow[S5_J]
    b1 = jnp.pad(jnp.transpose(b_bar, (0, 2, 1)), ((0, 0), (0, (S5_J - 1) * S5_CH), (0, 0)))
    return dict(bx_re=bx.real, bx_im=bx.imag, toep=toep, cx_re=cx.real, cx_im=-cx.imag,
                lj_re=lam_j.real[:, None, :], lj_im=lam_j.imag[:, None, :],
                l1_re=lam_bar.real[:, None, :], l1_im=lam_bar.imag[:, None, :],
                b1_re=b1.real, b1_im=b1.imag)


def _s5_kernel(u_ref, bre_ref, bim_ref, toep_ref, cre_ref, cim_ref, ljre_ref, ljim_ref,
               l1re_ref, l1im_ref, b1re_ref, b1im_ref, x0re_ref, x0im_ref,
               y_ref, fre_ref, fim_ref, zre_scr, zim_scr, *, nsub, bp):
    u = u_ref[...]
    zre_scr[...] = jnp.dot(u, bre_ref[...], precision=HIGHEST, preferred_element_type=F32)
    zim_scr[...] = jnp.dot(u, bim_ref[...], precision=HIGHEST, preferred_element_type=F32)
    u_first = u_ref[0:bp, :]
    x0re, x0im = x0re_ref[...], x0im_ref[...]
    fre_ref[...] = (l1re_ref[...] * x0re - l1im_ref[...] * x0im
                    + jnp.dot(u_first, b1re_ref[...], precision=HIGHEST, preferred_element_type=F32))
    fim_ref[...] = (l1re_ref[...] * x0im + l1im_ref[...] * x0re
                    + jnp.dot(u_first, b1im_ref[...], precision=HIGHEST, preferred_element_type=F32))
    lre = jnp.broadcast_to(ljre_ref[...], (bp, S5_P))
    lim = jnp.broadcast_to(ljim_ref[...], (bp, S5_P))

    def step(j, carry):
        xre, xim = carry
        rows = pl.ds(pl.multiple_of(j * bp, bp), bp)
        zre, zim = zre_scr[rows, :], zim_scr[rows, :]
        zre_scr[rows, :] = xre
        zim_scr[rows, :] = xim
        return (lre * xre - lim * xim + zre, lre * xim + lim * xre + zim)

    lax.fori_loop(0, nsub, step, (x0re, x0im))
    ub = u.astype(BF16)
    y_ref[...] = (_dot(ub, toep_ref[...].astype(BF16))
                  + _mm(zre_scr[...], cre_ref[...]) + _mm(zim_scr[...], cim_ref[...]))


def _s5_scan(u2, prm, x0re, x0im, nsub, bp):
    r = nsub * bp
    jc = S5_J * S5_CH
    gspec = lambda shp: pl.BlockSpec((None, None) + shp, lambda d, g: (d, g, 0, 0))
    return pl.pallas_call(
        functools.partial(_s5_kernel, nsub=nsub, bp=bp),
        grid=(2, S5_G),
        in_specs=[gspec((r, jc)), gspec((jc, S5_P)), gspec((jc, S5_P)), gspec((jc, jc)),
                  gspec((S5_P, jc)), gspec((S5_P, jc)), gspec((1, S5_P)), gspec((1, S5_P)),
                  gspec((1, S5_P)), gspec((1, S5_P)), gspec((jc, S5_P)), gspec((jc, S5_P)),
                  gspec((bp, S5_P)), gspec((bp, S5_P))],
        out_specs=[gspec((r, jc)), gspec((bp, S5_P)), gspec((bp, S5_P))],
        out_shape=[jax.ShapeDtypeStruct((2, S5_G, r, jc), F32),
                   jax.ShapeDtypeStruct((2, S5_G, bp, S5_P), F32),
                   jax.ShapeDtypeStruct((2, S5_G, bp, S5_P), F32)],
        scratch_shapes=[pltpu.VMEM((r, S5_P), F32), pltpu.VMEM((r, S5_P), F32)],
        compiler_params=_cparams(2),
    )(u2, prm['bx_re'], prm['bx_im'], prm['toep'], prm['cx_re'], prm['cx_im'],
      prm['lj_re'], prm['lj_im'], prm['l1_re'], prm['l1_im'], prm['b1_re'], prm['b1_im'], x0re, x0im)


def _s5_post_kernel(u_ref, yf_ref, yb_ref, d_ref, w_ref, o_ref):
    y = jax.nn.gelu(d_ref[...] * u_ref[...] + yf_ref[...] + yb_ref[...])
    o_ref[...] = y * _sigmoid(_dot(y.astype(BF16), w_ref[...]))


def _s5_mixer(proj, b, n, sp, s5_d, w_glu_b, x0re, x0im, tm):
    t = b * n
    nsub = n // S5_J
    bp = max(b, 8)
    u = proj[:, C_U:C_U + DB].reshape(b, nsub, S5_J, S5_G, S5_CH)
    u = jnp.stack([u, u[:, ::-1, ::-1]], axis=0)
    u = jnp.transpose(u, (0, 4, 2, 1, 3, 5))
    if bp != b:
        u = jnp.pad(u, ((0, 0), (0, 0), (0, 0), (0, bp - b), (0, 0), (0, 0)))
        x0re = jnp.pad(x0re, ((0, 0), (0, 0), (0, bp - b), (0, 0)))
        x0im = jnp.pad(x0im, ((0, 0), (0, 0), (0, bp - b), (0, 0)))
    u2 = u.reshape(2, S5_G, nsub * bp, S5_J * S5_CH)
    y2, fre, fim = _s5_scan(u2, sp, x0re, x0im, nsub, bp)
    y2 = y2.reshape(2, S5_G, nsub, bp, S5_J, S5_CH)[:, :, :, :b]
    y2 = jnp.transpose(y2, (0, 3, 2, 4, 1, 5))
    yf = y2[0].reshape(t, DB)
    yb = y2[1][:, ::-1, ::-1].reshape(t, DB)
    o = pl.pallas_call(
        _s5_post_kernel,
        grid=(t // tm,),
        in_specs=[pl.BlockSpec((tm, DB), lambda i: (i, C_U // DB)),
                  pl.BlockSpec((tm, DB), lambda i: (i, 0)),
                  pl.BlockSpec((tm, DB), lambda i: (i, 0)),
                  pl.BlockSpec((1, DB), lambda i: (0, 0)),
                  pl.BlockSpec((DB, DB), lambda i: (0, 0))],
        out_specs=pl.BlockSpec((tm, DB), lambda i: (i, 0)),
        out_shape=jax.ShapeDtypeStruct((t, DB), F32),
        compiler_params=_cparams(1),
    )(proj, yf, yb, s5_d, w_glu_b)
    return o, fre[:, :, :b], fim[:, :, :b]


def _conv_kernel(x_ref, prev_ref, next_ref, w_ref, o_ref, *, tiles_per_seq):
    i = pl.program_id(0)
    x = x_ref[...]
    tm = x.shape[0]
    row = lax.broadcasted_iota(jnp.int32, (tm, 1), 0)
    first = (i % tiles_per_seq) == 0
    last = (i % tiles_per_seq) == tiles_per_seq - 1
    prev_row = jnp.where(first, 0.0, prev_ref[7:8, :])
    next_row = jnp.where(last, 0.0, next_ref[0:1, :])
    x_dn = jnp.where(row == 0, prev_row, pltpu.roll(x, 1, axis=0))
    x_up = jnp.where(row == tm - 1, next_row, pltpu.roll(x, tm - 1, axis=0))
    o_ref[...] = x_dn * w_ref[0:1, :] + x * w_ref[1:2, :] + x_up * w_ref[2:3, :]


def _rwkv_conv(proj, conv_w, n, tm):
    t = proj.shape[0]
    w3 = 3 * DB
    cb = C_RKV // w3
    nb8 = t // 8
    return pl.pallas_call(
        functools.partial(_conv_kernel, tiles_per_seq=n // tm),
        grid=(t // tm,),
        in_specs=[pl.BlockSpec((tm, w3), lambda i: (i, cb)),
                  pl.BlockSpec((8, w3), lambda i: (jnp.maximum(i * (tm // 8) - 1, 0), cb)),
                  pl.BlockSpec((8, w3), lambda i: (jnp.minimum((i + 1) * (tm // 8), nb8 - 1), cb)),
                  pl.BlockSpec((3, w3), lambda i: (0, 0))],
        out_specs=pl.BlockSpec((tm, w3), lambda i: (i, 0)),
        out_shape=jax.ShapeDtypeStruct((t, w3), F32),
        compiler_params=_cparams(1),
    )(proj, proj, proj, conv_w)


def _rwkv_chunk_kernel(rc_ref, kc_ref, vc_ref, lora_ref, w0_ref, wup_ref, a0_ref, aup_ref,
                       kk_ref, ka_ref, s0_ref, y_ref, sfin_ref, z_scr, *, nc):
    d = pl.program_id(1)
    ci = pl.program_id(2)
    L = RW_L

    @pl.when(ci == 0)
    def _():
        z_scr[...] = s0_ref[...]

    rc, kc, vc = rc_ref[...], kc_ref[...], vc_ref[...]
    wd = lora_ref[:, 0:64]
    ad = lora_ref[:, 64:128]
    lora_w = jnp.dot(jnp.tanh(wd), wup_ref[...], precision=HIGHEST, preferred_element_type=F32)
    log_w = -_softplus(-(w0_ref[...] + lora_w)) - 0.5
    lw = -jnp.exp(log_w)
    a = _sigmoid(a0_ref[...] + jnp.dot(ad, aup_ref[...], precision=HIGHEST,
                                       preferred_element_type=F32))
    kd = kc * (1.0 + (a - 1.0) * ka_ref[...])
    kk = kc * kk_ref[...]
    kk = kk * lax.rsqrt(_dot_rhs_exact(kk * kk, _head_block_matrix(DB, 1.0)) + 1e-12)
    alpha = -kk
    beta = kk * a

    sgn = 1 - 2 * d
    tt = lax.broadcasted_iota(jnp.int32, (L, L), 0)
    ss = lax.broadcasted_iota(jnp.int32, (L, L), 1)
    tri = jnp.where((tt - ss) * sgn >= 0, 1.0, 0.0).astype(BF16)
    c = _dot_lhs_exact(tri, lw)
    c_ex = c - lw
    ctot = jnp.sum(lw, axis=0, keepdims=True)
    mid = 0.5 * ctot
    e_in = jnp.exp(c - mid)
    e_ex = jnp.exp(c_ex - mid)
    e_out = jnp.exp(mid - c)
    e_mid = jnp.exp(mid)
    al_t = alpha * e_ex
    r_t = rc * e_in
    be_t = beta * e_out
    k_t = kd * e_out
    a0s = al_t * e_mid
    r0s = r_t * e_mid
    bps = be_t * e_mid
    kps = k_t * e_mid
    p_l = e_mid * e_mid

    m0 = _pair_masks()
    t2 = lax.broadcasted_iota(jnp.int32, (2 * L, 2 * L), 0) % L
    s2 = lax.broadcasted_iota(jnp.int32, (2 * L, 2 * L), 1) % L
    strict = (t2 - s2) * sgn > 0
    incl = (t2 - s2) * sgn >= 0
    eye = (lax.broadcasted_iota(jnp.int32, (LANES, LANES), 0)
           == lax.broadcasted_iota(jnp.int32, (LANES, LANES), 1))
    eye_f = jnp.where(eye, 1.0, 0.0)
    zeros = jnp.zeros((LANES, LANES), F32)

    pairs = range(HEADS // 2)
    sls = [slice(p * LANES, (p + 1) * LANES) for p in pairs]

    def stack(x, p):
        xp = x[:, sls[p]]
        return jnp.concatenate([jnp.where(m0, xp, 0.0), jnp.where(m0, 0.0, xp)], axis=0)

    raws = [_dot_nt(jnp.concatenate([stack(al_t, p), stack(r_t, p)], axis=0).astype(BF16),
                    jnp.concatenate([stack(be_t, p), stack(k_t, p)], axis=0).astype(BF16))
            for p in pairs]
    amat = [jnp.where(strict, r[:LANES, :LANES], 0.0) for r in raws]
    bmat = [jnp.where(strict, r[:LANES, LANES:], 0.0) for r in raws]
    qbk = [jnp.concatenate([jnp.where(incl, r[LANES:, :LANES], 0.0),
                            jnp.where(incl, r[LANES:, LANES:], 0.0)], axis=1) for r in raws]

    smat = [eye_f + a for a in amat]
    pw = [_mm_inv(a, a) for a in amat]
    for _ in range(4):
        xs = [_mm_inv(pw[p], jnp.concatenate([pw[p], smat[p]], axis=1)) for p in pairs]
        smat = [smat[p] + xs[p][:, LANES:] for p in pairs]
        pw = [x[:, :LANES] for x in xs]
    tmat = [smat[p] + _mm_inv(pw[p], smat[p]) for p in pairs]

    vs = [stack(vc, p) for p in pairs]
    bv = [_mm(bmat[p], vs[p]) for p in pairs]
    wu = [_mm_inv(tmat[p], jnp.concatenate([stack(a0s, p), bv[p]], axis=1)) for p in pairs]
    rhs2 = [jnp.concatenate([wu[p], jnp.concatenate([zeros, vs[p]], axis=1)], axis=0).astype(BF16)
            for p in pairs]
    out_a = [_dot(qbk[p].astype(BF16), rhs2[p]) for p in pairs]
    out_b = [_dot_tn(jnp.concatenate([stack(bps, p), stack(kps, p)], axis=0).astype(BF16), rhs2[p])
             for p in pairs]
    hz = [_mm_state(jnp.concatenate(
        [stack(r0s, p) + out_a[p][:, :LANES],
         jnp.where(eye, p_l[:, sls[p]], 0.0) + out_b[p][:, :LANES]], axis=0), z_scr[p]) for p in pairs]
    for p in pairs:
        y = out_a[p][:, LANES:] + hz[p][:LANES]
        z_scr[p] = hz[p][LANES:] + out_b[p][:, LANES:]
        y_ref[:, sls[p]] = y[:L] + y[L:]

    @pl.when(ci == nc - 1)
    def _():
        sfin_ref[...] = z_scr[...]


def _rwkv_post_kernel(yf_ref, yb_ref, rkv_ref, gd_ref, gup_ref, u_ref, lnw_ref, lnb_ref, o_ref):
    y = yf_ref[...] + yb_ref[...]
    avg = _head_block_matrix(DB, 1.0 / DH)
    mu = _dot_rhs_exact(y, avg)
    yc = y - mu
    var = _dot_rhs_exact(yc * yc, avg)
    yn = yc * lax.rsqrt(var + GN_EPS) * lnw_ref[...] + lnb_ref[...]
    rc, kc, vc = rkv_ref[:, 0:DB], rkv_ref[:, DB:2 * DB], rkv_ref[:, 2 * DB:3 * DB]
    bonus = _dot_rhs_exact(rc * kc * u_ref[...], _head_block_matrix(DB, 1.0)) * vc
    g = _dot(_sigmoid(gd_ref[:, 128:256]).astype(BF16), gup_ref[...])
    o_ref[...] = (yn + bonus) * g


def _rwkv_mixer(proj, b, n, conv_w, prm, s0_pairs, tm):
    t = b * n
    nc = n // RW_L
    rkv = _rwkv_conv(proj, conv_w, n, min(tm, n))

    def chunk_row(i, dd, ci):
        return i * nc + jnp.where(dd == 0, ci, nc - 1 - ci)

    tok = lambda col: pl.BlockSpec((RW_L, DB), lambda i, dd, ci: (chunk_row(i, dd, ci), col))
    dirp = lambda r: pl.BlockSpec((None, r, DB), lambda i, dd, ci: (dd, 0, 0))
    shared = pl.BlockSpec((1, DB), lambda i, dd, ci: (0, 0))
    y2, sfin = pl.pallas_call(
        functools.partial(_rwkv_chunk_kernel, nc=nc),
        grid=(b, 2, nc),
        in_specs=[tok(0), tok(1), tok(2),
                  pl.BlockSpec((RW_L, 256), lambda i, dd, ci: (chunk_row(i, dd, ci), C_LORA // 256)),
                  dirp(1), dirp(64), dirp(1), dirp(64), shared, shared,
                  pl.BlockSpec((None, None, 4, LANES, LANES), lambda i, dd, ci: (i, dd, 0, 0, 0))],
        out_specs=[pl.BlockSpec((None, RW_L, DB), lambda i, dd, ci: (dd, chunk_row(i, dd, ci), 0)),
                   pl.BlockSpec((None, None, 4, LANES, LANES), lambda i, dd, ci: (i, dd, 0, 0, 0))],
        out_shape=[jax.ShapeDtypeStruct((2, t, DB), F32),
                   jax.ShapeDtypeStruct((b, 2, 4, LANES, LANES), F32)],
        scratch_shapes=[pltpu.VMEM((4, LANES, LANES), F32)],
        compiler_params=_cparams(3),
    )(rkv, rkv, rkv, proj, prm['w0'], prm['w_up'], prm['a0'], prm['a_up'], prm['k_k'], prm['k_a'],
      s0_pairs)
    o = pl.pallas_call(
        _rwkv_post_kernel,
        grid=(t // tm,),
        in_specs=[pl.BlockSpec((None, tm, DB), lambda i: (0, i, 0)),
                  pl.BlockSpec((None, tm, DB), lambda i: (1, i, 0)),
                  pl.BlockSpec((tm, 3 * DB), lambda i: (i, 0)),
                  pl.BlockSpec((tm, 256), lambda i: (i, C_LORA // 256)),
                  pl.BlockSpec((128, DB), lambda i: (0, 0)),
                  pl.BlockSpec((1, DB), lambda i: (0, 0)),
                  pl.BlockSpec((1, DB), lambda i: (0, 0)),
                  pl.BlockSpec((1, DB), lambda i: (0, 0))],
        out_specs=pl.BlockSpec((tm, DB), lambda i: (i, 0)),
        out_shape=jax.ShapeDtypeStruct((t, DB), F32),
        compiler_params=_cparams(1),
    )(y2, y2, rkv, proj, prm['g_up'], prm['u'], prm['ln_w'], prm['ln_b'])
    return o, sfin


def _state_to_pairs(s):
    b = s.shape[0]
    st = jnp.swapaxes(s, -1, -2).reshape(b, 2, HEADS // 2, 2, DH, DH)
    z = jnp.zeros((b, 2, HEADS // 2, 2, DH, 2, DH), F32)
    z = z.at[:, :, :, 0, :, 0, :].set(st[:, :, :, 0])
    z = z.at[:, :, :, 1, :, 1, :].set(st[:, :, :, 1])
    return z.reshape(b, 2, HEADS // 2, LANES, LANES)


def _pairs_to_state(z):
    b = z.shape[0]
    z = z.reshape(b, 2, HEADS // 2, 2, DH, 2, DH)
    st = jnp.stack([z[:, :, :, 0, :, 0, :], z[:, :, :, 1, :, 1, :]], axis=3)
    return jnp.swapaxes(st.reshape(b, 2, HEADS, DH, DH), -1, -2)


def _merge_kernel(os5_ref, orw_ref, oatt_ref, gs5_ref, grw_ref, gatt_ref, x_ref, ws5_ref, wrw_ref,
                  watt_ref, wmix_ref, gpost_ref, g1_ref, o_ref):
    merged = (_sigmoid(gs5_ref[...]) * _dot(os5_ref[...].astype(BF16), ws5_ref[...])
              + _sigmoid(grw_ref[...]) * _dot(orw_ref[...].astype(BF16), wrw_ref[...])
              + _sigmoid(gatt_ref[...]) * _dot(oatt_ref[...].astype(BF16), watt_ref[...]))
    m = _dot(merged.astype(BF16), wmix_ref[...])
    o_ref[...] = x_ref[...] + g1_ref[...] * _rms(m, gpost_ref[...])


def _merge(o_s5, o_rw, o_att, proj, x, wts, mod_l, row_of_tile, g_post, tm):
    t = x.shape[0]
    tok = lambda w, c: pl.BlockSpec((tm, w), lambda i: (i, c))
    full = lambda r, c: pl.BlockSpec((r, c), lambda i: (0, 0))
    return pl.pallas_call(
        _merge_kernel,
        grid=(t // tm,),
        in_specs=[tok(DB, 0), tok(DB, 0), tok(DB, 0), tok(D, 0), tok(D, 1), tok(D, 2), tok(D, 0),
                  full(DB, D), full(DB, D), full(DB, D), full(D, D), full(1, D),
                  _mod_spec(2, row_of_tile, 1)],
        out_specs=tok(D, 0),
        out_shape=jax.ShapeDtypeStruct((t, D), F32),
        compiler_params=_cparams(1),
    )(o_s5, o_rw, o_att, proj, proj, proj, x, wts['br_s5'], wts['br_rw'], wts['br_att'],
      wts['mix'], g_post, mod_l)


def _ffn_dense_kernel(x_ref, gpre_ref, sc_ref, sh_ref, w1_ref, w3_ref, w2_ref, gpost_ref, g2_ref,
                      o_ref, h_scr, acc_scr):
    f = pl.program_id(1)

    @pl.when(f == 0)
    def _():
        h = _rms(x_ref[...], gpre_ref[...]) * (1.0 + sc_ref[...]) + sh_ref[...]
        h_scr[...] = h.astype(BF16)
        acc_scr[...] = jnp.zeros_like(acc_scr)

    h = h_scr[...]
    hid = _silu(_dot(h, w1_ref[...])) * _dot(h, w3_ref[...])
    acc_scr[...] += _dot(hid.astype(BF16), w2_ref[...])

    @pl.when(f == pl.num_programs(1) - 1)
    def _():
        o_ref[...] = x_ref[...] + g2_ref[...] * _rms(acc_scr[...], gpost_ref[...])


def _ffn_dense(x, mod_l, row_of_tile, g_pre, g_post, w1, w3, w2, tm, tf):
    t = x.shape[0]
    return pl.pallas_call(
        _ffn_dense_kernel,
        grid=(t // tm, FF_DENSE // tf),
        in_specs=[pl.BlockSpec((tm, D), lambda i, f: (i, 0)),
                  pl.BlockSpec((1, D), lambda i, f: (0, 0)),
                  _mod_spec(4, row_of_tile, 2), _mod_spec(3, row_of_tile, 2),
                  pl.BlockSpec((D, tf), lambda i, f: (0, f)),
                  pl.BlockSpec((D, tf), lambda i, f: (0, f)),
                  pl.BlockSpec((tf, D), lambda i, f: (f, 0)),
                  pl.BlockSpec((1, D), lambda i, f: (0, 0)),
                  _mod_spec(5, row_of_tile, 2)],
        out_specs=pl.BlockSpec((tm, D), lambda i, f: (i, 0)),
        out_shape=jax.ShapeDtypeStruct((t, D), F32),
        scratch_shapes=[pltpu.VMEM((tm, D), BF16), pltpu.VMEM((tm, D), F32)],
        compiler_params=_cparams(2),
    )(x, g_pre, mod_l, mod_l, w1, w3, w2, g_post, mod_l)


def _ffn_moe_kernel(x_ref, gpre_ref, sc_ref, sh_ref, rw_ref, rb_ref, w1_ref, w3_ref, w2_ref,
                    gpost_ref, g2_ref, o_ref, h_scr, comb_scr, acc_scr):
    e = pl.program_id(1)
    lane = lax.broadcasted_iota(jnp.int32, (1, LANES), 1)

    @pl.when(e == 0)
    def _():
        h = _rms(x_ref[...], gpre_ref[...]) * (1.0 + sc_ref[...]) + sh_ref[...]
        h_scr[...] = h.astype(BF16)
        acc_scr[...] = jnp.zeros_like(acc_scr)
        logits = jnp.dot(h, rw_ref[...], precision=HIGHEST, preferred_element_type=F32) + rb_ref[...]
        ex = jnp.exp(logits - jnp.max(logits, axis=-1, keepdims=True))
        probs = ex / jnp.sum(ex, axis=-1, keepdims=True)
        p1 = jnp.max(probs, axis=-1, keepdims=True)
        i1 = jnp.min(jnp.where(probs == p1, lane, LANES), axis=-1, keepdims=True)
        rest = jnp.where(lane == i1, -1.0, probs)
        p2 = jnp.max(rest, axis=-1, keepdims=True)
        i2 = jnp.min(jnp.where(rest == p2, lane, LANES), axis=-1, keepdims=True)
        den = p1 + p2
        comb_scr[...] = jnp.where(lane == i1, p1 / den, 0.0) + jnp.where(lane == i2, p2 / den, 0.0)

    h = h_scr[...]
    cw = jnp.sum(jnp.where(lane == e, comb_scr[...], 0.0), axis=-1, keepdims=True)
    hid = _silu(_dot(h, w1_ref[...])) * _dot(h, w3_ref[...])
    acc_scr[...] += cw * _dot(hid.astype(BF16), w2_ref[...])

    @pl.when(e == N_EXP - 1)
    def _():
        o_ref[...] = x_ref[...] + g2_ref[...] * _rms(acc_scr[...], gpost_ref[...])


def _ffn_moe(x, mod_l, row_of_tile, g_pre, g_post, rw, rb, w1, w3, w2, tm):
    t = x.shape[0]
    return pl.pallas_call(
        _ffn_moe_kernel,
        grid=(t // tm, N_EXP),
        in_specs=[pl.BlockSpec((tm, D), lambda i, e: (i, 0)),
                  pl.BlockSpec((1, D), lambda i, e: (0, 0)),
                  _mod_spec(4, row_of_tile, 2), _mod_spec(3, row_of_tile, 2),
                  pl.BlockSpec((D, LANES), lambda i, e: (0, 0)),
                  pl.BlockSpec((1, LANES), lambda i, e: (0, 0)),
                  pl.BlockSpec((None, D, FF_EXP), lambda i, e: (e, 0, 0)),
                  pl.BlockSpec((None, D, FF_EXP), lambda i, e: (e, 0, 0)),
                  pl.BlockSpec((None, FF_EXP, D), lambda i, e: (e, 0, 0)),
                  pl.BlockSpec((1, D), lambda i, e: (0, 0)),
                  _mod_spec(5, row_of_tile, 2)],
        out_specs=pl.BlockSpec((tm, D), lambda i, e: (i, 0)),
        out_shape=jax.ShapeDtypeStruct((t, D), F32),
        scratch_shapes=[pltpu.VMEM((tm, D), BF16), pltpu.VMEM((tm, LANES), F32),
                        pltpu.VMEM((tm, D), F32)],
        compiler_params=_cparams(2),
    )(x, g_pre, mod_l, mod_l, rw, rb, w1, w3, w2, g_post, mod_l)


def _reorder_w_in(w):
    qkv, u, rkv, lora, gates = (w[:, 0:1536], w[:, 1536:2048], w[:, 2048:3584], w[:, 3584:3840],
                                w[:, 3840:6912])
    return jnp.concatenate([gates, rkv, qkv, u, lora], axis=1)


def _layer(x, b, n, l, mod_l, row_of_tile, P, cache, tm):
    row = lambda a: a.reshape(1, -1)
    proj, qkv = _projection(x, mod_l, row_of_tile, row(P['norm_pre_mix'][l]), P['w_in_b'][l], tm)
    if cache is None:
        o_att = _ctx_attention(qkv, b, n)
        x0re = jnp.zeros((2, S5_G, b, S5_P), F32)
        x0im = x0re
        s0 = jnp.zeros((b, 2, HEADS // 2, LANES, LANES), F32)
    else:
        k_c, v_c, s5re, s5im, rw0 = cache
        bias = _na_bias_tables(P['att_rpb'][l], n // GRID_W)
        o_att = _na_attention(qkv, k_c.reshape(-1, DB).astype(BF16), v_c.reshape(-1, DB).astype(BF16),
                              bias, b, n)
        x0re = jnp.transpose(s5re, (1, 2, 0, 3))
        x0im = jnp.transpose(s5im, (1, 2, 0, 3))
        s0 = _state_to_pairs(rw0)
    o_s5, fre, fim = _s5_mixer(proj, b, n, P['s5'][l], row(P['s5_d'][l]), P['s5_w_glu_b'][l],
                               x0re, x0im, tm)
    o_rw, sfin = _rwkv_mixer(proj, b, n, P['rwkv_conv'][l], P['rwkv'][l], s0, tm)
    x = _merge(o_s5, o_rw, o_att, proj, x, P['merge'][l], mod_l, row_of_tile,
               row(P['norm_post_mix'][l]), tm)
    i = l // 2
    if l % 2 == 0:
        x = _ffn_dense(x, mod_l, row_of_tile, row(P['norm_pre_ffn'][l]), row(P['norm_post_ffn'][l]),
                       P['dense_w1_b'][i], P['dense_w3_b'][i], P['dense_w2_b'][i], tm, FF_DENSE // 2)
    else:
        x = _ffn_moe(x, mod_l, row_of_tile, row(P['norm_pre_ffn'][l]), row(P['norm_post_ffn'][l]),
                     P['moe_rw'][i], P['moe_rb'][i], P['moe_w1_b'][i], P['moe_w3_b'][i],
                     P['moe_w2_b'][i], tm)
    if cache is None:
        k_new = proj[:, C_QKV + DB:C_QKV + 2 * DB]
        v_new = proj[:, C_QKV + 2 * DB:C_QKV + 3 * DB]
        return x, (k_new, v_new, fre, fim, sfin)
    return x, None


def kernel(x_prompt, x_sample, cache_k, cache_v, state_s5_re, state_s5_im, state_rwkv, c, c_ctx,
           w_ada, b_ada, norm_pre_mix, norm_post_mix, norm_pre_ffn, norm_post_ffn, w_in,
           s5_lam_re, s5_lam_im, s5_log_step, s5_b_re, s5_b_im, s5_c_re, s5_c_im, s5_d, s5_w_glu,
           rwkv_conv, rwkv_w0, rwkv_w_up, rwkv_a0, rwkv_a_up, rwkv_g_up, rwkv_k_k, rwkv_k_a,
           rwkv_u, rwkv_ln_w, rwkv_ln_b, att_rpb, w_br_s5, w_br_rwkv, w_br_att, w_mix_out,
           dense_w1, dense_w3, dense_w2, moe_router_w, moe_router_b, moe_w1, moe_w3, moe_w2):
    bc, nc_, _ = x_prompt.shape
    bl, nl, _ = x_sample.shape
    tm_c = min(512, bc * nc_)
    tm_l = min(512, nl)

    P = dict(norm_pre_mix=norm_pre_mix, norm_post_mix=norm_post_mix, norm_pre_ffn=norm_pre_ffn,
             norm_post_ffn=norm_post_ffn, s5_d=s5_d, rwkv_conv=rwkv_conv, att_rpb=att_rpb)
    P['w_in_b'] = [_reorder_w_in(w_in[l]).astype(BF16) for l in range(DEPTH)]
    P['s5_w_glu_b'] = s5_w_glu.astype(BF16)
    P['dense_w1_b'], P['dense_w3_b'], P['dense_w2_b'] = (dense_w1.astype(BF16), dense_w3.astype(BF16),
                                                         dense_w2.astype(BF16))
    P['moe_w1_b'], P['moe_w3_b'], P['moe_w2_b'] = (moe_w1.astype(BF16), moe_w3.astype(BF16),
                                                   moe_w2.astype(BF16))
    n_moe = moe_router_w.shape[0]
    P['moe_rw'] = jnp.pad(moe_router_w, ((0, 0), (0, 0), (0, LANES - N_EXP)))
    P['moe_rb'] = jnp.pad(moe_router_b, ((0, 0), (0, LANES - N_EXP)),
                          constant_values=NEG).reshape(n_moe, 1, LANES)
    P['merge'] = [dict(br_s5=w_br_s5[l].astype(BF16), br_rw=w_br_rwkv[l].astype(BF16),
                       br_att=w_br_att[l].astype(BF16), mix=w_mix_out[l].astype(BF16))
                  for l in range(DEPTH)]
    P['s5'] = []
    P['rwkv'] = []
    for l in range(DEPTH):
        per_dir = [_s5_params(s5_lam_re[l, dd], s5_lam_im[l, dd], s5_log_step[l, dd], s5_b_re[l, dd],
                              s5_b_im[l, dd], s5_c_re[l, dd], s5_c_im[l, dd]) for dd in range(2)]
        P['s5'].append({k: jnp.stack([per_dir[0][k], per_dir[1][k]], axis=0) for k in per_dir[0]})
        P['rwkv'].append(dict(
            w0=rwkv_w0[l].reshape(2, 1, DB), w_up=rwkv_w_up[l], a0=rwkv_a0[l].reshape(2, 1, DB),
            a_up=rwkv_a_up[l], k_k=rwkv_k_k[l].reshape(1, DB), k_a=rwkv_k_a[l].reshape(1, DB),
            g_up=rwkv_g_up[l].astype(BF16), u=rwkv_u[l].reshape(1, DB),
            ln_w=rwkv_ln_w[l].reshape(1, DB), ln_b=rwkv_ln_b[l].reshape(1, DB)))

    cvec = jnp.zeros((8, D), F32).at[0].set(c_ctx).at[1:1 + bl].set(c)
    mod = _modulation(cvec, w_ada, b_ada)

    xp = x_prompt.reshape(bc * nc_, D)
    ks, vs, s5re, s5im, rws = [], [], [], [], []
    for l in range(DEPTH):
        xp, (k_n, v_n, fre, fim, sfin) = _layer(xp, bc, nc_, l, mod[l], lambda i: 0, P, None, tm_c)
        ks.append(k_n.reshape(bc, nc_, HEADS, DH))
        vs.append(v_n.reshape(bc, nc_, HEADS, DH))
        s5re.append(jnp.transpose(fre, (2, 0, 1, 3)))
        s5im.append(jnp.transpose(fim, (2, 0, 1, 3)))
        rws.append(_pairs_to_state(sfin))
    new_k = jnp.stack(ks, axis=1)
    new_v = jnp.stack(vs, axis=1)
    new_s5_re = jnp.stack(s5re, axis=1)
    new_s5_im = jnp.stack(s5im, axis=1)
    new_rwkv = jnp.stack(rws, axis=1)

    xs = x_sample.reshape(bl * nl, D)
    tiles_per_seq = nl // tm_l
    for l in range(DEPTH):
        cache = (cache_k[:, l], cache_v[:, l], state_s5_re[:, l], state_s5_im[:, l], state_rwkv[:, l])
        xs, _ = _layer(xs, bl, nl, l, mod[l], lambda i: 1 + i // tiles_per_seq, P, cache, tm_l)

    return (xp.reshape(bc, nc_, D), xs.reshape(bl, nl, D), new_k, new_v, new_s5_re, new_s5_im,
            new_rwkv)
```

```python
import functools
import math

import numpy as np
import jax
import jax.numpy as jnp
from jax import lax
from jax.experimental import pallas as pl
from jax.experimental.pallas import tpu as pltpu

F32 = jnp.float32
BF16 = jnp.bfloat16
HIGHEST = lax.Precision.HIGHEST

D = 1024
DEPTH = 2
GRID_W = 64
DH = 64
HEADS = 8
DB = 512
WIN_ROWS = 8
WIN_COLS = 16
S5_CH = 16
S5_G = 32
S5_P = 64
S5_J = 8
S5_UG = 8
FF_DENSE = 2816
N_EXP = 8
FF_EXP = 1024
EPS = 1e-6
GN_EPS = 64e-5
NEG = -1e30

C_GATE = 0
C_RKV = 3072
C_QKV = 4608
C_U = 6144
C_LORA = 6656
D_IN = 6912
TN_PROJ = 768

LANES = 128
RW_L = 64
NA_RB = 4
NA_KR = 12
VMEM_LIMIT = 56 * 1024 * 1024


def _cparams(n_axes, vmem=VMEM_LIMIT):
    return pltpu.CompilerParams(dimension_semantics=("arbitrary",) * n_axes,
                                vmem_limit_bytes=vmem)


def _dot(a, b):
    return jnp.dot(a, b, preferred_element_type=F32)


def _dot_nt(a, b):
    return lax.dot_general(a, b, (((1,), (1,)), ((), ())), preferred_element_type=F32)


def _dot_tn(a, b):
    return lax.dot_general(a, b, (((0,), (0,)), ((), ())), preferred_element_type=F32)


def _split2(x):
    hi = x.astype(BF16)
    lo = (x - hi.astype(F32)).astype(BF16)
    return hi, lo


def _split3(x):
    x1 = x.astype(BF16)
    r1 = x - x1.astype(F32)
    x2 = r1.astype(BF16)
    x3 = (r1 - x2.astype(F32)).astype(BF16)
    return x1, x2, x3


def _dot_rhs_exact(x, m):
    x1, x2, x3 = _split3(x)
    return _dot(x1, m) + _dot(x2, m) + _dot(x3, m)


def _dot_lhs_exact(m, x):
    x1, x2, x3 = _split3(x)
    return _dot(m, x1) + _dot(m, x2) + _dot(m, x3)


def _mm3(a, b):
    a1, a2 = _split2(a)
    b1, b2 = _split2(b)
    return _dot(a1, b1) + _dot(a1, b2) + _dot(a2, b1)


def _mm(a, b):
    return _dot(a.astype(BF16), b.astype(BF16))


_mm_inv = _mm
_mm_state = _mm3


def _sigmoid(x):
    return 1.0 / (1.0 + jnp.exp(-x))


def _silu(x):
    return x * _sigmoid(x)


def _softplus(x):
    return jnp.maximum(x, 0.0) + jnp.log(1.0 + jnp.exp(-jnp.abs(x)))


def _rms(x, g):
    return x * lax.rsqrt(jnp.mean(x * x, axis=-1, keepdims=True) + EPS) * g


def _head_block_matrix(width, value):
    r = lax.broadcasted_iota(jnp.int32, (width, width), 0) // DH
    c = lax.broadcasted_iota(jnp.int32, (width, width), 1) // DH
    return jnp.where(r == c, value, 0.0).astype(BF16)


def _mod_kernel(c_ref, w_ref, b_ref, o_ref):
    s = _silu(c_ref[...]).astype(BF16)
    o_ref[...] = _dot(s, w_ref[...].astype(BF16)) + b_ref[...]


def _modulation(cvec, w_ada, b_ada):
    tn = 1536
    out = pl.pallas_call(
        _mod_kernel,
        grid=(DEPTH, 6 * D // tn),
        in_specs=[pl.BlockSpec((8, D), lambda l, j: (0, 0)),
                  pl.BlockSpec((None, D, tn), lambda l, j: (l, 0, j)),
                  pl.BlockSpec((None, 1, tn), lambda l, j: (l, 0, j))],
        out_specs=pl.BlockSpec((None, 8, tn), lambda l, j: (l, 0, j)),
        out_shape=jax.ShapeDtypeStruct((DEPTH, 8, 6 * D), F32),
        compiler_params=_cparams(2),
    )(cvec, w_ada, b_ada.reshape(DEPTH, 1, 6 * D))
    return out.reshape(DEPTH, 8, 1, 6 * D)


def _mod_spec(part, row_of_tile, n_grid):
    if n_grid == 1:
        return pl.BlockSpec((None, 1, D), lambda i: (row_of_tile(i), 0, part))
    return pl.BlockSpec((None, 1, D), lambda i, j: (row_of_tile(i), 0, part))


def _proj_kernel(x_ref, g_ref, sc_ref, sh_ref, w_ref, o_ref, oq_ref, h_scr):
    j = pl.program_id(1)

    @pl.when(j == 0)
    def _():
        h = _rms(x_ref[...], g_ref[...]) * (1.0 + sc_ref[...]) + sh_ref[...]
        h_scr[...] = h.astype(BF16)

    acc = _dot(h_scr[...], w_ref[...])
    o_ref[...] = acc

    @pl.when((j >= C_QKV // TN_PROJ) & (j < C_U // TN_PROJ))
    def _():
        oq_ref[...] = acc.astype(BF16)


def _projection(x, mod_l, row_of_tile, g_pre, w_in_b, tm):
    t = x.shape[0]
    jq = C_QKV // TN_PROJ
    return pl.pallas_call(
        _proj_kernel,
        grid=(t // tm, D_IN // TN_PROJ),
        in_specs=[pl.BlockSpec((tm, D), lambda i, j: (i, 0)),
                  pl.BlockSpec((1, D), lambda i, j: (0, 0)),
                  _mod_spec(1, row_of_tile, 2),
                  _mod_spec(0, row_of_tile, 2),
                  pl.BlockSpec((D, TN_PROJ), lambda i, j: (0, j))],
        out_specs=[pl.BlockSpec((tm, TN_PROJ), lambda i, j: (i, j)),
                   pl.BlockSpec((tm, TN_PROJ), lambda i, j: (i, jnp.clip(j - jq, 0, 1)))],
        out_shape=[jax.ShapeDtypeStruct((t, D_IN), F32),
                   jax.ShapeDtypeStruct((t, 3 * DB), BF16)],
        scratch_shapes=[pltpu.VMEM((tm, D), BF16)],
        compiler_params=_cparams(2),
    )(x, g_pre, mod_l, mod_l, w_in_b)


def _pair_masks():
    lane = lax.broadcasted_iota(jnp.int32, (1, LANES), 1)
    return lane < DH


def _softmax_pv(parts, scale):
    m = None
    for s, _ in parts:
        mx = jnp.max(s, axis=-1, keepdims=True)
        m = mx if m is None else jnp.maximum(m, mx)
    es = [jnp.exp(s - m) for s, _ in parts]
    den = None
    for e in es:
        sm = jnp.sum(e, axis=-1, keepdims=True)
        den = sm if den is None else den + sm
    inv = 1.0 / den
    out = None
    for e, (_, v) in zip(es, parts):
        o = _dot((e * inv).astype(BF16), v)
        out = o if out is None else out + o
    return out


def _ctx_att_kernel(q_ref, k_ref, v_ref, o_ref):
    scale = DH ** -0.5
    m0 = _pair_masks()
    for p in range(HEADS // 2):
        sl = slice(p * LANES, (p + 1) * LANES)
        qp, kp, vp = q_ref[:, sl], k_ref[:, sl], v_ref[:, sl]
        o_pair = None
        for hh in range(2):
            msk = m0 if hh == 0 else jnp.logical_not(m0)
            qm = jnp.where(msk, qp, jnp.zeros_like(qp))
            s = _dot_nt(qm, kp) * scale
            o = _softmax_pv([(s, vp)], scale)
            o_pair = o if o_pair is None else jnp.where(m0, o_pair, o)
        o_ref[:, sl] = o_pair


def _ctx_attention(qkv, b, n):
    return pl.pallas_call(
        _ctx_att_kernel,
        grid=(b,),
        in_specs=[pl.BlockSpec((n, DB), lambda i: (i, 0)),
                  pl.BlockSpec((n, DB), lambda i: (i, 1)),
                  pl.BlockSpec((n, DB), lambda i: (i, 2))],
        out_specs=pl.BlockSpec((n, DB), lambda i: (i, 0)),
        out_shape=jax.ShapeDtypeStruct((b * n, DB), F32),
        compiler_params=_cparams(1),
    )(qkv, qkv, qkv)


def _na_key_start(rb, rows):
    return jnp.clip(rb * NA_RB - WIN_ROWS // 2, 0, rows - NA_KR)


def _na_kernel(q_ref, k_ref, v_ref, kc_ref, vc_ref, bias_ref, o_ref, *, rows):
    scale = DH ** -0.5
    m0 = _pair_masks()
    rb = pl.program_id(1)
    start = pl.multiple_of(_na_key_start(rb, rows) * GRID_W, GRID_W)
    nk = NA_KR * GRID_W
    for p in range(HEADS // 2):
        sl = slice(p * LANES, (p + 1) * LANES)
        qp = q_ref[:, sl]
        kp = k_ref[pl.ds(start, nk), sl]
        vp = v_ref[pl.ds(start, nk), sl]
        kcp, vcp = kc_ref[:, sl], vc_ref[:, sl]
        o_pair = None
        for hh in range(2):
            msk = m0 if hh == 0 else jnp.logical_not(m0)
            qm = jnp.where(msk, qp, jnp.zeros_like(qp))
            s_loc = _dot_nt(qm, kp) * scale + bias_ref[2 * p + hh]
            s_ctx = _dot_nt(qm, kcp) * scale
            o = _softmax_pv([(s_loc, vp), (s_ctx, vcp)], scale)
            o_pair = o if o_pair is None else jnp.where(m0, o_pair, o)
        o_ref[:, sl] = o_pair


def _na_bias_tables(rpb, rows):
    n_rb = rows // NA_RB
    wr = min(WIN_ROWS, rows)
    nrel_r, nrel_c = 2 * WIN_ROWS - 1, 2 * WIN_COLS - 1
    qc = np.arange(GRID_W)[:, None]
    kc = np.arange(GRID_W)[None, :]
    cs = np.clip(qc - WIN_COLS // 2, 0, GRID_W - WIN_COLS)
    col_ok = (kc >= cs) & (kc < cs + WIN_COLS)
    col_rel = np.clip(kc - qc + (WIN_COLS - 1), 0, nrel_c - 1)
    col_hot = (np.arange(nrel_c)[:, None, None] == col_rel[None]) & col_ok[None]
    col_hot = jnp.asarray(col_hot.reshape(nrel_c, GRID_W * GRID_W), F32)
    row_hot = np.zeros((3, NA_RB, NA_KR, nrel_r), np.float32)
    row_ok = np.zeros((3, NA_RB, NA_KR), bool)
    for vi, rb in enumerate((0, 1, n_rb - 1)):
        u0 = int(np.clip(rb * NA_RB - WIN_ROWS // 2, 0, rows - NA_KR))
        for qr in range(NA_RB):
            r = rb * NA_RB + qr
            rs = int(np.clip(r - wr // 2, 0, rows - wr))
            for kr in range(NA_KR):
                krow = u0 + kr
                if rs <= krow < rs + wr:
                    row_ok[vi, qr, kr] = True
                    row_hot[vi, qr, kr, krow - r + (WIN_ROWS - 1)] = 1.0
    t1 = jnp.einsum('vqkr,hrc->vhqkc', jnp.asarray(row_hot), rpb, precision=HIGHEST)
    t2 = jnp.dot(t1.reshape(-1, nrel_c), col_hot, precision=HIGHEST)
    t2 = t2.reshape(3, HEADS, NA_RB, NA_KR, GRID_W, GRID_W)
    ok = jnp.asarray(row_ok[:, None, :, :, None, None] & col_ok[None, None, None, None])
    t2 = jnp.where(ok, t2, NEG)
    return jnp.transpose(t2, (0, 1, 2, 4, 3, 5)).reshape(3, HEADS, NA_RB * GRID_W, NA_KR * GRID_W)


def _na_attention(qkv, k_ctx, v_ctx, bias, b, n):
    rows = n // GRID_W
    n_rb = rows // NA_RB
    tq = NA_RB * GRID_W
    lc = k_ctx.shape[0] // b

    def variant(j):
        return jnp.where(j == 0, 0, jnp.where(j == n_rb - 1, 2, 1))

    return pl.pallas_call(
        functools.partial(_na_kernel, rows=rows),
        grid=(b, n_rb),
        in_specs=[pl.BlockSpec((tq, DB), lambda i, j: (i * n_rb + j, 0)),
                  pl.BlockSpec((n, DB), lambda i, j: (i, 1)),
                  pl.BlockSpec((n, DB), lambda i, j: (i, 2)),
                  pl.BlockSpec((lc, DB), lambda i, j: (i, 0)),
                  pl.BlockSpec((lc, DB), lambda i, j: (i, 0)),
                  pl.BlockSpec((None, HEADS, tq, NA_KR * GRID_W), lambda i, j: (variant(j), 0, 0, 0))],
        out_specs=pl.BlockSpec((tq, DB), lambda i, j: (i * n_rb + j, 0)),
        out_shape=jax.ShapeDtypeStruct((b * n, DB), F32),
        compiler_params=_cparams(2),
    )(qkv, qkv, qkv, k_ctx, v_ctx, bias)


def _cmul(a, b):
    return a[0] * b[0] - a[1] * b[1], a[0] * b[1] + a[1] * b[0]


def _cexp(re, im):
    e = jnp.exp(re)
    return e * jnp.cos(im), e * jnp.sin(im)


def _s5_params(lam_re, lam_im, log_step, b_re, b_im, c_re, c_im):
    hp = dict(precision=HIGHEST)
    J, nq, ug = S5_J, S5_G // S5_UG, S5_UG
    eye = jnp.eye(ug, dtype=F32)
    step = jnp.exp(log_step)[..., None]
    lam_bar = _cexp(lam_re * step, lam_im * step)
    den = lam_re * lam_re + lam_im * lam_im
    num = (lam_bar[0] - 1.0, lam_bar[1])
    coef = ((num[0] * lam_re + num[1] * lam_im) / den, (num[1] * lam_re - num[0] * lam_im) / den)
    b_bar = _cmul((coef[0][..., None], coef[1][..., None]), (b_re, b_im))
    dd = jnp.arange(J + 1, dtype=F32)[:, None, None, None]
    lam_pow = _cexp(dd * (lam_re * step)[None], dd * (lam_im * step)[None])

    def units(x, g_axis):
        return x.reshape(x.shape[:g_axis] + (nq, ug) + x.shape[g_axis + 1:])

    sel = [jnp.stack([c[:J][::-1, 0], c[:J][:, 1]], axis=0) for c in lam_pow]
    wb = _cmul((sel[0][..., None], sel[1][..., None]), (b_bar[0][:, None], b_bar[1][:, None]))
    wb = units(jnp.stack(wb, axis=0), 3)
    bcat = jnp.einsum('ab,rdoqapc->qdoacrbp', eye, wb).reshape(nq, 2, J * LANES, 2 * ug * S5_P)
    lp = [jnp.transpose(c[:J], (1, 2, 0, 3))[:, :, :, None, :] for c in lam_pow]
    cl = _cmul((c_re[:, :, None], c_im[:, :, None]), lp)
    kd = (jnp.einsum('dgjcp,dgpk->dgjck', cl[0], b_bar[0], **hp)
          - jnp.einsum('dgjcp,dgpk->dgjck', cl[1], b_bar[1], **hp))
    oi = np.arange(J)[:, None]
    oo = np.arange(J)[None, :]
    lag_hot = np.stack([(oo - oi)[..., None] == np.arange(J), (oi - oo)[..., None] == np.arange(J)])
    t5 = jnp.einsum('dioj,dgjce->dgioce', jnp.asarray(lag_hot, F32), kd, **hp)
    tsum = jnp.einsum('ab,dqaioce->qiaeobc', eye, units(t5, 1)).reshape(nq, J * LANES, J * LANES)
    lq = [jnp.transpose(c[1:], (1, 2, 0, 3))[:, :, :, None, :] for c in lam_pow]
    cm = _cmul((c_re[:, :, None], c_im[:, :, None]), lq)
    cc = jnp.stack([cm[0], -cm[1]], axis=0)
    cc = jnp.stack([cc[:, 0], cc[:, 1, :, ::-1]], axis=1)
    ccat = jnp.einsum('ab,rdqatcp->qdraptbc', eye, units(cc, 2)).reshape(nq, 4 * ug * S5_P, J * LANES)
    b1 = jnp.einsum('ab,rdqapc->qdacrbp', eye, units(jnp.stack(b_bar, axis=0), 2))
    b1 = b1.reshape(nq, 2, LANES, 2 * ug * S5_P)

    def lanes(c):
        x = jnp.concatenate([c[0].reshape(2, nq, ug * S5_P), c[1].reshape(2, nq, ug * S5_P)], axis=-1)
        return jnp.transpose(x, (1, 0, 2))[:, :, None, :]

    return dict(bcat=bcat.astype(BF16), tsum=tsum.astype(BF16), ccat=ccat.astype(BF16), b1=b1,
                lj=lanes((lam_pow[0][J], lam_pow[1][J])), l1=lanes(lam_bar))


def _s5_kernel(u_ref, bcat_ref, tsum_ref, ccat_ref, lj_ref, l1_ref, b1_ref, x0_ref,
               y_ref, fin_ref, z_scr, *, nseq, n):
    J = S5_J
    nsub = n // J
    ns = nseq * nsub
    hw = S5_UG * S5_P
    if nseq == 1:
        ucat = jnp.concatenate([u_ref[pl.ds(o, ns, stride=J), :] for o in range(J)], axis=1)
    else:
        ucat = jnp.concatenate(
            [jnp.concatenate([u_ref[pl.ds(j * J + o, nseq, stride=n), :] for o in range(J)], axis=1)
             for j in range(nsub)], axis=0)
    ucat = ucat.astype(BF16)
    for d in range(2):
        z_scr[d] = _dot(ucat, bcat_ref[d])

    def cstep(d, x, z):
        lre, lim = lj_ref[d, :, 0:hw], lj_ref[d, :, hw:]
        return jnp.concatenate([lre * x[:, :hw] - lim * x[:, hw:] + z[:, :hw],
                                lre * x[:, hw:] + lim * x[:, :hw] + z[:, hw:]], axis=1)

    for d in range(2):
        tok = 0 if d == 0 else n - 1
        uf = u_ref[pl.ds(tok, nseq, stride=n), :] if nseq > 1 else u_ref[tok:tok + 1, :]
        bu = jnp.dot(uf, b1_ref[d], precision=HIGHEST, preferred_element_type=F32)
        x0 = x0_ref[d]
        lre, lim = l1_ref[d, :, 0:hw], l1_ref[d, :, hw:]
        fin_ref[d] = jnp.concatenate([lre * x0[:, :hw] - lim * x0[:, hw:] + bu[:, :hw],
                                      lre * x0[:, hw:] + lim * x0[:, :hw] + bu[:, hw:]], axis=1)

    if nseq == 1:
        def step(j, carry):
            out = []
            for d in range(2):
                row = pl.ds(j if d == 0 else nsub - 1 - j, 1)
                z = z_scr[d, row, :]
                z_scr[d, row, :] = carry[d]
                out.append(cstep(d, carry[d], z))
            return tuple(out)

        lax.fori_loop(0, nsub, step, (x0_ref[0], x0_ref[1]))
    else:
        xs = [x0_ref[0], x0_ref[1]]
        for j in range(nsub):
            for d in range(2):
                rows = pl.ds((j if d == 0 else nsub - 1 - j) * nseq, nseq)
                z = z_scr[d, rows, :]
                z_scr[d, rows, :] = xs[d]
                xs[d] = cstep(d, xs[d], z)

    xin = jnp.concatenate([z_scr[0], z_scr[1]], axis=1).astype(BF16)
    ycat = _dot(ucat, tsum_ref[...]) + _dot(xin, ccat_ref[...])
    for o in range(J):
        if nseq == 1:
            y_ref[pl.ds(o, ns, stride=J), :] = ycat[:, o * LANES:(o + 1) * LANES]
        else:
            for j in range(nsub):
                y_ref[pl.ds(j * J + o, nseq, stride=n), :] = ycat[j * nseq:(j + 1) * nseq,
                                                                  o * LANES:(o + 1) * LANES]


def _s5_scan(proj, b, n, prm, x0, nseq):
    nq = S5_G // S5_UG
    sw = 2 * S5_UG * S5_P
    jl = S5_J * LANES
    r = nseq * n
    ns = r // S5_J
    if nseq == 1:
        x0 = x0.reshape(nq, 2, b, 1, sw)
        st_spec = pl.BlockSpec((None, 2, None, 1, sw), lambda q, i: (q, 0, i, 0, 0))
    else:
        st_spec = pl.BlockSpec((None, 2, nseq, sw), lambda q, i: (q, 0, i, 0))
    wspec = lambda *shp: pl.BlockSpec((None,) + shp, lambda q, i: (q,) + (0,) * len(shp))
    y, fin = pl.pallas_call(
        functools.partial(_s5_kernel, nseq=nseq, n=n),
        grid=(nq, b // nseq),
        in_specs=[pl.BlockSpec((r, LANES), lambda q, i: (i, C_U // LANES + q)),
                  wspec(2, jl, sw), wspec(jl, jl), wspec(2 * sw, jl), wspec(2, 1, sw),
                  wspec(2, 1, sw), wspec(2, LANES, sw), st_spec],
        out_specs=[pl.BlockSpec((r, LANES), lambda q, i: (i, q)), st_spec],
        out_shape=[jax.ShapeDtypeStruct((b * n, DB), F32), jax.ShapeDtypeStruct(x0.shape, F32)],
        scratch_shapes=[pltpu.VMEM((2, ns, sw), F32)],
        compiler_params=_cparams(2),
    )(proj, prm['bcat'], prm['tsum'], prm['ccat'], prm['lj'], prm['l1'], prm['b1'], x0)
    return y, fin.reshape(nq, 2, b, sw)


def _s5_post_kernel(u_ref, y_ref, d_ref, w_ref, o_ref):
    y = jax.nn.gelu(d_ref[...] * u_ref[...] + y_ref[...])
    o_ref[...] = y * _sigmoid(_dot(y.astype(BF16), w_ref[...]))


def _s5_state_to_lanes(s_re, s_im):
    b = s_re.shape[0]
    nq = S5_G // S5_UG
    x = jnp.concatenate([s_re.reshape(b, 2, nq, S5_UG * S5_P), s_im.reshape(b, 2, nq, S5_UG * S5_P)],
                        axis=-1)
    return jnp.transpose(x, (2, 1, 0, 3))


def _s5_lanes_to_state(x):
    nq, _, b, _ = x.shape
    hw = S5_UG * S5_P
    x = jnp.transpose(x, (2, 1, 0, 3))
    return x[..., :hw].reshape(b, 2, S5_G, S5_P), x[..., hw:].reshape(b, 2, S5_G, S5_P)


def _s5_mixer(proj, b, n, sp, s5_d, w_glu_b, x0, tm):
    t = b * n
    nseq = 8 if (n // S5_J <= 64 and b % 8 == 0) else 1
    y, fin = _s5_scan(proj, b, n, sp, x0, nseq)
    o = pl.pallas_call(
        _s5_post_kernel,
        grid=(t // tm,),
        in_specs=[pl.BlockSpec((tm, DB), lambda i: (i, C_U // DB)),
                  pl.BlockSpec((tm, DB), lambda i: (i, 0)),
                  pl.BlockSpec((1, DB), lambda i: (0, 0)),
                  pl.BlockSpec((DB, DB), lambda i: (0, 0))],
        out_specs=pl.BlockSpec((tm, DB), lambda i: (i, 0)),
        out_shape=jax.ShapeDtypeStruct((t, DB), F32),
        compiler_params=_cparams(1),
    )(proj, y, s5_d, w_glu_b)
    return o, fin


def _conv_kernel(x_ref, prev_ref, next_ref, w_ref, o_ref, *, tiles_per_seq):
    i = pl.program_id(0)
    x = x_ref[...]
    tm = x.shape[0]
    row = lax.broadcasted_iota(jnp.int32, (tm, 1), 0)
    first = (i % tiles_per_seq) == 0
    last = (i % tiles_per_seq) == tiles_per_seq - 1
    prev_row = jnp.where(first, 0.0, prev_ref[7:8, :])
    next_row = jnp.where(last, 0.0, next_ref[0:1, :])
    x_dn = jnp.where(row == 0, prev_row, pltpu.roll(x, 1, axis=0))
    x_up = jnp.where(row == tm - 1, next_row, pltpu.roll(x, tm - 1, axis=0))
    o_ref[...] = x_dn * w_ref[0:1, :] + x * w_ref[1:2, :] + x_up * w_ref[2:3, :]


def _rwkv_conv(proj, conv_w, n, tm):
    t = proj.shape[0]
    w3 = 3 * DB
    cb = C_RKV // w3
    nb8 = t // 8
    return pl.pallas_call(
        functools.partial(_conv_kernel, tiles_per_seq=n // tm),
        grid=(t // tm,),
        in_specs=[pl.BlockSpec((tm, w3), lambda i: (i, cb)),
                  pl.BlockSpec((8, w3), lambda i: (jnp.maximum(i * (tm // 8) - 1, 0), cb)),
                  pl.BlockSpec((8, w3), lambda i: (jnp.minimum((i + 1) * (tm // 8), nb8 - 1), cb)),
                  pl.BlockSpec((3, w3), lambda i: (0, 0))],
        out_specs=pl.BlockSpec((tm, w3), lambda i: (i, 0)),
        out_shape=jax.ShapeDtypeStruct((t, w3), F32),
        compiler_params=_cparams(1),
    )(proj, proj, proj, conv_w)


def _rwkv_chunk_kernel(rc_ref, kc_ref, vc_ref, lora_ref, w0_ref, wup_ref, a0_ref, aup_ref,
                       kk_ref, ka_ref, s0_ref, y_ref, sfin_ref, z_scr, *, nc):
    d = pl.program_id(1)
    ci = pl.program_id(2)
    L = RW_L

    @pl.when(ci == 0)
    def _():
        z_scr[...] = s0_ref[...]

    rc, kc, vc = rc_ref[...], kc_ref[...], vc_ref[...]
    wd = lora_ref[:, 0:64]
    ad = lora_ref[:, 64:128]
    lora_w = jnp.dot(jnp.tanh(wd), wup_ref[...], precision=HIGHEST, preferred_element_type=F32)
    log_w = -_softplus(-(w0_ref[...] + lora_w)) - 0.5
    lw = -jnp.exp(log_w)
    a = _sigmoid(a0_ref[...] + jnp.dot(ad, aup_ref[...], precision=HIGHEST,
                                       preferred_element_type=F32))
    kd = kc * (1.0 + (a - 1.0) * ka_ref[...])
    kk = kc * kk_ref[...]
    kk = kk * lax.rsqrt(_dot_rhs_exact(kk * kk, _head_block_matrix(DB, 1.0)) + 1e-12)
    alpha = -kk
    beta = kk * a

    sgn = 1 - 2 * d
    tt = lax.broadcasted_iota(jnp.int32, (L, L), 0)
    ss = lax.broadcasted_iota(jnp.int32, (L, L), 1)
    tri = jnp.where((tt - ss) * sgn >= 0, 1.0, 0.0).astype(BF16)
    c = _dot_lhs_exact(tri, lw)
    c_ex = c - lw
    ctot = jnp.sum(lw, axis=0, keepdims=True)
    mid = 0.5 * ctot
    e_in = jnp.exp(c - mid)
    e_ex = jnp.exp(c_ex - mid)
    e_out = jnp.exp(mid - c)
    e_mid = jnp.exp(mid)
    al_t = alpha * e_ex
    r_t = rc * e_in
    be_t = beta * e_out
    k_t = kd * e_out
    a0s = al_t * e_mid
    r0s = r_t * e_mid
    bps = be_t * e_mid
    kps = k_t * e_mid
    p_l = e_mid * e_mid

    m0 = _pair_masks()
    t2 = lax.broadcasted_iota(jnp.int32, (2 * L, 2 * L), 0) % L
    s2 = lax.broadcasted_iota(jnp.int32, (2 * L, 2 * L), 1) % L
    strict = (t2 - s2) * sgn > 0
    incl = (t2 - s2) * sgn >= 0
    eye = (lax.broadcasted_iota(jnp.int32, (LANES, LANES), 0)
           == lax.broadcasted_iota(jnp.int32, (LANES, LANES), 1))
    eye_f = jnp.where(eye, 1.0, 0.0)
    zeros = jnp.zeros((LANES, LANES), F32)

    pairs = range(HEADS // 2)
    sls = [slice(p * LANES, (p + 1) * LANES) for p in pairs]

    def stack(x, p):
        xp = x[:, sls[p]]
        return jnp.concatenate([jnp.where(m0, xp, 0.0), jnp.where(m0, 0.0, xp)], axis=0)

    raws = [_dot_nt(jnp.concatenate([stack(al_t, p), stack(r_t, p)], axis=0).astype(BF16),
                    jnp.concatenate([stack(be_t, p), stack(k_t, p)], axis=0).astype(BF16))
            for p in pairs]
    amat = [jnp.where(strict, r[:LANES, :LANES], 0.0) for r in raws]
    bmat = [jnp.where(strict, r[:LANES, LANES:], 0.0) for r in raws]
    qbk = [jnp.concatenate([jnp.where(incl, r[LANES:, :LANES], 0.0),
                            jnp.where(incl, r[LANES:, LANES:], 0.0)], axis=1) for r in raws]

    smat = [eye_f + a for a in amat]
    pw = [_mm_inv(a, a) for a in amat]
    for _ in range(4):
        xs = [_mm_inv(pw[p], jnp.concatenate([pw[p], smat[p]], axis=1)) for p in pairs]
        smat = [smat[p] + xs[p][:, LANES:] for p in pairs]
        pw = [x[:, :LANES] for x in xs]
    tmat = [smat[p] + _mm_inv(pw[p], smat[p]) for p in pairs]

    vs = [stack(vc, p) for p in pairs]
    bv = [_mm(bmat[p], vs[p]) for p in pairs]
    wu = [_mm_inv(tmat[p], jnp.concatenate([stack(a0s, p), bv[p]], axis=1)) for p in pairs]
    rhs2 = [jnp.concatenate([wu[p], jnp.concatenate([zeros, vs[p]], axis=1)], axis=0).astype(BF16)
            for p in pairs]
    out_a = [_dot(qbk[p].astype(BF16), rhs2[p]) for p in pairs]
    out_b = [_dot_tn(jnp.concatenate([stack(bps, p), stack(kps, p)], axis=0).astype(BF16), rhs2[p])
             for p in pairs]
    hz = [_mm_state(jnp.concatenate(
        [stack(r0s, p) + out_a[p][:, :LANES],
         jnp.where(eye, p_l[:, sls[p]], 0.0) + out_b[p][:, :LANES]], axis=0), z_scr[p]) for p in pairs]
    for p in pairs:
        y = out_a[p][:, LANES:] + hz[p][:LANES]
        z_scr[p] = hz[p][LANES:] + out_b[p][:, LANES:]
        y_ref[:, sls[p]] = y[:L] + y[L:]

    @pl.when(ci == nc - 1)
    def _():
        sfin_ref[...] = z_scr[...]


def _rwkv_post_kernel(yf_ref, yb_ref, rkv_ref, gd_ref, gup_ref, u_ref, lnw_ref, lnb_ref, o_ref):
    y = yf_ref[...] + yb_ref[...]
    avg = _head_block_matrix(DB, 1.0 / DH)
    mu = _dot_rhs_exact(y, avg)
    yc = y - mu
    var = _dot_rhs_exact(yc * yc, avg)
    yn = yc * lax.rsqrt(var + GN_EPS) * lnw_ref[...] + lnb_ref[...]
    rc, kc, vc = rkv_ref[:, 0:DB], rkv_ref[:, DB:2 * DB], rkv_ref[:, 2 * DB:3 * DB]
    bonus = _dot_rhs_exact(rc * kc * u_ref[...], _head_block_matrix(DB, 1.0)) * vc
    g = _dot(_sigmoid(gd_ref[:, 128:256]).astype(BF16), gup_ref[...])
    o_ref[...] = (yn + bonus) * g


def _rwkv_mixer(proj, b, n, conv_w, prm, s0_pairs, tm):
    t = b * n
    nc = n // RW_L
    rkv = _rwkv_conv(proj, conv_w, n, min(tm, n))

    def chunk_row(i, dd, ci):
        return i * nc + jnp.where(dd == 0, ci, nc - 1 - ci)

    tok = lambda col: pl.BlockSpec((RW_L, DB), lambda i, dd, ci: (chunk_row(i, dd, ci), col))
    dirp = lambda r: pl.BlockSpec((None, r, DB), lambda i, dd, ci: (dd, 0, 0))
    shared = pl.BlockSpec((1, DB), lambda i, dd, ci: (0, 0))
    y2, sfin = pl.pallas_call(
        functools.partial(_rwkv_chunk_kernel, nc=nc),
        grid=(b, 2, nc),
        in_specs=[tok(0), tok(1), tok(2),
                  pl.BlockSpec((RW_L, 256), lambda i, dd, ci: (chunk_row(i, dd, ci), C_LORA // 256)),
                  dirp(1), dirp(64), dirp(1), dirp(64), shared, shared,
                  pl.BlockSpec((None, None, 4, LANES, LANES), lambda i, dd, ci: (i, dd, 0, 0, 0))],
        out_specs=[pl.BlockSpec((None, RW_L, DB), lambda i, dd, ci: (dd, chunk_row(i, dd, ci), 0)),
                   pl.BlockSpec((None, None, 4, LANES, LANES), lambda i, dd, ci: (i, dd, 0, 0, 0))],
        out_shape=[jax.ShapeDtypeStruct((2, t, DB), F32),
                   jax.ShapeDtypeStruct((b, 2, 4, LANES, LANES), F32)],
        scratch_shapes=[pltpu.VMEM((4, LANES, LANES), F32)],
        compiler_params=_cparams(3),
    )(rkv, rkv, rkv, proj, prm['w0'], prm['w_up'], prm['a0'], prm['a_up'], prm['k_k'], prm['k_a'],
      s0_pairs)
    o = pl.pallas_call(
        _rwkv_post_kernel,
        grid=(t // tm,),
        in_specs=[pl.BlockSpec((None, tm, DB), lambda i: (0, i, 0)),
                  pl.BlockSpec((None, tm, DB), lambda i: (1, i, 0)),
                  pl.BlockSpec((tm, 3 * DB), lambda i: (i, 0)),
                  pl.BlockSpec((tm, 256), lambda i: (i, C_LORA // 256)),
                  pl.BlockSpec((128, DB), lambda i: (0, 0)),
                  pl.BlockSpec((1, DB), lambda i: (0, 0)),
                  pl.BlockSpec((1, DB), lambda i: (0, 0)),
                  pl.BlockSpec((1, DB), lambda i: (0, 0))],
        out_specs=pl.BlockSpec((tm, DB), lambda i: (i, 0)),
        out_shape=jax.ShapeDtypeStruct((t, DB), F32),
        compiler_params=_cparams(1),
    )(y2, y2, rkv, proj, prm['g_up'], prm['u'], prm['ln_w'], prm['ln_b'])
    return o, sfin


def _state_to_pairs(s):
    b = s.shape[0]
    st = jnp.swapaxes(s, -1, -2).reshape(b, 2, HEADS // 2, 2, DH, DH)
    z = jnp.zeros((b, 2, HEADS // 2, 2, DH, 2, DH), F32)
    z = z.at[:, :, :, 0, :, 0, :].set(st[:, :, :, 0])
    z = z.at[:, :, :, 1, :, 1, :].set(st[:, :, :, 1])
    return z.reshape(b, 2, HEADS // 2, LANES, LANES)


def _pairs_to_state(z):
    b = z.shape[0]
    z = z.reshape(b, 2, HEADS // 2, 2, DH, 2, DH)
    st = jnp.stack([z[:, :, :, 0, :, 0, :], z[:, :, :, 1, :, 1, :]], axis=3)
    return jnp.swapaxes(st.reshape(b, 2, HEADS, DH, DH), -1, -2)


def _merge_kernel(os5_ref, orw_ref, oatt_ref, gs5_ref, grw_ref, gatt_ref, x_ref, ws5_ref, wrw_ref,
                  watt_ref, wmix_ref, gpost_ref, g1_ref, o_ref):
    merged = (_sigmoid(gs5_ref[...]) * _dot(os5_ref[...].astype(BF16), ws5_ref[...])
              + _sigmoid(grw_ref[...]) * _dot(orw_ref[...].astype(BF16), wrw_ref[...])
              + _sigmoid(gatt_ref[...]) * _dot(oatt_ref[...].astype(BF16), watt_ref[...]))
    m = _dot(merged.astype(BF16), wmix_ref[...])
    o_ref[...] = x_ref[...] + g1_ref[...] * _rms(m, gpost_ref[...])


def _merge(o_s5, o_rw, o_att, proj, x, wts, mod_l, row_of_tile, g_post, tm):
    t = x.shape[0]
    tok = lambda w, c: pl.BlockSpec((tm, w), lambda i: (i, c))
    full = lambda r, c: pl.BlockSpec((r, c), lambda i: (0, 0))
    return pl.pallas_call(
        _merge_kernel,
        grid=(t // tm,),
        in_specs=[tok(DB, 0), tok(DB, 0), tok(DB, 0), tok(D, 0), tok(D, 1), tok(D, 2), tok(D, 0),
                  full(DB, D), full(DB, D), full(DB, D), full(D, D), full(1, D),
                  _mod_spec(2, row_of_tile, 1)],
        out_specs=tok(D, 0),
        out_shape=jax.ShapeDtypeStruct((t, D), F32),
        compiler_params=_cparams(1),
    )(o_s5, o_rw, o_att, proj, proj, proj, x, wts['br_s5'], wts['br_rw'], wts['br_att'],
      wts['mix'], g_post, mod_l)


def _ffn_dense_kernel(x_ref, gpre_ref, sc_ref, sh_ref, w1_ref, w3_ref, w2_ref, gpost_ref, g2_ref,
                      o_ref, h_scr, acc_scr):
    f = pl.program_id(1)

    @pl.when(f == 0)
    def _():
        h = _rms(x_ref[...], gpre_ref[...]) * (1.0 + sc_ref[...]) + sh_ref[...]
        h_scr[...] = h.astype(BF16)
        acc_scr[...] = jnp.zeros_like(acc_scr)

    h = h_scr[...]
    hid = _silu(_dot(h, w1_ref[...])) * _dot(h, w3_ref[...])
    acc_scr[...] += _dot(hid.astype(BF16), w2_ref[...])

    @pl.when(f == pl.num_programs(1) - 1)
    def _():
        o_ref[...] = x_ref[...] + g2_ref[...] * _rms(acc_scr[...], gpost_ref[...])


def _ffn_dense(x, mod_l, row_of_tile, g_pre, g_post, w1, w3, w2, tm, tf):
    t = x.shape[0]
    return pl.pallas_call(
        _ffn_dense_kernel,
        grid=(t // tm, FF_DENSE // tf),
        in_specs=[pl.BlockSpec((tm, D), lambda i, f: (i, 0)),
                  pl.BlockSpec((1, D), lambda i, f: (0, 0)),
                  _mod_spec(4, row_of_tile, 2), _mod_spec(3, row_of_tile, 2),
                  pl.BlockSpec((D, tf), lambda i, f: (0, f)),
                  pl.BlockSpec((D, tf), lambda i, f: (0, f)),
                  pl.BlockSpec((tf, D), lambda i, f: (f, 0)),
                  pl.BlockSpec((1, D), lambda i, f: (0, 0)),
                  _mod_spec(5, row_of_tile, 2)],
        out_specs=pl.BlockSpec((tm, D), lambda i, f: (i, 0)),
        out_shape=jax.ShapeDtypeStruct((t, D), F32),
        scratch_shapes=[pltpu.VMEM((tm, D), BF16), pltpu.VMEM((tm, D), F32)],
        compiler_params=_cparams(2),
    )(x, g_pre, mod_l, mod_l, w1, w3, w2, g_post, mod_l)


def _ffn_moe_kernel(x_ref, gpre_ref, sc_ref, sh_ref, rw_ref, rb_ref, w1_ref, w3_ref, w2_ref,
                    gpost_ref, g2_ref, o_ref, h_scr, comb_scr, acc_scr):
    e = pl.program_id(1)
    lane = lax.broadcasted_iota(jnp.int32, (1, LANES), 1)

    @pl.when(e == 0)
    def _():
        h = _rms(x_ref[...], gpre_ref[...]) * (1.0 + sc_ref[...]) + sh_ref[...]
        h_scr[...] = h.astype(BF16)
        acc_scr[...] = jnp.zeros_like(acc_scr)
        logits = jnp.dot(h, rw_ref[...], precision=HIGHEST, preferred_element_type=F32) + rb_ref[...]
        ex = jnp.exp(logits - jnp.max(logits, axis=-1, keepdims=True))
        probs = ex / jnp.sum(ex, axis=-1, keepdims=True)
        p1 = jnp.max(probs, axis=-1, keepdims=True)
        i1 = jnp.min(jnp.where(probs == p1, lane, LANES), axis=-1, keepdims=True)
        rest = jnp.where(lane == i1, -1.0, probs)
        p2 = jnp.max(rest, axis=-1, keepdims=True)
        i2 = jnp.min(jnp.where(rest == p2, lane, LANES), axis=-1, keepdims=True)
        den = p1 + p2
        comb_scr[...] = jnp.where(lane == i1, p1 / den, 0.0) + jnp.where(lane == i2, p2 / den, 0.0)

    h = h_scr[...]
    cw = jnp.sum(jnp.where(lane == e, comb_scr[...], 0.0), axis=-1, keepdims=True)
    hid = _silu(_dot(h, w1_ref[...])) * _dot(h, w3_ref[...])
    acc_scr[...] += cw * _dot(hid.astype(BF16), w2_ref[...])

    @pl.when(e == N_EXP - 1)
    def _():
        o_ref[...] = x_ref[...] + g2_ref[...] * _rms(acc_scr[...], gpost_ref[...])


def _ffn_moe(x, mod_l, row_of_tile, g_pre, g_post, rw, rb, w1, w3, w2, tm):
    t = x.shape[0]
    return pl.pallas_call(
        _ffn_moe_kernel,
        grid=(t // tm, N_EXP),
        in_specs=[pl.BlockSpec((tm, D), lambda i, e: (i, 0)),
                  pl.BlockSpec((1, D), lambda i, e: (0, 0)),
                  _mod_spec(4, row_of_tile, 2), _mod_spec(3, row_of_tile, 2),
                  pl.BlockSpec((D, LANES), lambda i, e: (0, 0)),
                  pl.BlockSpec((1, LANES), lambda i, e: (0, 0)),
                  pl.BlockSpec((None, D, FF_EXP), lambda i, e: (e, 0, 0)),
                  pl.BlockSpec((None, D, FF_EXP), lambda i, e: (e, 0, 0)),
                  pl.BlockSpec((None, FF_EXP, D), lambda i, e: (e, 0, 0)),
                  pl.BlockSpec((1, D), lambda i, e: (0, 0)),
                  _mod_spec(5, row_of_tile, 2)],
        out_specs=pl.BlockSpec((tm, D), lambda i, e: (i, 0)),
        out_shape=jax.ShapeDtypeStruct((t, D), F32),
        scratch_shapes=[pltpu.VMEM((tm, D), BF16), pltpu.VMEM((tm, LANES), F32),
                        pltpu.VMEM((tm, D), F32)],
        compiler_params=_cparams(2),
    )(x, g_pre, mod_l, mod_l, rw, rb, w1, w3, w2, g_post, mod_l)


def _reorder_w_in(w):
    qkv, u, rkv, lora, gates = (w[:, 0:1536], w[:, 1536:2048], w[:, 2048:3584], w[:, 3584:3840],
                                w[:, 3840:6912])
    return jnp.concatenate([gates, rkv, qkv, u, lora], axis=1)


def _layer(x, b, n, l, mod_l, row_of_tile, P, cache, tm):
    row = lambda a: a.reshape(1, -1)
    proj, qkv = _projection(x, mod_l, row_of_tile, row(P['norm_pre_mix'][l]), P['w_in_b'][l], tm)
    if cache is None:
        o_att = _ctx_attention(qkv, b, n)
        x0 = jnp.zeros((S5_G // S5_UG, 2, b, 2 * S5_UG * S5_P), F32)
        s0 = jnp.zeros((b, 2, HEADS // 2, LANES, LANES), F32)
    else:
        k_c, v_c, s5re, s5im, rw0 = cache
        bias = _na_bias_tables(P['att_rpb'][l], n // GRID_W)
        o_att = _na_attention(qkv, k_c.reshape(-1, DB).astype(BF16), v_c.reshape(-1, DB).astype(BF16),
                              bias, b, n)
        x0 = _s5_state_to_lanes(s5re, s5im)
        s0 = _state_to_pairs(rw0)
    o_s5, s5fin = _s5_mixer(proj, b, n, P['s5'][l], row(P['s5_d'][l]), P['s5_w_glu_b'][l], x0, tm)
    o_rw, sfin = _rwkv_mixer(proj, b, n, P['rwkv_conv'][l], P['rwkv'][l], s0, tm)
    x = _merge(o_s5, o_rw, o_att, proj, x, P['merge'][l], mod_l, row_of_tile,
               row(P['norm_post_mix'][l]), tm)
    i = l // 2
    if l % 2 == 0:
        x = _ffn_dense(x, mod_l, row_of_tile, row(P['norm_pre_ffn'][l]), row(P['norm_post_ffn'][l]),
                       P['dense_w1_b'][i], P['dense_w3_b'][i], P['dense_w2_b'][i], tm, FF_DENSE // 2)
    else:
        x = _ffn_moe(x, mod_l, row_of_tile, row(P['norm_pre_ffn'][l]), row(P['norm_post_ffn'][l]),
                     P['moe_rw'][i], P['moe_rb'][i], P['moe_w1_b'][i], P['moe_w3_b'][i],
                     P['moe_w2_b'][i], tm)
    if cache is None:
        k_new = proj[:, C_QKV + DB:C_QKV + 2 * DB]
        v_new = proj[:, C_QKV + 2 * DB:C_QKV + 3 * DB]
        fre, fim = _s5_lanes_to_state(s5fin)
        return x, (k_new, v_new, fre, fim, sfin)
    return x, None


def kernel(x_prompt, x_sample, cache_k, cache_v, state_s5_re, state_s5_im, state_rwkv, c, c_ctx,
           w_ada, b_ada, norm_pre_mix, norm_post_mix, norm_pre_ffn, norm_post_ffn, w_in,
           s5_lam_re, s5_lam_im, s5_log_step, s5_b_re, s5_b_im, s5_c_re, s5_c_im, s5_d, s5_w_glu,
           rwkv_conv, rwkv_w0, rwkv_w_up, rwkv_a0, rwkv_a_up, rwkv_g_up, rwkv_k_k, rwkv_k_a,
           rwkv_u, rwkv_ln_w, rwkv_ln_b, att_rpb, w_br_s5, w_br_rwkv, w_br_att, w_mix_out,
           dense_w1, dense_w3, dense_w2, moe_router_w, moe_router_b, moe_w1, moe_w3, moe_w2):
    bc, nc_, _ = x_prompt.shape
    bl, nl, _ = x_sample.shape
    tm_c = min(512, bc * nc_)
    tm_l = min(512, nl)

    P = dict(norm_pre_mix=norm_pre_mix, norm_post_mix=norm_post_mix, norm_pre_ffn=norm_pre_ffn,
             norm_post_ffn=norm_post_ffn, s5_d=s5_d, rwkv_conv=rwkv_conv, att_rpb=att_rpb)
    P['w_in_b'] = [_reorder_w_in(w_in[l]).astype(BF16) for l in range(DEPTH)]
    P['s5_w_glu_b'] = s5_w_glu.astype(BF16)
    P['dense_w1_b'], P['dense_w3_b'], P['dense_w2_b'] = (dense_w1.astype(BF16), dense_w3.astype(BF16),
                                                         dense_w2.astype(BF16))
    P['moe_w1_b'], P['moe_w3_b'], P['moe_w2_b'] = (moe_w1.astype(BF16), moe_w3.astype(BF16),
                                                   moe_w2.astype(BF16))
    n_moe = moe_router_w.shape[0]
    P['moe_rw'] = jnp.pad(moe_router_w, ((0, 0), (0, 0), (0, LANES - N_EXP)))
    P['moe_rb'] = jnp.pad(moe_router_b, ((0, 0), (0, LANES - N_EXP)),
                          constant_values=NEG).reshape(n_moe, 1, LANES)
    P['merge'] = [dict(br_s5=w_br_s5[l].astype(BF16), br_rw=w_br_rwkv[l].astype(BF16),
                       br_att=w_br_att[l].astype(BF16), mix=w_mix_out[l].astype(BF16))
                  for l in range(DEPTH)]
    P['s5'] = []
    P['rwkv'] = []
    for l in range(DEPTH):
        P['s5'].append(_s5_params(s5_lam_re[l], s5_lam_im[l], s5_log_step[l], s5_b_re[l], s5_b_im[l],
                                  s5_c_re[l], s5_c_im[l]))
        P['rwkv'].append(dict(
            w0=rwkv_w0[l].reshape(2, 1, DB), w_up=rwkv_w_up[l], a0=rwkv_a0[l].reshape(2, 1, DB),
            a_up=rwkv_a_up[l], k_k=rwkv_k_k[l].reshape(1, DB), k_a=rwkv_k_a[l].reshape(1, DB),
            g_up=rwkv_g_up[l].astype(BF16), u=rwkv_u[l].reshape(1, DB),
            ln_w=rwkv_ln_w[l].reshape(1, DB), ln_b=rwkv_ln_b[l].reshape(1, DB)))

    cvec = jnp.zeros((8, D), F32).at[0].set(c_ctx).at[1:1 + bl].set(c)
    mod = _modulation(cvec, w_ada, b_ada)

    xp = x_prompt.reshape(bc * nc_, D)
    ks, vs, s5re, s5im, rws = [], [], [], [], []
    for l in range(DEPTH):
        xp, (k_n, v_n, fre, fim, sfin) = _layer(xp, bc, nc_, l, mod[l], lambda i: 0, P, None, tm_c)
        ks.append(k_n.reshape(bc, nc_, HEADS, DH))
        vs.append(v_n.reshape(bc, nc_, HEADS, DH))
        s5re.append(fre)
        s5im.append(fim)
        rws.append(_pairs_to_state(sfin))
    new_k = jnp.stack(ks, axis=1)
    new_v = jnp.stack(vs, axis=1)
    new_s5_re = jnp.stack(s5re, axis=1)
    new_s5_im = jnp.stack(s5im, axis=1)
    new_rwkv = jnp.stack(rws, axis=1)

    xs = x_sample.reshape(bl * nl, D)
    tiles_per_seq = nl // tm_l
    for l in range(DEPTH):
        cache = (cache_k[:, l], cache_v[:, l], state_s5_re[:, l], state_s5_im[:, l], state_rwkv[:, l])
        xs, _ = _layer(xs, bl, nl, l, mod[l], lambda i: 1 + i // tiles_per_seq, P, cache, tm_l)

    return (xp.reshape(bc, nc_, D), xs.reshape(bl, nl, D), new_k, new_v, new_s5_re, new_s5_im,
            new_rwkv)
```

```python
import functools
import math

import numpy as np
import jax
import jax.numpy as jnp
from jax import lax
from jax.experimental import pallas as pl
from jax.experimental.pallas import tpu as pltpu

F32 = jnp.float32
BF16 = jnp.bfloat16
HIGHEST = lax.Precision.HIGHEST

D = 1024
DEPTH = 2
GRID_W = 64
DH = 64
HEADS = 8
DB = 512
WIN_ROWS = 8
WIN_COLS = 16
S5_CH = 16
S5_G = 32
S5_P = 64
S5_J = 8
S5_UG = 8
FF_DENSE = 2816
N_EXP = 8
FF_EXP = 1024
EPS = 1e-6
GN_EPS = 64e-5
NEG = -1e30

D_IN = 6912
TN_PROJ = 768
J_MAIN = 4
J_QKV = 7
W_MAIN = (J_QKV - J_MAIN) * TN_PROJ
C_RKV = 0
C_U = 1536
C_LORA = 2048

LANES = 128
RW_L = 64
NA_RB = 4
NA_KR = 12
VMEM_LIMIT = 56 * 1024 * 1024


def _cparams(n_axes, vmem=VMEM_LIMIT):
    return pltpu.CompilerParams(dimension_semantics=("arbitrary",) * n_axes,
                                vmem_limit_bytes=vmem)


def _dot(a, b):
    return jnp.dot(a, b, preferred_element_type=F32)


def _dot_nt(a, b):
    return lax.dot_general(a, b, (((1,), (1,)), ((), ())), preferred_element_type=F32)


def _dot_tn(a, b):
    return lax.dot_general(a, b, (((0,), (0,)), ((), ())), preferred_element_type=F32)


def _split2(x):
    hi = x.astype(BF16)
    lo = (x - hi.astype(F32)).astype(BF16)
    return hi, lo


def _split3(x):
    x1 = x.astype(BF16)
    r1 = x - x1.astype(F32)
    x2 = r1.astype(BF16)
    x3 = (r1 - x2.astype(F32)).astype(BF16)
    return x1, x2, x3


def _dot_rhs_exact(x, m):
    x1, x2, x3 = _split3(x)
    return _dot(x1, m) + _dot(x2, m) + _dot(x3, m)


def _dot_lhs_exact(m, x):
    x1, x2, x3 = _split3(x)
    return _dot(m, x1) + _dot(m, x2) + _dot(m, x3)


def _mm3(a, b):
    a1, a2 = _split2(a)
    b1, b2 = _split2(b)
    return _dot(a1, b1) + _dot(a1, b2) + _dot(a2, b1)


def _mm(a, b):
    return _dot(a.astype(BF16), b.astype(BF16))


_mm_inv = _mm
_mm_state = _mm


def _sigmoid(x):
    return 1.0 / (1.0 + jnp.exp(-x))


def _silu(x):
    return x * _sigmoid(x)


def _softplus(x):
    return jnp.maximum(x, 0.0) + jnp.log(1.0 + jnp.exp(-jnp.abs(x)))


def _rms(x, g):
    return x * lax.rsqrt(jnp.mean(x * x, axis=-1, keepdims=True) + EPS) * g


def _head_block_matrix(width, value):
    r = lax.broadcasted_iota(jnp.int32, (width, width), 0) // DH
    c = lax.broadcasted_iota(jnp.int32, (width, width), 1) // DH
    return jnp.where(r == c, value, 0.0).astype(BF16)


def _mod_kernel(c_ref, w_ref, b_ref, o_ref):
    s = _silu(c_ref[...]).astype(BF16)
    o_ref[...] = _dot(s, w_ref[...].astype(BF16)) + b_ref[...]


def _modulation(cvec, w_ada, b_ada):
    tn = 1536
    out = pl.pallas_call(
        _mod_kernel,
        grid=(DEPTH, 6 * D // tn),
        in_specs=[pl.BlockSpec((8, D), lambda l, j: (0, 0)),
                  pl.BlockSpec((None, D, tn), lambda l, j: (l, 0, j)),
                  pl.BlockSpec((None, 1, tn), lambda l, j: (l, 0, j))],
        out_specs=pl.BlockSpec((None, 8, tn), lambda l, j: (l, 0, j)),
        out_shape=jax.ShapeDtypeStruct((DEPTH, 8, 6 * D), F32),
        compiler_params=_cparams(2),
    )(cvec, w_ada, b_ada.reshape(DEPTH, 1, 6 * D))
    return out.reshape(DEPTH, 8, 1, 6 * D)


def _mod_spec(part, row_of_tile, n_grid):
    if n_grid == 1:
        return pl.BlockSpec((None, 1, D), lambda i: (row_of_tile(i), 0, part))
    return pl.BlockSpec((None, 1, D), lambda i, j: (row_of_tile(i), 0, part))


def _proj_kernel(x_ref, g_ref, sc_ref, sh_ref, w_ref, og_ref, om_ref, oq_ref, *rest, f32_qkv):
    h_scr = rest[-1]
    j = pl.program_id(1)

    @pl.when(j == 0)
    def _():
        h = _rms(x_ref[...], g_ref[...]) * (1.0 + sc_ref[...]) + sh_ref[...]
        h_scr[...] = h.astype(BF16)

    acc = _dot(h_scr[...], w_ref[...])

    @pl.when(j < J_MAIN)
    def _():
        og_ref[...] = acc.astype(BF16)

    @pl.when((j >= J_MAIN) & (j < J_QKV))
    def _():
        om_ref[...] = acc

    @pl.when(j >= J_QKV)
    def _():
        oq_ref[...] = acc.astype(BF16)
        if f32_qkv:
            rest[0][...] = acc


def _projection(x, mod_l, row_of_tile, g_pre, w_in_b, tm, f32_qkv):
    t = x.shape[0]
    n_tiles = D_IN // TN_PROJ
    tile = lambda j0, j1: pl.BlockSpec((tm, TN_PROJ), lambda i, j: (i, jnp.clip(j - j0, 0, j1 - j0 - 1)))
    out_specs = [tile(0, J_MAIN), tile(J_MAIN, J_QKV), tile(J_QKV, n_tiles)]
    out_shape = [jax.ShapeDtypeStruct((t, 3 * D), BF16), jax.ShapeDtypeStruct((t, W_MAIN), F32),
                 jax.ShapeDtypeStruct((t, 3 * DB), BF16)]
    if f32_qkv:
        out_specs.append(tile(J_QKV, n_tiles))
        out_shape.append(jax.ShapeDtypeStruct((t, 3 * DB), F32))
    return pl.pallas_call(
        functools.partial(_proj_kernel, f32_qkv=f32_qkv),
        grid=(t // tm, n_tiles),
        in_specs=[pl.BlockSpec((tm, D), lambda i, j: (i, 0)),
                  pl.BlockSpec((1, D), lambda i, j: (0, 0)),
                  _mod_spec(1, row_of_tile, 2),
                  _mod_spec(0, row_of_tile, 2),
                  pl.BlockSpec((D, TN_PROJ), lambda i, j: (0, j))],
        out_specs=out_specs,
        out_shape=out_shape,
        scratch_shapes=[pltpu.VMEM((tm, D), BF16)],
        compiler_params=_cparams(2),
    )(x, g_pre, mod_l, mod_l, w_in_b)


def _pair_masks():
    lane = lax.broadcasted_iota(jnp.int32, (1, LANES), 1)
    return lane < DH


def _softmax_pv(parts, scale):
    m = None
    for s, _ in parts:
        mx = jnp.max(s, axis=-1, keepdims=True)
        m = mx if m is None else jnp.maximum(m, mx)
    es = [jnp.exp(s - m) for s, _ in parts]
    den = None
    for e in es:
        sm = jnp.sum(e, axis=-1, keepdims=True)
        den = sm if den is None else den + sm
    inv = 1.0 / den
    out = None
    for e, (_, v) in zip(es, parts):
        o = _dot((e * inv).astype(BF16), v)
        out = o if out is None else out + o
    return out


def _ctx_att_kernel(q_ref, k_ref, v_ref, o_ref):
    scale = DH ** -0.5
    m0 = _pair_masks()
    for p in range(HEADS // 2):
        sl = slice(p * LANES, (p + 1) * LANES)
        qp, kp, vp = q_ref[:, sl], k_ref[:, sl], v_ref[:, sl]
        o_pair = None
        for hh in range(2):
            msk = m0 if hh == 0 else jnp.logical_not(m0)
            qm = jnp.where(msk, qp, jnp.zeros_like(qp))
            s = _dot_nt(qm, kp) * scale
            o = _softmax_pv([(s, vp)], scale)
            o_pair = o if o_pair is None else jnp.where(m0, o_pair, o)
        o_ref[:, sl] = o_pair


def _ctx_attention(qkv, b, n):
    return pl.pallas_call(
        _ctx_att_kernel,
        grid=(b,),
        in_specs=[pl.BlockSpec((n, DB), lambda i: (i, 0)),
                  pl.BlockSpec((n, DB), lambda i: (i, 1)),
                  pl.BlockSpec((n, DB), lambda i: (i, 2))],
        out_specs=pl.BlockSpec((n, DB), lambda i: (i, 0)),
        out_shape=jax.ShapeDtypeStruct((b * n, DB), F32),
        compiler_params=_cparams(1),
    )(qkv, qkv, qkv)


def _na_key_start(rb, rows):
    return jnp.clip(rb * NA_RB - WIN_ROWS // 2, 0, rows - NA_KR)


def _na_kernel(q_ref, k_ref, v_ref, kc_ref, vc_ref, bias_ref, o_ref, *, rows):
    scale = DH ** -0.5
    m0 = _pair_masks()
    rb = pl.program_id(1)
    start = pl.multiple_of(_na_key_start(rb, rows) * GRID_W, GRID_W)
    nk = NA_KR * GRID_W
    for p in range(HEADS // 2):
        sl = slice(p * LANES, (p + 1) * LANES)
        qp = q_ref[:, sl]
        kp = k_ref[pl.ds(start, nk), sl]
        vp = v_ref[pl.ds(start, nk), sl]
        kcp, vcp = kc_ref[:, sl], vc_ref[:, sl]
        o_pair = None
        for hh in range(2):
            msk = m0 if hh == 0 else jnp.logical_not(m0)
            qm = jnp.where(msk, qp, jnp.zeros_like(qp))
            s_loc = _dot_nt(qm, kp) * scale + bias_ref[2 * p + hh]
            s_ctx = _dot_nt(qm, kcp) * scale
            o = _softmax_pv([(s_loc, vp), (s_ctx, vcp)], scale)
            o_pair = o if o_pair is None else jnp.where(m0, o_pair, o)
        o_ref[:, sl] = o_pair


def _na_bias_tables(rpb, rows):
    n_rb = rows // NA_RB
    wr = min(WIN_ROWS, rows)
    nrel_r, nrel_c = 2 * WIN_ROWS - 1, 2 * WIN_COLS - 1
    qc = np.arange(GRID_W)[:, None]
    kc = np.arange(GRID_W)[None, :]
    cs = np.clip(qc - WIN_COLS // 2, 0, GRID_W - WIN_COLS)
    col_ok = (kc >= cs) & (kc < cs + WIN_COLS)
    col_rel = np.clip(kc - qc + (WIN_COLS - 1), 0, nrel_c - 1)
    col_hot = (np.arange(nrel_c)[:, None, None] == col_rel[None]) & col_ok[None]
    col_hot = jnp.asarray(col_hot.reshape(nrel_c, GRID_W * GRID_W), F32)
    row_hot = np.zeros((3, NA_RB, NA_KR, nrel_r), np.float32)
    row_ok = np.zeros((3, NA_RB, NA_KR), bool)
    for vi, rb in enumerate((0, 1, n_rb - 1)):
        u0 = int(np.clip(rb * NA_RB - WIN_ROWS // 2, 0, rows - NA_KR))
        for qr in range(NA_RB):
            r = rb * NA_RB + qr
            rs = int(np.clip(r - wr // 2, 0, rows - wr))
            for kr in range(NA_KR):
                krow = u0 + kr
                if rs <= krow < rs + wr:
                    row_ok[vi, qr, kr] = True
                    row_hot[vi, qr, kr, krow - r + (WIN_ROWS - 1)] = 1.0
    t1 = jnp.einsum('vqkr,hrc->vhqkc', jnp.asarray(row_hot), rpb, precision=HIGHEST)
    t2 = jnp.dot(t1.reshape(-1, nrel_c), col_hot, precision=HIGHEST)
    t2 = t2.reshape(3, HEADS, NA_RB, NA_KR, GRID_W, GRID_W)
    ok = jnp.asarray(row_ok[:, None, :, :, None, None] & col_ok[None, None, None, None])
    t2 = jnp.where(ok, t2, NEG)
    return jnp.transpose(t2, (0, 1, 2, 4, 3, 5)).reshape(3, HEADS, NA_RB * GRID_W, NA_KR * GRID_W)


def _na_attention(qkv, k_ctx, v_ctx, bias, b, n):
    rows = n // GRID_W
    n_rb = rows // NA_RB
    tq = NA_RB * GRID_W
    lc = k_ctx.shape[0] // b

    def variant(j):
        return jnp.where(j == 0, 0, jnp.where(j == n_rb - 1, 2, 1))

    return pl.pallas_call(
        functools.partial(_na_kernel, rows=rows),
        grid=(b, n_rb),
        in_specs=[pl.BlockSpec((tq, DB), lambda i, j: (i * n_rb + j, 0)),
                  pl.BlockSpec((n, DB), lambda i, j: (i, 1)),
                  pl.BlockSpec((n, DB), lambda i, j: (i, 2)),
                  pl.BlockSpec((lc, DB), lambda i, j: (i, 0)),
                  pl.BlockSpec((lc, DB), lambda i, j: (i, 0)),
                  pl.BlockSpec((None, HEADS, tq, NA_KR * GRID_W), lambda i, j: (variant(j), 0, 0, 0))],
        out_specs=pl.BlockSpec((tq, DB), lambda i, j: (i * n_rb + j, 0)),
        out_shape=jax.ShapeDtypeStruct((b * n, DB), F32),
        compiler_params=_cparams(2),
    )(qkv, qkv, qkv, k_ctx, v_ctx, bias)


def _cmul(a, b):
    return a[0] * b[0] - a[1] * b[1], a[0] * b[1] + a[1] * b[0]


def _cexp(re, im):
    e = jnp.exp(re)
    return e * jnp.cos(im), e * jnp.sin(im)


def _s5_params(lam_re, lam_im, log_step, b_re, b_im, c_re, c_im):
    hp = dict(precision=HIGHEST)
    J, nq, ug = S5_J, S5_G // S5_UG, S5_UG
    eye = jnp.eye(ug, dtype=F32)
    step = jnp.exp(log_step)[..., None]
    lam_bar = _cexp(lam_re * step, lam_im * step)
    den = lam_re * lam_re + lam_im * lam_im
    num = (lam_bar[0] - 1.0, lam_bar[1])
    coef = ((num[0] * lam_re + num[1] * lam_im) / den, (num[1] * lam_re - num[0] * lam_im) / den)
    b_bar = _cmul((coef[0][..., None], coef[1][..., None]), (b_re, b_im))
    dd = jnp.arange(J + 1, dtype=F32)[:, None, None, None]
    lam_pow = _cexp(dd * (lam_re * step)[None], dd * (lam_im * step)[None])

    def units(x, g_axis):
        return x.reshape(x.shape[:g_axis] + (nq, ug) + x.shape[g_axis + 1:])

    sel = [jnp.stack([c[:J][::-1, 0], c[:J][:, 1]], axis=0) for c in lam_pow]
    wb = _cmul((sel[0][..., None], sel[1][..., None]), (b_bar[0][:, None], b_bar[1][:, None]))
    wb = units(jnp.stack(wb, axis=0), 3)
    bcat = jnp.einsum('ab,rdoqapc->qdoacrbp', eye, wb).reshape(nq, 2, J * LANES, 2 * ug * S5_P)
    lp = [jnp.transpose(c[:J], (1, 2, 0, 3))[:, :, :, None, :] for c in lam_pow]
    cl = _cmul((c_re[:, :, None], c_im[:, :, None]), lp)
    kd = (jnp.einsum('dgjcp,dgpk->dgjck', cl[0], b_bar[0], **hp)
          - jnp.einsum('dgjcp,dgpk->dgjck', cl[1], b_bar[1], **hp))
    oi = np.arange(J)[:, None]
    oo = np.arange(J)[None, :]
    lag_hot = np.stack([(oo - oi)[..., None] == np.arange(J), (oi - oo)[..., None] == np.arange(J)])
    t5 = jnp.einsum('dioj,dgjce->dgioce', jnp.asarray(lag_hot, F32), kd, **hp)
    tsum = jnp.einsum('ab,dqaioce->qiaeobc', eye, units(t5, 1)).reshape(nq, J * LANES, J * LANES)
    lq = [jnp.transpose(c[1:], (1, 2, 0, 3))[:, :, :, None, :] for c in lam_pow]
    cm = _cmul((c_re[:, :, None], c_im[:, :, None]), lq)
    cc = jnp.stack([cm[0], -cm[1]], axis=0)
    cc = jnp.stack([cc[:, 0], cc[:, 1, :, ::-1]], axis=1)
    ccat = jnp.einsum('ab,rdqatcp->qdraptbc', eye, units(cc, 2)).reshape(nq, 4 * ug * S5_P, J * LANES)
    b1 = jnp.einsum('ab,rdqapc->qdacrbp', eye, units(jnp.stack(b_bar, axis=0), 2))
    b1 = b1.reshape(nq, 2, LANES, 2 * ug * S5_P)

    def lanes(c):
        x = jnp.concatenate([c[0].reshape(2, nq, ug * S5_P), c[1].reshape(2, nq, ug * S5_P)], axis=-1)
        return jnp.transpose(x, (1, 0, 2))[:, :, None, :]

    return dict(bcat=bcat.astype(BF16), tsum=tsum.astype(BF16), ccat=ccat.astype(BF16), b1=b1,
                lj=lanes((lam_pow[0][J], lam_pow[1][J])), l1=lanes(lam_bar))


def _s5_kernel(u_ref, bcat_ref, tsum_ref, ccat_ref, lj_ref, l1_ref, b1_ref, x0_ref,
               y_ref, fin_ref, z_scr, *, nseq, n):
    J = S5_J
    nsub = n // J
    ns = nseq * nsub
    hw = S5_UG * S5_P
    if nseq == 1:
        ucat = jnp.concatenate([u_ref[pl.ds(o, ns, stride=J), :] for o in range(J)], axis=1)
    else:
        ucat = jnp.concatenate(
            [jnp.concatenate([u_ref[pl.ds(j * J + o, nseq, stride=n), :] for o in range(J)], axis=1)
             for j in range(nsub)], axis=0)
    ucat = ucat.astype(BF16)
    for d in range(2):
        z_scr[d] = _dot(ucat, bcat_ref[d])

    def cstep(d, x, z):
        lre, lim = lj_ref[d, :, 0:hw], lj_ref[d, :, hw:]
        return jnp.concatenate([lre * x[:, :hw] - lim * x[:, hw:] + z[:, :hw],
                                lre * x[:, hw:] + lim * x[:, :hw] + z[:, hw:]], axis=1)

    for d in range(2):
        tok = 0 if d == 0 else n - 1
        uf = u_ref[pl.ds(tok, nseq, stride=n), :] if nseq > 1 else u_ref[tok:tok + 1, :]
        bu = jnp.dot(uf, b1_ref[d], precision=HIGHEST, preferred_element_type=F32)
        x0 = x0_ref[d]
        lre, lim = l1_ref[d, :, 0:hw], l1_ref[d, :, hw:]
        fin_ref[d] = jnp.concatenate([lre * x0[:, :hw] - lim * x0[:, hw:] + bu[:, :hw],
                                      lre * x0[:, hw:] + lim * x0[:, :hw] + bu[:, hw:]], axis=1)

    if nseq == 1:
        def step(j, carry):
            out = []
            for d in range(2):
                row = pl.ds(j if d == 0 else nsub - 1 - j, 1)
                z = z_scr[d, row, :]
                z_scr[d, row, :] = carry[d]
                out.append(cstep(d, carry[d], z))
            return tuple(out)

        lax.fori_loop(0, nsub, step, (x0_ref[0], x0_ref[1]))
    else:
        xs = [x0_ref[0], x0_ref[1]]
        for j in range(nsub):
            for d in range(2):
                rows = pl.ds((j if d == 0 else nsub - 1 - j) * nseq, nseq)
                z = z_scr[d, rows, :]
                z_scr[d, rows, :] = xs[d]
                xs[d] = cstep(d, xs[d], z)

    xin = jnp.concatenate([z_scr[0], z_scr[1]], axis=1).astype(BF16)
    ycat = _dot(ucat, tsum_ref[...]) + _dot(xin, ccat_ref[...])
    for o in range(J):
        if nseq == 1:
            y_ref[pl.ds(o, ns, stride=J), :] = ycat[:, o * LANES:(o + 1) * LANES]
        else:
            for j in range(nsub):
                y_ref[pl.ds(j * J + o, nseq, stride=n), :] = ycat[j * nseq:(j + 1) * nseq,
                                                                  o * LANES:(o + 1) * LANES]


def _s5_scan(proj, b, n, prm, x0, nseq):
    nq = S5_G // S5_UG
    sw = 2 * S5_UG * S5_P
    jl = S5_J * LANES
    r = nseq * n
    ns = r // S5_J
    if nseq == 1:
        x0 = x0.reshape(nq, 2, b, 1, sw)
        st_spec = pl.BlockSpec((None, 2, None, 1, sw), lambda q, i: (q, 0, i, 0, 0))
    else:
        st_spec = pl.BlockSpec((None, 2, nseq, sw), lambda q, i: (q, 0, i, 0))
    wspec = lambda *shp: pl.BlockSpec((None,) + shp, lambda q, i: (q,) + (0,) * len(shp))
    y, fin = pl.pallas_call(
        functools.partial(_s5_kernel, nseq=nseq, n=n),
        grid=(nq, b // nseq),
        in_specs=[pl.BlockSpec((r, LANES), lambda q, i: (i, C_U // LANES + q)),
                  wspec(2, jl, sw), wspec(jl, jl), wspec(2 * sw, jl), wspec(2, 1, sw),
                  wspec(2, 1, sw), wspec(2, LANES, sw), st_spec],
        out_specs=[pl.BlockSpec((r, LANES), lambda q, i: (i, q)), st_spec],
        out_shape=[jax.ShapeDtypeStruct((b * n, DB), F32), jax.ShapeDtypeStruct(x0.shape, F32)],
        scratch_shapes=[pltpu.VMEM((2, ns, sw), F32)],
        compiler_params=_cparams(2),
    )(proj, prm['bcat'], prm['tsum'], prm['ccat'], prm['lj'], prm['l1'], prm['b1'], x0)
    return y, fin.reshape(nq, 2, b, sw)


def _s5_post_kernel(u_ref, y_ref, d_ref, w_ref, o_ref):
    y = jax.nn.gelu(d_ref[...] * u_ref[...] + y_ref[...])
    o_ref[...] = y * _sigmoid(_dot(y.astype(BF16), w_ref[...]))


def _s5_state_to_lanes(s_re, s_im):
    b = s_re.shape[0]
    nq = S5_G // S5_UG
    x = jnp.concatenate([s_re.reshape(b, 2, nq, S5_UG * S5_P), s_im.reshape(b, 2, nq, S5_UG * S5_P)],
                        axis=-1)
    return jnp.transpose(x, (2, 1, 0, 3))


def _s5_lanes_to_state(x):
    nq, _, b, _ = x.shape
    hw = S5_UG * S5_P
    x = jnp.transpose(x, (2, 1, 0, 3))
    return x[..., :hw].reshape(b, 2, S5_G, S5_P), x[..., hw:].reshape(b, 2, S5_G, S5_P)


def _s5_mixer(proj, b, n, sp, s5_d, w_glu_b, x0, tm):
    t = b * n
    nseq = 8 if (n // S5_J <= 64 and b % 8 == 0) else 1
    y, fin = _s5_scan(proj, b, n, sp, x0, nseq)
    o = pl.pallas_call(
        _s5_post_kernel,
        grid=(t // tm,),
        in_specs=[pl.BlockSpec((tm, DB), lambda i: (i, C_U // DB)),
                  pl.BlockSpec((tm, DB), lambda i: (i, 0)),
                  pl.BlockSpec((1, DB), lambda i: (0, 0)),
                  pl.BlockSpec((DB, DB), lambda i: (0, 0))],
        out_specs=pl.BlockSpec((tm, DB), lambda i: (i, 0)),
        out_shape=jax.ShapeDtypeStruct((t, DB), F32),
        compiler_params=_cparams(1),
    )(proj, y, s5_d, w_glu_b)
    return o, fin


def _conv_kernel(x_ref, prev_ref, next_ref, w_ref, o_ref, *, tiles_per_seq):
    i = pl.program_id(0)
    x = x_ref[...]
    tm = x.shape[0]
    row = lax.broadcasted_iota(jnp.int32, (tm, 1), 0)
    first = (i % tiles_per_seq) == 0
    last = (i % tiles_per_seq) == tiles_per_seq - 1
    prev_row = jnp.where(first, 0.0, prev_ref[7:8, :])
    next_row = jnp.where(last, 0.0, next_ref[0:1, :])
    x_dn = jnp.where(row == 0, prev_row, pltpu.roll(x, 1, axis=0))
    x_up = jnp.where(row == tm - 1, next_row, pltpu.roll(x, tm - 1, axis=0))
    o_ref[...] = x_dn * w_ref[0:1, :] + x * w_ref[1:2, :] + x_up * w_ref[2:3, :]


def _rwkv_conv(proj, conv_w, n, tm):
    t = proj.shape[0]
    w3 = 3 * DB
    cb = C_RKV // w3
    nb8 = t // 8
    return pl.pallas_call(
        functools.partial(_conv_kernel, tiles_per_seq=n // tm),
        grid=(t // tm,),
        in_specs=[pl.BlockSpec((tm, w3), lambda i: (i, cb)),
                  pl.BlockSpec((8, w3), lambda i: (jnp.maximum(i * (tm // 8) - 1, 0), cb)),
                  pl.BlockSpec((8, w3), lambda i: (jnp.minimum((i + 1) * (tm // 8), nb8 - 1), cb)),
                  pl.BlockSpec((3, w3), lambda i: (0, 0))],
        out_specs=pl.BlockSpec((tm, w3), lambda i: (i, 0)),
        out_shape=jax.ShapeDtypeStruct((t, w3), F32),
        compiler_params=_cparams(1),
    )(proj, proj, proj, conv_w)


def _rwkv_chunk_kernel(rc_ref, kc_ref, vc_ref, lora_ref, w0_ref, wup_ref, a0_ref, aup_ref,
                       kk_ref, ka_ref, s0_ref, y_ref, sfin_ref, z_scr, *, nc, nseq):
    d = pl.program_id(1)
    ci = pl.program_id(2)
    L = RW_L
    rows = nseq * L

    @pl.when(ci == 0)
    def _():
        z_scr[...] = s0_ref[...]

    flat = lambda ref: ref[...].reshape(rows, ref.shape[-1])
    rc, kc, vc, lora = flat(rc_ref), flat(kc_ref), flat(vc_ref), flat(lora_ref)
    wd = lora[:, 0:64]
    ad = lora[:, 64:128]
    lora_w = _mm3(jnp.tanh(wd), wup_ref[...])
    log_w = -_softplus(-(w0_ref[...] + lora_w)) - 0.5
    lw = -jnp.exp(log_w)
    a = _sigmoid(a0_ref[...] + _mm3(ad, aup_ref[...]))
    kd = kc * (1.0 + (a - 1.0) * ka_ref[...])
    kk = kc * kk_ref[...]
    kk = kk * lax.rsqrt(_dot_rhs_exact(kk * kk, _head_block_matrix(DB, 1.0)) + 1e-12)
    alpha = -kk
    beta = kk * a

    sgn = 1 - 2 * d
    tt = lax.broadcasted_iota(jnp.int32, (rows, rows), 0)
    ss = lax.broadcasted_iota(jnp.int32, (rows, rows), 1)
    tri = jnp.where(((tt % L - ss % L) * sgn >= 0) & (tt // L == ss // L), 1.0, 0.0).astype(BF16)
    c = _dot_lhs_exact(tri, lw)
    c_ex = c - lw
    ctot = jnp.concatenate(
        [jnp.broadcast_to(jnp.sum(lw[s * L:(s + 1) * L], axis=0, keepdims=True), (L, DB))
         for s in range(nseq)], axis=0)
    mid = 0.5 * ctot
    e_in = jnp.exp(c - mid)
    e_ex = jnp.exp(c_ex - mid)
    e_out = jnp.exp(mid - c)
    e_mid = jnp.exp(mid)
    al_t = alpha * e_ex
    r_t = rc * e_in
    be_t = beta * e_out
    k_t = kd * e_out
    a0s = al_t * e_mid
    r0s = r_t * e_mid
    bps = be_t * e_mid
    kps = k_t * e_mid
    p_l = e_mid * e_mid

    m0 = _pair_masks()
    t2 = lax.broadcasted_iota(jnp.int32, (2 * L, 2 * L), 0) % L
    s2 = lax.broadcasted_iota(jnp.int32, (2 * L, 2 * L), 1) % L
    strict = (t2 - s2) * sgn > 0
    incl = (t2 - s2) * sgn >= 0
    eye = (lax.broadcasted_iota(jnp.int32, (LANES, LANES), 0)
           == lax.broadcasted_iota(jnp.int32, (LANES, LANES), 1))
    eye_f = jnp.where(eye, 1.0, 0.0)
    zeros = jnp.zeros((LANES, LANES), F32)

    chains = [(s, p) for s in range(nseq) for p in range(HEADS // 2)]
    pairs = range(len(chains))
    rws = [slice(s * L, (s + 1) * L) for s, _ in chains]
    sls = [slice(p * LANES, (p + 1) * LANES) for _, p in chains]

    def stack(x, p):
        xp = x[rws[p], sls[p]]
        return jnp.concatenate([jnp.where(m0, xp, 0.0), jnp.where(m0, 0.0, xp)], axis=0)

    raws = [_dot_nt(jnp.concatenate([stack(al_t, p), stack(r_t, p)], axis=0).astype(BF16),
                    jnp.concatenate([stack(be_t, p), stack(k_t, p)], axis=0).astype(BF16))
            for p in pairs]
    amat = [jnp.where(strict, r[:LANES, :LANES], 0.0) for r in raws]
    bmat = [jnp.where(strict, r[:LANES, LANES:], 0.0) for r in raws]
    qbk = [jnp.concatenate([jnp.where(incl, r[LANES:, :LANES], 0.0),
                            jnp.where(incl, r[LANES:, LANES:], 0.0)], axis=1) for r in raws]

    smat = [eye_f + a for a in amat]
    pw = [_mm_inv(a, a) for a in amat]
    for _ in range(4):
        xs = [_mm_inv(pw[p], jnp.concatenate([pw[p], smat[p]], axis=1)) for p in pairs]
        smat = [smat[p] + xs[p][:, LANES:] for p in pairs]
        pw = [x[:, :LANES] for x in xs]
    tmat = [smat[p] + _mm_inv(pw[p], smat[p]) for p in pairs]

    vs = [stack(vc, p) for p in pairs]
    bv = [_mm(bmat[p], vs[p]) for p in pairs]
    wu = [_mm_inv(tmat[p], jnp.concatenate([stack(a0s, p), bv[p]], axis=1)) for p in pairs]
    rhs2 = [jnp.concatenate([wu[p], jnp.concatenate([zeros, vs[p]], axis=1)], axis=0).astype(BF16)
            for p in pairs]
    out_a = [_dot(qbk[p].astype(BF16), rhs2[p]) for p in pairs]
    out_b = [_dot_tn(jnp.concatenate([stack(bps, p), stack(kps, p)], axis=0).astype(BF16), rhs2[p])
             for p in pairs]
    hz = [_mm_state(jnp.concatenate(
        [stack(r0s, p) + out_a[p][:, :LANES],
         jnp.where(eye, p_l[rws[p], sls[p]][0:1], 0.0) + out_b[p][:, :LANES]], axis=0), z_scr[chains[p]])
          for p in pairs]
    for p in pairs:
        y = out_a[p][:, LANES:] + hz[p][:LANES]
        z_scr[chains[p]] = hz[p][LANES:] + out_b[p][:, LANES:]
        y_ref[chains[p][0], :, sls[p]] = y[:L] + y[L:]

    @pl.when(ci == nc - 1)
    def _():
        sfin_ref[...] = z_scr[...]


def _rwkv_post_kernel(yf_ref, yb_ref, rkv_ref, gd_ref, gup_ref, u_ref, lnw_ref, lnb_ref, o_ref):
    y = yf_ref[...] + yb_ref[...]
    avg = _head_block_matrix(DB, 1.0 / DH)
    mu = _dot_rhs_exact(y, avg)
    yc = y - mu
    var = _dot_rhs_exact(yc * yc, avg)
    yn = yc * lax.rsqrt(var + GN_EPS) * lnw_ref[...] + lnb_ref[...]
    rc, kc, vc = rkv_ref[:, 0:DB], rkv_ref[:, DB:2 * DB], rkv_ref[:, 2 * DB:3 * DB]
    bonus = _dot_rhs_exact(rc * kc * u_ref[...], _head_block_matrix(DB, 1.0)) * vc
    g = _dot(_sigmoid(gd_ref[:, 128:256]).astype(BF16), gup_ref[...])
    o_ref[...] = (yn + bonus) * g


def _rwkv_mixer(proj, b, n, conv_w, prm, s0_pairs, tm):
    t = b * n
    nc = n // RW_L
    rkv = _rwkv_conv(proj, conv_w, n, min(tm, n))

    nseq = 2 if b % 2 == 0 else 1
    hp = HEADS // 2

    def chunk(dd, ci):
        return jnp.where(dd == 0, ci, nc - 1 - ci)

    tok = lambda w, col: pl.BlockSpec((nseq, RW_L, w), lambda i, dd, ci: (i, chunk(dd, ci), col))
    dirp = lambda r: pl.BlockSpec((None, r, DB), lambda i, dd, ci: (dd, 0, 0))
    shared = pl.BlockSpec((1, DB), lambda i, dd, ci: (0, 0))
    state = pl.BlockSpec((nseq, None, hp, LANES, LANES), lambda i, dd, ci: (i, dd, 0, 0, 0))
    rkv3 = rkv.reshape(b, n, 3 * DB)
    y2, sfin = pl.pallas_call(
        functools.partial(_rwkv_chunk_kernel, nc=nc, nseq=nseq),
        grid=(b // nseq, 2, nc),
        in_specs=[tok(DB, 0), tok(DB, 1), tok(DB, 2), tok(256, C_LORA // 256),
                  dirp(1), dirp(64), dirp(1), dirp(64), shared, shared, state],
        out_specs=[pl.BlockSpec((None, nseq, RW_L, DB), lambda i, dd, ci: (dd, i, chunk(dd, ci), 0)),
                   state],
        out_shape=[jax.ShapeDtypeStruct((2, b, n, DB), F32),
                   jax.ShapeDtypeStruct((b, 2, hp, LANES, LANES), F32)],
        scratch_shapes=[pltpu.VMEM((nseq, hp, LANES, LANES), F32)],
        compiler_params=_cparams(3),
    )(rkv3, rkv3, rkv3, proj.reshape(b, n, W_MAIN), prm['w0'], prm['w_up'], prm['a0'], prm['a_up'],
      prm['k_k'], prm['k_a'], s0_pairs)
    y2 = y2.reshape(2, t, DB)
    o = pl.pallas_call(
        _rwkv_post_kernel,
        grid=(t // tm,),
        in_specs=[pl.BlockSpec((None, tm, DB), lambda i: (0, i, 0)),
                  pl.BlockSpec((None, tm, DB), lambda i: (1, i, 0)),
                  pl.BlockSpec((tm, 3 * DB), lambda i: (i, 0)),
                  pl.BlockSpec((tm, 256), lambda i: (i, C_LORA // 256)),
                  pl.BlockSpec((128, DB), lambda i: (0, 0)),
                  pl.BlockSpec((1, DB), lambda i: (0, 0)),
                  pl.BlockSpec((1, DB), lambda i: (0, 0)),
                  pl.BlockSpec((1, DB), lambda i: (0, 0))],
        out_specs=pl.BlockSpec((tm, DB), lambda i: (i, 0)),
        out_shape=jax.ShapeDtypeStruct((t, DB), F32),
        compiler_params=_cparams(1),
    )(y2, y2, rkv, proj, prm['g_up'], prm['u'], prm['ln_w'], prm['ln_b'])
    return o, sfin


def _state_to_pairs(s):
    b = s.shape[0]
    st = jnp.swapaxes(s, -1, -2).reshape(b, 2, HEADS // 2, 2, DH, DH)
    z = jnp.zeros((b, 2, HEADS // 2, 2, DH, 2, DH), F32)
    z = z.at[:, :, :, 0, :, 0, :].set(st[:, :, :, 0])
    z = z.at[:, :, :, 1, :, 1, :].set(st[:, :, :, 1])
    return z.reshape(b, 2, HEADS // 2, LANES, LANES)


def _pairs_to_state(z):
    b = z.shape[0]
    z = z.reshape(b, 2, HEADS // 2, 2, DH, 2, DH)
    st = jnp.stack([z[:, :, :, 0, :, 0, :], z[:, :, :, 1, :, 1, :]], axis=3)
    return jnp.swapaxes(st.reshape(b, 2, HEADS, DH, DH), -1, -2)


def _merge_kernel(os5_ref, orw_ref, oatt_ref, gs5_ref, grw_ref, gatt_ref, x_ref, ws5_ref, wrw_ref,
                  watt_ref, wmix_ref, gpost_ref, g1_ref, o_ref):
    gate = lambda ref: _sigmoid(ref[...].astype(F32))
    merged = (gate(gs5_ref) * _dot(os5_ref[...].astype(BF16), ws5_ref[...])
              + gate(grw_ref) * _dot(orw_ref[...].astype(BF16), wrw_ref[...])
              + gate(gatt_ref) * _dot(oatt_ref[...].astype(BF16), watt_ref[...]))
    m = _dot(merged.astype(BF16), wmix_ref[...])
    o_ref[...] = x_ref[...] + g1_ref[...] * _rms(m, gpost_ref[...])


def _merge(o_s5, o_rw, o_att, gates, x, wts, mod_l, row_of_tile, g_post, tm):
    t = x.shape[0]
    tok = lambda w, c: pl.BlockSpec((tm, w), lambda i: (i, c))
    full = lambda r, c: pl.BlockSpec((r, c), lambda i: (0, 0))
    return pl.pallas_call(
        _merge_kernel,
        grid=(t // tm,),
        in_specs=[tok(DB, 0), tok(DB, 0), tok(DB, 0), tok(D, 0), tok(D, 1), tok(D, 2), tok(D, 0),
                  full(DB, D), full(DB, D), full(DB, D), full(D, D), full(1, D),
                  _mod_spec(2, row_of_tile, 1)],
        out_specs=tok(D, 0),
        out_shape=jax.ShapeDtypeStruct((t, D), F32),
        compiler_params=_cparams(1),
    )(o_s5, o_rw, o_att, gates, gates, gates, x, wts['br_s5'], wts['br_rw'], wts['br_att'],
      wts['mix'], g_post, mod_l)


def _ffn_dense_kernel(x_ref, gpre_ref, sc_ref, sh_ref, w1_ref, w3_ref, w2_ref, gpost_ref, g2_ref,
                      o_ref, h_scr, acc_scr):
    f = pl.program_id(1)

    @pl.when(f == 0)
    def _():
        h = _rms(x_ref[...], gpre_ref[...]) * (1.0 + sc_ref[...]) + sh_ref[...]
        h_scr[...] = h.astype(BF16)
        acc_scr[...] = jnp.zeros_like(acc_scr)

    h = h_scr[...]
    hid = _silu(_dot(h, w1_ref[...])) * _dot(h, w3_ref[...])
    acc_scr[...] += _dot(hid.astype(BF16), w2_ref[...])

    @pl.when(f == pl.num_programs(1) - 1)
    def _():
        o_ref[...] = x_ref[...] + g2_ref[...] * _rms(acc_scr[...], gpost_ref[...])


def _ffn_dense(x, mod_l, row_of_tile, g_pre, g_post, w1, w3, w2, tm, tf):
    t = x.shape[0]
    return pl.pallas_call(
        _ffn_dense_kernel,
        grid=(t // tm, FF_DENSE // tf),
        in_specs=[pl.BlockSpec((tm, D), lambda i, f: (i, 0)),
                  pl.BlockSpec((1, D), lambda i, f: (0, 0)),
                  _mod_spec(4, row_of_tile, 2), _mod_spec(3, row_of_tile, 2),
                  pl.BlockSpec((D, tf), lambda i, f: (0, f)),
                  pl.BlockSpec((D, tf), lambda i, f: (0, f)),
                  pl.BlockSpec((tf, D), lambda i, f: (f, 0)),
                  pl.BlockSpec((1, D), lambda i, f: (0, 0)),
                  _mod_spec(5, row_of_tile, 2)],
        out_specs=pl.BlockSpec((tm, D), lambda i, f: (i, 0)),
        out_shape=jax.ShapeDtypeStruct((t, D), F32),
        scratch_shapes=[pltpu.VMEM((tm, D), BF16), pltpu.VMEM((tm, D), F32)],
        compiler_params=_cparams(2),
    )(x, g_pre, mod_l, mod_l, w1, w3, w2, g_post, mod_l)


def _ffn_moe_kernel(x_ref, gpre_ref, sc_ref, sh_ref, rw_ref, rb_ref, w1_ref, w3_ref, w2_ref,
                    gpost_ref, g2_ref, o_ref, h_scr, comb_scr, acc_scr):
    e = pl.program_id(1)
    lane = lax.broadcasted_iota(jnp.int32, (1, LANES), 1)

    @pl.when(e == 0)
    def _():
        h = _rms(x_ref[...], gpre_ref[...]) * (1.0 + sc_ref[...]) + sh_ref[...]
        h_scr[...] = h.astype(BF16)
        acc_scr[...] = jnp.zeros_like(acc_scr)
        logits = jnp.dot(h, rw_ref[...], precision=HIGHEST, preferred_element_type=F32) + rb_ref[...]
        ex = jnp.exp(logits - jnp.max(logits, axis=-1, keepdims=True))
        probs = ex / jnp.sum(ex, axis=-1, keepdims=True)
        p1 = jnp.max(probs, axis=-1, keepdims=True)
        i1 = jnp.min(jnp.where(probs == p1, lane, LANES), axis=-1, keepdims=True)
        rest = jnp.where(lane == i1, -1.0, probs)
        p2 = jnp.max(rest, axis=-1, keepdims=True)
        i2 = jnp.min(jnp.where(rest == p2, lane, LANES), axis=-1, keepdims=True)
        den = p1 + p2
        comb_scr[...] = jnp.where(lane == i1, p1 / den, 0.0) + jnp.where(lane == i2, p2 / den, 0.0)

    h = h_scr[...]
    cw = jnp.sum(jnp.where(lane == e, comb_scr[...], 0.0), axis=-1, keepdims=True)
    hid = _silu(_dot(h, w1_ref[...])) * _dot(h, w3_ref[...])
    acc_scr[...] += cw * _dot(hid.astype(BF16), w2_ref[...])

    @pl.when(e == N_EXP - 1)
    def _():
        o_ref[...] = x_ref[...] + g2_ref[...] * _rms(acc_scr[...], gpost_ref[...])


def _ffn_moe(x, mod_l, row_of_tile, g_pre, g_post, rw, rb, w1, w3, w2, tm):
    t = x.shape[0]
    return pl.pallas_call(
        _ffn_moe_kernel,
        grid=(t // tm, N_EXP),
        in_specs=[pl.BlockSpec((tm, D), lambda i, e: (i, 0)),
                  pl.BlockSpec((1, D), lambda i, e: (0, 0)),
                  _mod_spec(4, row_of_tile, 2), _mod_spec(3, row_of_tile, 2),
                  pl.BlockSpec((D, LANES), lambda i, e: (0, 0)),
                  pl.BlockSpec((1, LANES), lambda i, e: (0, 0)),
                  pl.BlockSpec((None, D, FF_EXP), lambda i, e: (e, 0, 0)),
                  pl.BlockSpec((None, D, FF_EXP), lambda i, e: (e, 0, 0)),
                  pl.BlockSpec((None, FF_EXP, D), lambda i, e: (e, 0, 0)),
                  pl.BlockSpec((1, D), lambda i, e: (0, 0)),
                  _mod_spec(5, row_of_tile, 2)],
        out_specs=pl.BlockSpec((tm, D), lambda i, e: (i, 0)),
        out_shape=jax.ShapeDtypeStruct((t, D), F32),
        scratch_shapes=[pltpu.VMEM((tm, D), BF16), pltpu.VMEM((tm, LANES), F32),
                        pltpu.VMEM((tm, D), F32)],
        compiler_params=_cparams(2),
    )(x, g_pre, mod_l, mod_l, rw, rb, w1, w3, w2, g_post, mod_l)


def _reorder_w_in(w):
    qkv, u, rkv, lora, gates = (w[:, 0:1536], w[:, 1536:2048], w[:, 2048:3584], w[:, 3584:3840],
                                w[:, 3840:6912])
    return jnp.concatenate([gates, rkv, u, lora, qkv], axis=1)


def _layer(x, b, n, l, mod_l, row_of_tile, P, cache, tm):
    row = lambda a: a.reshape(1, -1)
    tm_proj = min(2 * tm, x.shape[0], n) if cache is not None else min(2 * tm, x.shape[0])
    row_of_ptile = (lambda i: row_of_tile(i * (tm_proj // tm)))
    outs = _projection(x, mod_l, row_of_ptile, row(P['norm_pre_mix'][l]), P['w_in_b'][l], tm_proj,
                       cache is None)
    gates, proj, qkv = outs[:3]
    if cache is None:
        o_att = _ctx_attention(qkv, b, n)
        x0 = jnp.zeros((S5_G // S5_UG, 2, b, 2 * S5_UG * S5_P), F32)
        s0 = jnp.zeros((b, 2, HEADS // 2, LANES, LANES), F32)
    else:
        k_c, v_c, s5re, s5im, rw0 = cache
        bias = _na_bias_tables(P['att_rpb'][l], n // GRID_W)
        o_att = _na_attention(qkv, k_c.reshape(-1, DB).astype(BF16), v_c.reshape(-1, DB).astype(BF16),
                              bias, b, n)
        x0 = _s5_state_to_lanes(s5re, s5im)
        s0 = _state_to_pairs(rw0)
    o_s5, s5fin = _s5_mixer(proj, b, n, P['s5'][l], row(P['s5_d'][l]), P['s5_w_glu_b'][l], x0, tm)
    o_rw, sfin = _rwkv_mixer(proj, b, n, P['rwkv_conv'][l], P['rwkv'][l], s0, tm)
    x = _merge(o_s5, o_rw, o_att, gates, x, P['merge'][l], mod_l, row_of_tile,
               row(P['norm_post_mix'][l]), tm)
    i = l // 2
    if l % 2 == 0:
        x = _ffn_dense(x, mod_l, row_of_tile, row(P['norm_pre_ffn'][l]), row(P['norm_post_ffn'][l]),
                       P['dense_w1_b'][i], P['dense_w3_b'][i], P['dense_w2_b'][i], tm, FF_DENSE // 2)
    else:
        x = _ffn_moe(x, mod_l, row_of_tile, row(P['norm_pre_ffn'][l]), row(P['norm_post_ffn'][l]),
                     P['moe_rw'][i], P['moe_rb'][i], P['moe_w1_b'][i], P['moe_w3_b'][i],
                     P['moe_w2_b'][i], tm)
    if cache is None:
        k_new = outs[3][:, DB:2 * DB]
        v_new = outs[3][:, 2 * DB:3 * DB]
        fre, fim = _s5_lanes_to_state(s5fin)
        return x, (k_new, v_new, fre, fim, sfin)
    return x, None


def kernel(x_prompt, x_sample, cache_k, cache_v, state_s5_re, state_s5_im, state_rwkv, c, c_ctx,
           w_ada, b_ada, norm_pre_mix, norm_post_mix, norm_pre_ffn, norm_post_ffn, w_in,
           s5_lam_re, s5_lam_im, s5_log_step, s5_b_re, s5_b_im, s5_c_re, s5_c_im, s5_d, s5_w_glu,
           rwkv_conv, rwkv_w0, rwkv_w_up, rwkv_a0, rwkv_a_up, rwkv_g_up, rwkv_k_k, rwkv_k_a,
           rwkv_u, rwkv_ln_w, rwkv_ln_b, att_rpb, w_br_s5, w_br_rwkv, w_br_att, w_mix_out,
           dense_w1, dense_w3, dense_w2, moe_router_w, moe_router_b, moe_w1, moe_w3, moe_w2):
    bc, nc_, _ = x_prompt.shape
    bl, nl, _ = x_sample.shape
    tm_c = min(512, bc * nc_)
    tm_l = min(512, nl)

    P = dict(norm_pre_mix=norm_pre_mix, norm_post_mix=norm_post_mix, norm_pre_ffn=norm_pre_ffn,
             norm_post_ffn=norm_post_ffn, s5_d=s5_d, rwkv_conv=rwkv_conv, att_rpb=att_rpb)
    P['w_in_b'] = [_reorder_w_in(w_in[l]).astype(BF16) for l in range(DEPTH)]
    P['s5_w_glu_b'] = s5_w_glu.astype(BF16)
    P['dense_w1_b'], P['dense_w3_b'], P['dense_w2_b'] = (dense_w1.astype(BF16), dense_w3.astype(BF16),
                                                         dense_w2.astype(BF16))
    P['moe_w1_b'], P['moe_w3_b'], P['moe_w2_b'] = (moe_w1.astype(BF16), moe_w3.astype(BF16),
                                                   moe_w2.astype(BF16))
    n_moe = moe_router_w.shape[0]
    P['moe_rw'] = jnp.pad(moe_router_w, ((0, 0), (0, 0), (0, LANES - N_EXP)))
    P['moe_rb'] = jnp.pad(moe_router_b, ((0, 0), (0, LANES - N_EXP)),
                          constant_values=NEG).reshape(n_moe, 1, LANES)
    P['merge'] = [dict(br_s5=w_br_s5[l].astype(BF16), br_rw=w_br_rwkv[l].astype(BF16),
                       br_att=w_br_att[l].astype(BF16), mix=w_mix_out[l].astype(BF16))
                  for l in range(DEPTH)]
    P['s5'] = []
    P['rwkv'] = []
    for l in range(DEPTH):
        P['s5'].append(_s5_params(s5_lam_re[l], s5_lam_im[l], s5_log_step[l], s5_b_re[l], s5_b_im[l],
                                  s5_c_re[l], s5_c_im[l]))
        P['rwkv'].append(dict(
            w0=rwkv_w0[l].reshape(2, 1, DB), w_up=rwkv_w_up[l], a0=rwkv_a0[l].reshape(2, 1, DB),
            a_up=rwkv_a_up[l], k_k=rwkv_k_k[l].reshape(1, DB), k_a=rwkv_k_a[l].reshape(1, DB),
            g_up=rwkv_g_up[l].astype(BF16), u=rwkv_u[l].reshape(1, DB),
            ln_w=rwkv_ln_w[l].reshape(1, DB), ln_b=rwkv_ln_b[l].reshape(1, DB)))

    cvec = jnp.zeros((8, D), F32).at[0].set(c_ctx).at[1:1 + bl].set(c)
    mod = _modulation(cvec, w_ada, b_ada)

    xp = x_prompt.reshape(bc * nc_, D)
    ks, vs, s5re, s5im, rws = [], [], [], [], []
    for l in range(DEPTH):
        xp, (k_n, v_n, fre, fim, sfin) = _layer(xp, bc, nc_, l, mod[l], lambda i: 0, P, None, tm_c)
        ks.append(k_n.reshape(bc, nc_, HEADS, DH))
        vs.append(v_n.reshape(bc, nc_, HEADS, DH))
        s5re.append(fre)
        s5im.append(fim)
        rws.append(_pairs_to_state(sfin))
    new_k = jnp.stack(ks, axis=1)
    new_v = jnp.stack(vs, axis=1)
    new_s5_re = jnp.stack(s5re, axis=1)
    new_s5_im = jnp.stack(s5im, axis=1)
    new_rwkv = jnp.stack(rws, axis=1)

    xs = x_sample.reshape(bl * nl, D)
    tiles_per_seq = nl // tm_l
    for l in range(DEPTH):
        cache = (cache_k[:, l], cache_v[:, l], state_s5_re[:, l], state_s5_im[:, l], state_rwkv[:, l])
        xs, _ = _layer(xs, bl, nl, l, mod[l], lambda i: 1 + i // tiles_per_seq, P, cache, tm_l)

    return (xp.reshape(bc, nc_, D), xs.reshape(bl, nl, D), new_k, new_v, new_s5_re, new_s5_im,
            new_rwkv)
```

```python
import functools
import math

import numpy as np
import jax
import jax.numpy as jnp
from jax import lax
from jax.experimental import pallas as pl
from jax.experimental.pallas import tpu as pltpu

F32 = jnp.float32
BF16 = jnp.bfloat16
HIGHEST = lax.Precision.HIGHEST

D = 1024
DEPTH = 2
GRID_W = 64
DH = 64
HEADS = 8
DB = 512
WIN_ROWS = 8
WIN_COLS = 16
S5_CH = 16
S5_G = 32
S5_P = 64
S5_J = 8
S5_UG = 8
FF_DENSE = 2816
N_EXP = 8
FF_EXP = 1024
EPS = 1e-6
GN_EPS = 64e-5
NEG = -1e30

D_IN = 6912
TN_PROJ = 768
J_MAIN = 4
J_QKV = 7
W_MAIN = (J_QKV - J_MAIN) * TN_PROJ
C_RKV = 0
C_U = 1536
C_LORA = 2048

LANES = 128
RW_L = 64
NA_RB = 4
NA_KR = 12
VMEM_LIMIT = 56 * 1024 * 1024


def _cparams(n_axes, vmem=VMEM_LIMIT):
    return pltpu.CompilerParams(dimension_semantics=("arbitrary",) * n_axes,
                                vmem_limit_bytes=vmem)


def _dot(a, b):
    return jnp.dot(a, b, preferred_element_type=F32)


def _dot_nt(a, b):
    return lax.dot_general(a, b, (((1,), (1,)), ((), ())), preferred_element_type=F32)


def _dot_tn(a, b):
    return lax.dot_general(a, b, (((0,), (0,)), ((), ())), preferred_element_type=F32)


def _split2(x):
    hi = x.astype(BF16)
    lo = (x - hi.astype(F32)).astype(BF16)
    return hi, lo


def _split3(x):
    x1 = x.astype(BF16)
    r1 = x - x1.astype(F32)
    x2 = r1.astype(BF16)
    x3 = (r1 - x2.astype(F32)).astype(BF16)
    return x1, x2, x3


def _dot_rhs_exact(x, m):
    x1, x2, x3 = _split3(x)
    return _dot(x1, m) + _dot(x2, m) + _dot(x3, m)


def _dot_lhs_exact(m, x):
    x1, x2, x3 = _split3(x)
    return _dot(m, x1) + _dot(m, x2) + _dot(m, x3)


def _mm3(a, b):
    a1, a2 = _split2(a)
    b1, b2 = _split2(b)
    return _dot(a1, b1) + _dot(a1, b2) + _dot(a2, b1)


def _mm(a, b):
    return _dot(a.astype(BF16), b.astype(BF16))


_mm_inv = _mm
_mm_state = _mm


def _sigmoid(x):
    return 1.0 / (1.0 + jnp.exp(-x))


def _silu(x):
    return x * _sigmoid(x)


def _softplus(x):
    return jnp.maximum(x, 0.0) + jnp.log(1.0 + jnp.exp(-jnp.abs(x)))


def _rms(x, g):
    return x * lax.rsqrt(jnp.mean(x * x, axis=-1, keepdims=True) + EPS) * g


def _head_block_matrix(width, value):
    r = lax.broadcasted_iota(jnp.int32, (width, width), 0) // DH
    c = lax.broadcasted_iota(jnp.int32, (width, width), 1) // DH
    return jnp.where(r == c, value, 0.0).astype(BF16)


def _mod_kernel(c_ref, w_ref, b_ref, o_ref):
    s = _silu(c_ref[...]).astype(BF16)
    o_ref[...] = _dot(s, w_ref[...].astype(BF16)) + b_ref[...]


def _modulation(cvec, w_ada, b_ada):
    tn = 1536
    out = pl.pallas_call(
        _mod_kernel,
        grid=(DEPTH, 6 * D // tn),
        in_specs=[pl.BlockSpec((8, D), lambda l, j: (0, 0)),
                  pl.BlockSpec((None, D, tn), lambda l, j: (l, 0, j)),
                  pl.BlockSpec((None, 1, tn), lambda l, j: (l, 0, j))],
        out_specs=pl.BlockSpec((None, 8, tn), lambda l, j: (l, 0, j)),
        out_shape=jax.ShapeDtypeStruct((DEPTH, 8, 6 * D), F32),
        compiler_params=_cparams(2),
    )(cvec, w_ada, b_ada.reshape(DEPTH, 1, 6 * D))
    return out.reshape(DEPTH, 8, 1, 6 * D)


def _mod_spec(part, row_of_tile, n_grid):
    if n_grid == 1:
        return pl.BlockSpec((None, 1, D), lambda i: (row_of_tile(i), 0, part))
    return pl.BlockSpec((None, 1, D), lambda i, j: (row_of_tile(i), 0, part))


def _proj_kernel(x_ref, g_ref, sc_ref, sh_ref, w_ref, og_ref, om_ref, oq_ref, *rest, f32_qkv):
    h_scr = rest[-1]
    j = pl.program_id(1)

    @pl.when(j == 0)
    def _():
        h = _rms(x_ref[...], g_ref[...]) * (1.0 + sc_ref[...]) + sh_ref[...]
        h_scr[...] = h.astype(BF16)

    acc = _dot(h_scr[...], w_ref[...])

    @pl.when(j < J_MAIN)
    def _():
        og_ref[...] = acc.astype(BF16)

    @pl.when((j >= J_MAIN) & (j < J_QKV))
    def _():
        om_ref[...] = acc

    @pl.when(j >= J_QKV)
    def _():
        oq_ref[...] = acc.astype(BF16)
        if f32_qkv:
            rest[0][...] = acc


def _projection(x, mod_l, row_of_tile, g_pre, w_in_b, tm, f32_qkv):
    t = x.shape[0]
    n_tiles = D_IN // TN_PROJ
    tile = lambda j0, j1: pl.BlockSpec((tm, TN_PROJ), lambda i, j: (i, jnp.clip(j - j0, 0, j1 - j0 - 1)))
    out_specs = [tile(0, J_MAIN), tile(J_MAIN, J_QKV), tile(J_QKV, n_tiles)]
    out_shape = [jax.ShapeDtypeStruct((t, 3 * D), BF16), jax.ShapeDtypeStruct((t, W_MAIN), F32),
                 jax.ShapeDtypeStruct((t, 3 * DB), BF16)]
    if f32_qkv:
        out_specs.append(tile(J_QKV, n_tiles))
        out_shape.append(jax.ShapeDtypeStruct((t, 3 * DB), F32))
    return pl.pallas_call(
        functools.partial(_proj_kernel, f32_qkv=f32_qkv),
        grid=(t // tm, n_tiles),
        in_specs=[pl.BlockSpec((tm, D), lambda i, j: (i, 0)),
                  pl.BlockSpec((1, D), lambda i, j: (0, 0)),
                  _mod_spec(1, row_of_tile, 2),
                  _mod_spec(0, row_of_tile, 2),
                  pl.BlockSpec((D, TN_PROJ), lambda i, j: (0, j))],
        out_specs=out_specs,
        out_shape=out_shape,
        scratch_shapes=[pltpu.VMEM((tm, D), BF16)],
        compiler_params=_cparams(2),
    )(x, g_pre, mod_l, mod_l, w_in_b)


def _pair_masks():
    lane = lax.broadcasted_iota(jnp.int32, (1, LANES), 1)
    return lane < DH


def _softmax_pv(parts):
    m = None
    for s, _ in parts:
        mx = jnp.max(s, axis=-1, keepdims=True)
        m = mx if m is None else jnp.maximum(m, mx)
    es = [jnp.exp(s - m) for s, _ in parts]
    den = None
    for e in es:
        sm = jnp.sum(e, axis=-1, keepdims=True)
        den = sm if den is None else den + sm
    out = None
    for e, (_, v) in zip(es, parts):
        o = _dot(e.astype(BF16), v)
        out = o if out is None else out + o
    return out * (1.0 / den)


def _ctx_att_kernel(q_ref, k_ref, v_ref, o_ref):
    scale = DH ** -0.5
    m0 = _pair_masks()
    for p in range(HEADS // 2):
        sl = slice(p * LANES, (p + 1) * LANES)
        qp, kp, vp = q_ref[:, sl] * scale, k_ref[:, sl], v_ref[:, sl]
        o_pair = None
        for hh in range(2):
            msk = m0 if hh == 0 else jnp.logical_not(m0)
            qm = jnp.where(msk, qp, jnp.zeros_like(qp))
            o = _softmax_pv([(_dot_nt(qm, kp), vp)])
            o_pair = o if o_pair is None else jnp.where(m0, o_pair, o)
        o_ref[:, sl] = o_pair


def _ctx_attention(qkv, b, n):
    return pl.pallas_call(
        _ctx_att_kernel,
        grid=(b,),
        in_specs=[pl.BlockSpec((n, DB), lambda i: (i, 0)),
                  pl.BlockSpec((n, DB), lambda i: (i, 1)),
                  pl.BlockSpec((n, DB), lambda i: (i, 2))],
        out_specs=pl.BlockSpec((n, DB), lambda i: (i, 0)),
        out_shape=jax.ShapeDtypeStruct((b * n, DB), F32),
        compiler_params=_cparams(1),
    )(qkv, qkv, qkv)


def _na_key_start(rb, rows):
    return jnp.clip(rb * NA_RB - WIN_ROWS // 2, 0, rows - NA_KR)


def _na_kernel(q_ref, k_ref, v_ref, kc_ref, vc_ref, bias_ref, o_ref, *, rows):
    scale = DH ** -0.5
    m0 = _pair_masks()
    rb = pl.program_id(1)
    start = pl.multiple_of(_na_key_start(rb, rows) * GRID_W, GRID_W)
    nk = NA_KR * GRID_W
    for p in range(HEADS // 2):
        sl = slice(p * LANES, (p + 1) * LANES)
        qp = q_ref[:, sl] * scale
        kp = k_ref[pl.ds(start, nk), sl]
        vp = v_ref[pl.ds(start, nk), sl]
        kcp, vcp = kc_ref[:, sl], vc_ref[:, sl]
        o_pair = None
        for hh in range(2):
            msk = m0 if hh == 0 else jnp.logical_not(m0)
            qm = jnp.where(msk, qp, jnp.zeros_like(qp))
            s_loc = _dot_nt(qm, kp) + bias_ref[2 * p + hh]
            s_ctx = _dot_nt(qm, kcp)
            o = _softmax_pv([(s_loc, vp), (s_ctx, vcp)])
            o_pair = o if o_pair is None else jnp.where(m0, o_pair, o)
        o_ref[:, sl] = o_pair


def _na_bias_tables(rpb, rows):
    n_rb = rows // NA_RB
    wr = min(WIN_ROWS, rows)
    nrel_r, nrel_c = 2 * WIN_ROWS - 1, 2 * WIN_COLS - 1
    qc = np.arange(GRID_W)[:, None]
    kc = np.arange(GRID_W)[None, :]
    cs = np.clip(qc - WIN_COLS // 2, 0, GRID_W - WIN_COLS)
    col_ok = (kc >= cs) & (kc < cs + WIN_COLS)
    col_rel = np.clip(kc - qc + (WIN_COLS - 1), 0, nrel_c - 1)
    col_hot = (np.arange(nrel_c)[:, None, None] == col_rel[None]) & col_ok[None]
    col_hot = jnp.asarray(col_hot.reshape(nrel_c, GRID_W * GRID_W), F32)
    row_hot = np.zeros((3, NA_RB, NA_KR, nrel_r), np.float32)
    row_ok = np.zeros((3, NA_RB, NA_KR), bool)
    for vi, rb in enumerate((0, 1, n_rb - 1)):
        u0 = int(np.clip(rb * NA_RB - WIN_ROWS // 2, 0, rows - NA_KR))
        for qr in range(NA_RB):
            r = rb * NA_RB + qr
            rs = int(np.clip(r - wr // 2, 0, rows - wr))
            for kr in range(NA_KR):
                krow = u0 + kr
                if rs <= krow < rs + wr:
                    row_ok[vi, qr, kr] = True
                    row_hot[vi, qr, kr, krow - r + (WIN_ROWS - 1)] = 1.0
    t1 = jnp.einsum('vqkr,hrc->vhqkc', jnp.asarray(row_hot), rpb, precision=HIGHEST)
    t2 = jnp.dot(t1.reshape(-1, nrel_c), col_hot, precision=HIGHEST)
    t2 = t2.reshape(3, HEADS, NA_RB, NA_KR, GRID_W, GRID_W)
    ok = jnp.asarray(row_ok[:, None, :, :, None, None] & col_ok[None, None, None, None])
    t2 = jnp.where(ok, t2, NEG)
    return jnp.transpose(t2, (0, 1, 2, 4, 3, 5)).reshape(3, HEADS, NA_RB * GRID_W, NA_KR * GRID_W)


def _na_attention(qkv, k_ctx, v_ctx, bias, b, n):
    rows = n // GRID_W
    n_rb = rows // NA_RB
    tq = NA_RB * GRID_W
    lc = k_ctx.shape[0] // b

    def variant(j):
        return jnp.where(j == 0, 0, jnp.where(j == n_rb - 1, 2, 1))

    return pl.pallas_call(
        functools.partial(_na_kernel, rows=rows),
        grid=(b, n_rb),
        in_specs=[pl.BlockSpec((tq, DB), lambda i, j: (i * n_rb + j, 0)),
                  pl.BlockSpec((n, DB), lambda i, j: (i, 1)),
                  pl.BlockSpec((n, DB), lambda i, j: (i, 2)),
                  pl.BlockSpec((lc, DB), lambda i, j: (i, 0)),
                  pl.BlockSpec((lc, DB), lambda i, j: (i, 0)),
                  pl.BlockSpec((None, HEADS, tq, NA_KR * GRID_W), lambda i, j: (variant(j), 0, 0, 0))],
        out_specs=pl.BlockSpec((tq, DB), lambda i, j: (i * n_rb + j, 0)),
        out_shape=jax.ShapeDtypeStruct((b * n, DB), F32),
        compiler_params=_cparams(2),
    )(qkv, qkv, qkv, k_ctx, v_ctx, bias)


def _cmul(a, b):
    return a[0] * b[0] - a[1] * b[1], a[0] * b[1] + a[1] * b[0]


def _cexp(re, im):
    e = jnp.exp(re)
    return e * jnp.cos(im), e * jnp.sin(im)


def _s5_params(lam_re, lam_im, log_step, b_re, b_im, c_re, c_im):
    hp = dict(precision=HIGHEST)
    J, nq, ug = S5_J, S5_G // S5_UG, S5_UG
    sw = 2 * ug * S5_P

    def block_diag(x2d, row_group, col_inner):
        rows = x2d.shape[-2]
        ci = np.arange(sw)
        src = (ci // (ug * col_inner)) * col_inner + ci % col_inner
        expand = jnp.asarray(np.arange(LANES)[:, None] == src[None, :], F32)
        keep = jnp.asarray(row_group(np.arange(rows))[:, None] == ((ci // col_inner) % ug)[None, :])
        return jnp.where(keep, jnp.matmul(x2d, expand, precision=HIGHEST), 0.0)

    grp16 = lambda r: (r // S5_CH) % ug
    grp64 = lambda r: (r // S5_P) % ug
    step = jnp.exp(log_step)[..., None]
    lam_bar = _cexp(lam_re * step, lam_im * step)
    den = lam_re * lam_re + lam_im * lam_im
    num = (lam_bar[0] - 1.0, lam_bar[1])
    coef = ((num[0] * lam_re + num[1] * lam_im) / den, (num[1] * lam_re - num[0] * lam_im) / den)
    b_bar = _cmul((coef[0][..., None], coef[1][..., None]), (b_re, b_im))
    dd = jnp.arange(J + 1, dtype=F32)[:, None, None, None]
    lam_pow = _cexp(dd * (lam_re * step)[None], dd * (lam_im * step)[None])

    def units(x, g_axis):
        return x.reshape(x.shape[:g_axis] + (nq, ug) + x.shape[g_axis + 1:])

    sel = [jnp.stack([c[:J][::-1, 0], c[:J][:, 1]], axis=0) for c in lam_pow]
    wb = _cmul((sel[0][..., None], sel[1][..., None]), (b_bar[0][:, None], b_bar[1][:, None]))
    wb = units(jnp.stack(wb, axis=0), 3)
    bcat = block_diag(jnp.transpose(wb, (3, 1, 2, 4, 6, 0, 5)).reshape(nq, 2, J * LANES, LANES),
                      grp16, S5_P)
    lp = [jnp.transpose(c[:J], (1, 2, 0, 3))[:, :, :, None, :] for c in lam_pow]
    cl = _cmul((c_re[:, :, None], c_im[:, :, None]), lp)
    kd = (jnp.einsum('dgjcp,dgpk->dgjck', cl[0], b_bar[0], **hp)
          - jnp.einsum('dgjcp,dgpk->dgjck', cl[1], b_bar[1], **hp))
    oi = np.arange(J)[:, None]
    oo = np.arange(J)[None, :]
    lag_hot = np.stack([(oo - oi)[..., None] == np.arange(J), (oi - oo)[..., None] == np.arange(J)])
    t5 = jnp.einsum('dioj,dgjce->dgioce', jnp.asarray(lag_hot, F32), kd, **hp)
    t5 = units(t5[0] + t5[1], 0)
    tsum = block_diag(jnp.transpose(t5, (0, 2, 1, 5, 3, 4)).reshape(nq, J * LANES, LANES),
                      grp16, S5_CH)
    lq = [jnp.transpose(c[1:], (1, 2, 0, 3))[:, :, :, None, :] for c in lam_pow]
    cm = _cmul((c_re[:, :, None], c_im[:, :, None]), lq)
    cc = jnp.stack([cm[0], -cm[1]], axis=0)
    cc = jnp.stack([cc[:, 0], cc[:, 1, :, ::-1]], axis=1)
    ccat = block_diag(jnp.transpose(units(cc, 2), (2, 1, 0, 3, 6, 4, 5)).reshape(nq, 2 * sw, LANES),
                      grp64, S5_CH)
    b1 = jnp.transpose(units(jnp.stack(b_bar, axis=0), 2), (2, 1, 3, 5, 0, 4))
    b1 = block_diag(b1.reshape(nq, 2, LANES, LANES), grp16, S5_P)

    def lanes(c):
        x = jnp.concatenate([c[0].reshape(2, nq, ug * S5_P), c[1].reshape(2, nq, ug * S5_P)], axis=-1)
        return jnp.transpose(x, (1, 0, 2))[:, :, None, :]

    return dict(bcat=bcat.astype(BF16), tsum=tsum.astype(BF16), ccat=ccat.astype(BF16), b1=b1,
                lj=lanes((lam_pow[0][J], lam_pow[1][J])), l1=lanes(lam_bar))


def _s5_kernel(u_ref, bcat_ref, tsum_ref, ccat_ref, lj_ref, l1_ref, b1_ref, x0_ref,
               y_ref, fin_ref, z_scr, *, nseq, n):
    J = S5_J
    nsub = n // J
    ns = nseq * nsub
    hw = S5_UG * S5_P
    if nseq == 1:
        ucat = jnp.concatenate([u_ref[pl.ds(o, ns, stride=J), :] for o in range(J)], axis=1)
    else:
        ucat = jnp.concatenate(
            [jnp.concatenate([u_ref[pl.ds(j * J + o, nseq, stride=n), :] for o in range(J)], axis=1)
             for j in range(nsub)], axis=0)
    ucat = ucat.astype(BF16)
    for d in range(2):
        z_scr[d] = _dot(ucat, bcat_ref[d])

    def cstep(d, x, z):
        lre, lim = lj_ref[d, :, 0:hw], lj_ref[d, :, hw:]
        return jnp.concatenate([lre * x[:, :hw] - lim * x[:, hw:] + z[:, :hw],
                                lre * x[:, hw:] + lim * x[:, :hw] + z[:, hw:]], axis=1)

    for d in range(2):
        tok = 0 if d == 0 else n - 1
        uf = u_ref[pl.ds(tok, nseq, stride=n), :] if nseq > 1 else u_ref[tok:tok + 1, :]
        bu = jnp.dot(uf, b1_ref[d], precision=HIGHEST, preferred_element_type=F32)
        x0 = x0_ref[d]
        lre, lim = l1_ref[d, :, 0:hw], l1_ref[d, :, hw:]
        fin_ref[d] = jnp.concatenate([lre * x0[:, :hw] - lim * x0[:, hw:] + bu[:, :hw],
                                      lre * x0[:, hw:] + lim * x0[:, :hw] + bu[:, hw:]], axis=1)

    if nseq == 1:
        def step(j, carry):
            out = []
            for d in range(2):
                row = pl.ds(j if d == 0 else nsub - 1 - j, 1)
                z = z_scr[d, row, :]
                z_scr[d, row, :] = carry[d]
                out.append(cstep(d, carry[d], z))
            return tuple(out)

        lax.fori_loop(0, nsub, step, (x0_ref[0], x0_ref[1]))
    else:
        xs = [x0_ref[0], x0_ref[1]]
        for j in range(nsub):
            for d in range(2):
                rows = pl.ds((j if d == 0 else nsub - 1 - j) * nseq, nseq)
                z = z_scr[d, rows, :]
                z_scr[d, rows, :] = xs[d]
                xs[d] = cstep(d, xs[d], z)

    xin = jnp.concatenate([z_scr[0], z_scr[1]], axis=1).astype(BF16)
    ycat = _dot(ucat, tsum_ref[...]) + _dot(xin, ccat_ref[...])
    for o in range(J):
        if nseq == 1:
            y_ref[pl.ds(o, ns, stride=J), :] = ycat[:, o * LANES:(o + 1) * LANES]
        else:
            for j in range(nsub):
                y_ref[pl.ds(j * J + o, nseq, stride=n), :] = ycat[j * nseq:(j + 1) * nseq,
                                                                  o * LANES:(o + 1) * LANES]


def _s5_scan(proj, b, n, prm, x0, nseq):
    nq = S5_G // S5_UG
    sw = 2 * S5_UG * S5_P
    jl = S5_J * LANES
    r = nseq * n
    ns = r // S5_J
    if nseq == 1:
        x0 = x0.reshape(nq, 2, b, 1, sw)
        st_spec = pl.BlockSpec((None, 2, None, 1, sw), lambda q, i: (q, 0, i, 0, 0))
    else:
        st_spec = pl.BlockSpec((None, 2, nseq, sw), lambda q, i: (q, 0, i, 0))
    wspec = lambda *shp: pl.BlockSpec((None,) + shp, lambda q, i: (q,) + (0,) * len(shp))
    y, fin = pl.pallas_call(
        functools.partial(_s5_kernel, nseq=nseq, n=n),
        grid=(nq, b // nseq),
        in_specs=[pl.BlockSpec((r, LANES), lambda q, i: (i, C_U // LANES + q)),
                  wspec(2, jl, sw), wspec(jl, jl), wspec(2 * sw, jl), wspec(2, 1, sw),
                  wspec(2, 1, sw), wspec(2, LANES, sw), st_spec],
        out_specs=[pl.BlockSpec((r, LANES), lambda q, i: (i, q)), st_spec],
        out_shape=[jax.ShapeDtypeStruct((b * n, DB), F32), jax.ShapeDtypeStruct(x0.shape, F32)],
        scratch_shapes=[pltpu.VMEM((2, ns, sw), F32)],
        compiler_params=_cparams(2),
    )(proj, prm['bcat'], prm['tsum'], prm['ccat'], prm['lj'], prm['l1'], prm['b1'], x0)
    return y, fin.reshape(nq, 2, b, sw)


def _s5_post(u, y_scan, d_skip, w_glu):
    y = jax.nn.gelu(d_skip * u + y_scan)
    return y * _sigmoid(_dot(y.astype(BF16), w_glu))


def _s5_state_to_lanes(s_re, s_im):
    b = s_re.shape[0]
    nq = S5_G // S5_UG
    x = jnp.concatenate([s_re.reshape(b, 2, nq, S5_UG * S5_P), s_im.reshape(b, 2, nq, S5_UG * S5_P)],
                        axis=-1)
    return jnp.transpose(x, (2, 1, 0, 3))


def _s5_lanes_to_state(x):
    nq, _, b, _ = x.shape
    hw = S5_UG * S5_P
    x = jnp.transpose(x, (2, 1, 0, 3))
    return x[..., :hw].reshape(b, 2, S5_G, S5_P), x[..., hw:].reshape(b, 2, S5_G, S5_P)


def _s5_mixer(proj, b, n, sp, x0):
    nseq = 8 if (n // S5_J <= 64 and b % 8 == 0) else 1
    return _s5_scan(proj, b, n, sp, x0, nseq)


def _conv_kernel(x_ref, prev_ref, next_ref, w_ref, o_ref, *, tiles_per_seq):
    i = pl.program_id(0)
    x = x_ref[...]
    tm = x.shape[0]
    row = lax.broadcasted_iota(jnp.int32, (tm, 1), 0)
    first = (i % tiles_per_seq) == 0
    last = (i % tiles_per_seq) == tiles_per_seq - 1
    prev_row = jnp.where(first, 0.0, prev_ref[7:8, :])
    next_row = jnp.where(last, 0.0, next_ref[0:1, :])
    x_dn = jnp.where(row == 0, prev_row, pltpu.roll(x, 1, axis=0))
    x_up = jnp.where(row == tm - 1, next_row, pltpu.roll(x, tm - 1, axis=0))
    o_ref[...] = x_dn * w_ref[0:1, :] + x * w_ref[1:2, :] + x_up * w_ref[2:3, :]


def _rwkv_conv(proj, conv_w, n, tm):
    t = proj.shape[0]
    w3 = 3 * DB
    cb = C_RKV // w3
    nb8 = t // 8
    return pl.pallas_call(
        functools.partial(_conv_kernel, tiles_per_seq=n // tm),
        grid=(t // tm,),
        in_specs=[pl.BlockSpec((tm, w3), lambda i: (i, cb)),
                  pl.BlockSpec((8, w3), lambda i: (jnp.maximum(i * (tm // 8) - 1, 0), cb)),
                  pl.BlockSpec((8, w3), lambda i: (jnp.minimum((i + 1) * (tm // 8), nb8 - 1), cb)),
                  pl.BlockSpec((3, w3), lambda i: (0, 0))],
        out_specs=pl.BlockSpec((tm, w3), lambda i: (i, 0)),
        out_shape=jax.ShapeDtypeStruct((t, w3), F32),
        compiler_params=_cparams(1),
    )(proj, proj, proj, conv_w)


def _rwkv_chunk_kernel(rc_ref, kc_ref, vc_ref, lora_ref, w0_ref, wup_ref, a0_ref, aup_ref,
                       kk_ref, ka_ref, s0_ref, y_ref, sfin_ref, z_scr, *, nc, nseq):
    d = pl.program_id(1)
    ci = pl.program_id(2)
    L = RW_L
    rows = nseq * L

    @pl.when(ci == 0)
    def _():
        z_scr[...] = s0_ref[...]

    flat = lambda ref: ref[...].reshape(rows, ref.shape[-1])
    rc, kc, vc, lora = flat(rc_ref), flat(kc_ref), flat(vc_ref), flat(lora_ref)
    wd = lora[:, 0:64]
    ad = lora[:, 64:128]
    lora_w = _mm3(jnp.tanh(wd), wup_ref[...])
    log_w = -_softplus(-(w0_ref[...] + lora_w)) - 0.5
    lw = -jnp.exp(log_w)
    a = _sigmoid(a0_ref[...] + _mm3(ad, aup_ref[...]))
    kd = kc * (1.0 + (a - 1.0) * ka_ref[...])
    kk = kc * kk_ref[...]
    kk = kk * lax.rsqrt(_dot_rhs_exact(kk * kk, _head_block_matrix(DB, 1.0)) + 1e-12)
    alpha = -kk
    beta = kk * a

    sgn = 1 - 2 * d
    tt = lax.broadcasted_iota(jnp.int32, (rows, rows), 0)
    ss = lax.broadcasted_iota(jnp.int32, (rows, rows), 1)
    tri = jnp.where(((tt % L - ss % L) * sgn >= 0) & (tt // L == ss // L), 1.0, 0.0).astype(BF16)
    c = _dot_lhs_exact(tri, lw)
    c_ex = c - lw
    ctot = jnp.concatenate(
        [jnp.broadcast_to(jnp.sum(lw[s * L:(s + 1) * L], axis=0, keepdims=True), (L, DB))
         for s in range(nseq)], axis=0)
    mid = 0.5 * ctot
    e_in = jnp.exp(c - mid)
    e_ex = jnp.exp(c_ex - mid)
    e_out = jnp.exp(mid - c)
    e_mid = jnp.exp(mid)
    al_t = alpha * e_ex
    r_t = rc * e_in
    be_t = beta * e_out
    k_t = kd * e_out
    a0s = al_t * e_mid
    r0s = r_t * e_mid
    bps = be_t * e_mid
    kps = k_t * e_mid
    p_l = e_mid * e_mid

    m0 = _pair_masks()
    t2 = lax.broadcasted_iota(jnp.int32, (2 * L, 2 * L), 0) % L
    s2 = lax.broadcasted_iota(jnp.int32, (2 * L, 2 * L), 1) % L
    strict = (t2 - s2) * sgn > 0
    incl = (t2 - s2) * sgn >= 0
    eye = (lax.broadcasted_iota(jnp.int32, (LANES, LANES), 0)
           == lax.broadcasted_iota(jnp.int32, (LANES, LANES), 1))
    eye_f = jnp.where(eye, 1.0, 0.0)
    zeros = jnp.zeros((LANES, LANES), F32)

    chains = [(s, p) for s in range(nseq) for p in range(HEADS // 2)]
    pairs = range(len(chains))
    rws = [slice(s * L, (s + 1) * L) for s, _ in chains]
    sls = [slice(p * LANES, (p + 1) * LANES) for _, p in chains]

    def stack(x, p):
        xp = x[rws[p], sls[p]]
        return jnp.concatenate([jnp.where(m0, xp, 0.0), jnp.where(m0, 0.0, xp)], axis=0)

    raws = [_dot_nt(jnp.concatenate([stack(al_t, p), stack(r_t, p)], axis=0).astype(BF16),
                    jnp.concatenate([stack(be_t, p), stack(k_t, p)], axis=0).astype(BF16))
            for p in pairs]
    amat = [jnp.where(strict, r[:LANES, :LANES], 0.0) for r in raws]
    bmat = [jnp.where(strict, r[:LANES, LANES:], 0.0) for r in raws]
    qbk = [jnp.concatenate([jnp.where(incl, r[LANES:, :LANES], 0.0),
                            jnp.where(incl, r[LANES:, LANES:], 0.0)], axis=1) for r in raws]

    smat = [eye_f + a for a in amat]
    pw = [_mm_inv(a, a) for a in amat]
    for _ in range(4):
        xs = [_mm_inv(pw[p], jnp.concatenate([pw[p], smat[p]], axis=1)) for p in pairs]
        smat = [smat[p] + xs[p][:, LANES:] for p in pairs]
        pw = [x[:, :LANES] for x in xs]
    tmat = [smat[p] + _mm_inv(pw[p], smat[p]) for p in pairs]

    vs = [stack(vc, p) for p in pairs]
    bv = [_mm(bmat[p], vs[p]) for p in pairs]
    wu = [_mm_inv(tmat[p], jnp.concatenate([stack(a0s, p), bv[p]], axis=1)) for p in pairs]
    rhs2 = [jnp.concatenate([wu[p], jnp.concatenate([zeros, vs[p]], axis=1)], axis=0).astype(BF16)
            for p in pairs]
    out_a = [_dot(qbk[p].astype(BF16), rhs2[p]) for p in pairs]
    out_b = [_dot_tn(jnp.concatenate([stack(bps, p), stack(kps, p)], axis=0).astype(BF16), rhs2[p])
             for p in pairs]
    hz = [_mm_state(jnp.concatenate(
        [stack(r0s, p) + out_a[p][:, :LANES],
         jnp.where(eye, p_l[rws[p], sls[p]][0:1], 0.0) + out_b[p][:, :LANES]], axis=0), z_scr[chains[p]])
          for p in pairs]
    for p in pairs:
        y = out_a[p][:, LANES:] + hz[p][:LANES]
        z_scr[chains[p]] = hz[p][LANES:] + out_b[p][:, LANES:]
        y_ref[chains[p][0], :, sls[p]] = y[:L] + y[L:]

    @pl.when(ci == nc - 1)
    def _():
        sfin_ref[...] = z_scr[...]


def _rwkv_post(y, rkv_ref, gd, g_up, u_bonus, ln_w, ln_b):
    avg = _head_block_matrix(DB, 1.0 / DH)
    mu = _dot_rhs_exact(y, avg)
    yc = y - mu
    var = _dot_rhs_exact(yc * yc, avg)
    yn = yc * lax.rsqrt(var + GN_EPS) * ln_w + ln_b
    rc, kc, vc = rkv_ref[:, 0:DB], rkv_ref[:, DB:2 * DB], rkv_ref[:, 2 * DB:3 * DB]
    bonus = _dot_rhs_exact(rc * kc * u_bonus, _head_block_matrix(DB, 1.0)) * vc
    g = _dot(_sigmoid(gd).astype(BF16), g_up)
    return (yn + bonus) * g


def _rwkv_mixer(proj, b, n, conv_w, prm, s0_pairs, tm):
    t = b * n
    nc = n // RW_L
    rkv = _rwkv_conv(proj, conv_w, n, min(tm, n))

    nseq = 4 if b % 4 == 0 else (2 if b % 2 == 0 else 1)
    hp = HEADS // 2

    def chunk(dd, ci):
        return jnp.where(dd == 0, ci, nc - 1 - ci)

    tok = lambda w, col: pl.BlockSpec((nseq, RW_L, w), lambda i, dd, ci: (i, chunk(dd, ci), col))
    dirp = lambda r: pl.BlockSpec((None, r, DB), lambda i, dd, ci: (dd, 0, 0))
    shared = pl.BlockSpec((1, DB), lambda i, dd, ci: (0, 0))
    state = pl.BlockSpec((nseq, None, hp, LANES, LANES), lambda i, dd, ci: (i, dd, 0, 0, 0))
    rkv3 = rkv.reshape(b, n, 3 * DB)
    y2, sfin = pl.pallas_call(
        functools.partial(_rwkv_chunk_kernel, nc=nc, nseq=nseq),
        grid=(b // nseq, 2, nc),
        in_specs=[tok(DB, 0), tok(DB, 1), tok(DB, 2), tok(256, C_LORA // 256),
                  dirp(1), dirp(64), dirp(1), dirp(64), shared, shared, state],
        out_specs=[pl.BlockSpec((None, nseq, RW_L, DB), lambda i, dd, ci: (dd, i, chunk(dd, ci), 0)),
                   state],
        out_shape=[jax.ShapeDtypeStruct((2, b, n, DB), F32),
                   jax.ShapeDtypeStruct((b, 2, hp, LANES, LANES), F32)],
        scratch_shapes=[pltpu.VMEM((nseq, hp, LANES, LANES), F32)],
        compiler_params=_cparams(3),
    )(rkv3, rkv3, rkv3, proj.reshape(b, n, W_MAIN), prm['w0'], prm['w_up'], prm['a0'], prm['a_up'],
      prm['k_k'], prm['k_a'], s0_pairs)
    return y2.reshape(2, t, DB), rkv, sfin


def _state_to_pairs(s):
    b = s.shape[0]
    st = jnp.swapaxes(s, -1, -2).reshape(b, 2, HEADS // 2, 2, DH, DH)
    z = jnp.zeros((b, 2, HEADS // 2, 2, DH, 2, DH), F32)
    z = z.at[:, :, :, 0, :, 0, :].set(st[:, :, :, 0])
    z = z.at[:, :, :, 1, :, 1, :].set(st[:, :, :, 1])
    return z.reshape(b, 2, HEADS // 2, LANES, LANES)


def _pairs_to_state(z):
    b = z.shape[0]
    z = z.reshape(b, 2, HEADS // 2, 2, DH, 2, DH)
    st = jnp.stack([z[:, :, :, 0, :, 0, :], z[:, :, :, 1, :, 1, :]], axis=3)
    return jnp.swapaxes(st.reshape(b, 2, HEADS, DH, DH), -1, -2)


def _merge_kernel(u_ref, ys5_ref, d_ref, wglu_ref,
                  yf_ref, yb_ref, rkv_ref, gd_ref, gup_ref, ub_ref, lnw_ref, lnb_ref,
                  oatt_ref, gs5_ref, grw_ref, gatt_ref, x_ref, ws5_ref, wrw_ref, watt_ref, wmix_ref,
                  gpost_ref, g1_ref, o_ref):
    o_s5 = _s5_post(u_ref[...], ys5_ref[...], d_ref[...], wglu_ref[...])
    o_rw = _rwkv_post(yf_ref[...] + yb_ref[...], rkv_ref, gd_ref[:, 128:256], gup_ref[...],
                      ub_ref[...], lnw_ref[...], lnb_ref[...])
    gate = lambda ref: _sigmoid(ref[...].astype(F32))
    merged = (gate(gs5_ref) * _dot(o_s5.astype(BF16), ws5_ref[...])
              + gate(grw_ref) * _dot(o_rw.astype(BF16), wrw_ref[...])
              + gate(gatt_ref) * _dot(oatt_ref[...].astype(BF16), watt_ref[...]))
    m = _dot(merged.astype(BF16), wmix_ref[...])
    o_ref[...] = x_ref[...] + g1_ref[...] * _rms(m, gpost_ref[...])


def _merge(main, y_s5, y_rw, rkv, o_att, gates, x, s5_d, w_glu_b, rwp, wts, mod_l, row_of_tile,
           g_post, tm):
    t = x.shape[0]
    tok = lambda w, c: pl.BlockSpec((tm, w), lambda i: (i, c))
    full = lambda r, c: pl.BlockSpec((r, c), lambda i: (0, 0))
    return pl.pallas_call(
        _merge_kernel,
        grid=(t // tm,),
        in_specs=[tok(DB, C_U // DB), tok(DB, 0), full(1, DB), full(DB, DB),
                  pl.BlockSpec((None, tm, DB), lambda i: (0, i, 0)),
                  pl.BlockSpec((None, tm, DB), lambda i: (1, i, 0)),
                  tok(3 * DB, 0), tok(256, C_LORA // 256), full(128, DB), full(1, DB), full(1, DB),
                  full(1, DB),
                  tok(DB, 0), tok(D, 0), tok(D, 1), tok(D, 2), tok(D, 0),
                  full(DB, D), full(DB, D), full(DB, D), full(D, D), full(1, D),
                  _mod_spec(2, row_of_tile, 1)],
        out_specs=tok(D, 0),
        out_shape=jax.ShapeDtypeStruct((t, D), F32),
        compiler_params=_cparams(1),
    )(main, y_s5, s5_d, w_glu_b, y_rw, y_rw, rkv, main, rwp['g_up'], rwp['u'], rwp['ln_w'],
      rwp['ln_b'], o_att, gates, gates, gates, x, wts['br_s5'], wts['br_rw'], wts['br_att'],
      wts['mix'], g_post, mod_l)


def _ffn_dense_kernel(x_ref, gpre_ref, sc_ref, sh_ref, w1_ref, w3_ref, w2_ref, gpost_ref, g2_ref,
                      o_ref, h_scr, acc_scr):
    f = pl.program_id(1)

    @pl.when(f == 0)
    def _():
        h = _rms(x_ref[...], gpre_ref[...]) * (1.0 + sc_ref[...]) + sh_ref[...]
        h_scr[...] = h.astype(BF16)
        acc_scr[...] = jnp.zeros_like(acc_scr)

    h = h_scr[...]
    hid = _silu(_dot(h, w1_ref[...])) * _dot(h, w3_ref[...])
    acc_scr[...] += _dot(hid.astype(BF16), w2_ref[...])

    @pl.when(f == pl.num_programs(1) - 1)
    def _():
        o_ref[...] = x_ref[...] + g2_ref[...] * _rms(acc_scr[...], gpost_ref[...])


def _ffn_dense(x, mod_l, row_of_tile, g_pre, g_post, w1, w3, w2, tm, tf):
    t = x.shape[0]
    return pl.pallas_call(
        _ffn_dense_kernel,
        grid=(t // tm, FF_DENSE // tf),
        in_specs=[pl.BlockSpec((tm, D), lambda i, f: (i, 0)),
                  pl.BlockSpec((1, D), lambda i, f: (0, 0)),
                  _mod_spec(4, row_of_tile, 2), _mod_spec(3, row_of_tile, 2),
                  pl.BlockSpec((D, tf), lambda i, f: (0, f)),
                  pl.BlockSpec((D, tf), lambda i, f: (0, f)),
                  pl.BlockSpec((tf, D), lambda i, f: (f, 0)),
                  pl.BlockSpec((1, D), lambda i, f: (0, 0)),
                  _mod_spec(5, row_of_tile, 2)],
        out_specs=pl.BlockSpec((tm, D), lambda i, f: (i, 0)),
        out_shape=jax.ShapeDtypeStruct((t, D), F32),
        scratch_shapes=[pltpu.VMEM((tm, D), BF16), pltpu.VMEM((tm, D), F32)],
        compiler_params=_cparams(2),
    )(x, g_pre, mod_l, mod_l, w1, w3, w2, g_post, mod_l)


def _ffn_moe_kernel(x_ref, gpre_ref, sc_ref, sh_ref, rw_ref, rb_ref, w1_ref, w3_ref, w2_ref,
                    gpost_ref, g2_ref, o_ref, h_scr, comb_scr, acc_scr):
    e = pl.program_id(1)
    lane = lax.broadcasted_iota(jnp.int32, (1, LANES), 1)

    @pl.when(e == 0)
    def _():
        h = _rms(x_ref[...], gpre_ref[...]) * (1.0 + sc_ref[...]) + sh_ref[...]
        h_scr[...] = h.astype(BF16)
        acc_scr[...] = jnp.zeros_like(acc_scr)
        logits = jnp.dot(h, rw_ref[...], precision=HIGHEST, preferred_element_type=F32) + rb_ref[...]
        ex = jnp.exp(logits - jnp.max(logits, axis=-1, keepdims=True))
        probs = ex / jnp.sum(ex, axis=-1, keepdims=True)
        p1 = jnp.max(probs, axis=-1, keepdims=True)
        i1 = jnp.min(jnp.where(probs == p1, lane, LANES), axis=-1, keepdims=True)
        rest = jnp.where(lane == i1, -1.0, probs)
        p2 = jnp.max(rest, axis=-1, keepdims=True)
        i2 = jnp.min(jnp.where(rest == p2, lane, LANES), axis=-1, keepdims=True)
        den = p1 + p2
        comb_scr[...] = jnp.where(lane == i1, p1 / den, 0.0) + jnp.where(lane == i2, p2 / den, 0.0)

    h = h_scr[...]
    cw = jnp.sum(jnp.where(lane == e, comb_scr[...], 0.0), axis=-1, keepdims=True)
    hid = _silu(_dot(h, w1_ref[...])) * _dot(h, w3_ref[...])
    acc_scr[...] += cw * _dot(hid.astype(BF16), w2_ref[...])

    @pl.when(e == N_EXP - 1)
    def _():
        o_ref[...] = x_ref[...] + g2_ref[...] * _rms(acc_scr[...], gpost_ref[...])


def _ffn_moe(x, mod_l, row_of_tile, g_pre, g_post, rw, rb, w1, w3, w2, tm):
    t = x.shape[0]
    return pl.pallas_call(
        _ffn_moe_kernel,
        grid=(t // tm, N_EXP),
        in_specs=[pl.BlockSpec((tm, D), lambda i, e: (i, 0)),
                  pl.BlockSpec((1, D), lambda i, e: (0, 0)),
                  _mod_spec(4, row_of_tile, 2), _mod_spec(3, row_of_tile, 2),
                  pl.BlockSpec((D, LANES), lambda i, e: (0, 0)),
                  pl.BlockSpec((1, LANES), lambda i, e: (0, 0)),
                  pl.BlockSpec((None, D, FF_EXP), lambda i, e: (e, 0, 0)),
                  pl.BlockSpec((None, D, FF_EXP), lambda i, e: (e, 0, 0)),
                  pl.BlockSpec((None, FF_EXP, D), lambda i, e: (e, 0, 0)),
                  pl.BlockSpec((1, D), lambda i, e: (0, 0)),
                  _mod_spec(5, row_of_tile, 2)],
        out_specs=pl.BlockSpec((tm, D), lambda i, e: (i, 0)),
        out_shape=jax.ShapeDtypeStruct((t, D), F32),
        scratch_shapes=[pltpu.VMEM((tm, D), BF16), pltpu.VMEM((tm, LANES), F32),
                        pltpu.VMEM((tm, D), F32)],
        compiler_params=_cparams(2),
    )(x, g_pre, mod_l, mod_l, rw, rb, w1, w3, w2, g_post, mod_l)


def _reorder_w_in(w):
    qkv, u, rkv, lora, gates = (w[:, 0:1536], w[:, 1536:2048], w[:, 2048:3584], w[:, 3584:3840],
                                w[:, 3840:6912])
    return jnp.concatenate([gates, rkv, u, lora, qkv], axis=1)


def _layer(x, b, n, l, mod_l, row_of_tile, P, cache, tm):
    row = lambda a: a.reshape(1, -1)
    tm_proj = min(2 * tm, x.shape[0], n) if cache is not None else min(2 * tm, x.shape[0])
    row_of_ptile = (lambda i: row_of_tile(i * (tm_proj // tm)))
    outs = _projection(x, mod_l, row_of_ptile, row(P['norm_pre_mix'][l]), P['w_in_b'][l], tm_proj,
                       cache is None)
    gates, proj, qkv = outs[:3]
    if cache is None:
        o_att = _ctx_attention(qkv, b, n)
        x0 = jnp.zeros((S5_G // S5_UG, 2, b, 2 * S5_UG * S5_P), F32)
        s0 = jnp.zeros((b, 2, HEADS // 2, LANES, LANES), F32)
    else:
        k_c, v_c, s5re, s5im, rw0 = cache
        bias = _na_bias_tables(P['att_rpb'][l], n // GRID_W)
        o_att = _na_attention(qkv, k_c.reshape(-1, DB).astype(BF16), v_c.reshape(-1, DB).astype(BF16),
                              bias, b, n)
        x0 = _s5_state_to_lanes(s5re, s5im)
        s0 = _state_to_pairs(rw0)
    y_s5, s5fin = _s5_mixer(proj, b, n, P['s5'][l], x0)
    y_rw, rkv, sfin = _rwkv_mixer(proj, b, n, P['rwkv_conv'][l], P['rwkv'][l], s0, tm)
    x = _merge(proj, y_s5, y_rw, rkv, o_att, gates, x, row(P['s5_d'][l]), P['s5_w_glu_b'][l],
               P['rwkv'][l], P['merge'][l], mod_l, row_of_tile, row(P['norm_post_mix'][l]), tm)
    i = l // 2
    if l % 2 == 0:
        x = _ffn_dense(x, mod_l, row_of_tile, row(P['norm_pre_ffn'][l]), row(P['norm_post_ffn'][l]),
                       P['dense_w1_b'][i], P['dense_w3_b'][i], P['dense_w2_b'][i], tm, FF_DENSE // 2)
    else:
        x = _ffn_moe(x, mod_l, row_of_tile, row(P['norm_pre_ffn'][l]), row(P['norm_post_ffn'][l]),
                     P['moe_rw'][i], P['moe_rb'][i], P['moe_w1_b'][i], P['moe_w3_b'][i],
                     P['moe_w2_b'][i], tm)
    if cache is None:
        k_new = outs[3][:, DB:2 * DB]
        v_new = outs[3][:, 2 * DB:3 * DB]
        fre, fim = _s5_lanes_to_state(s5fin)
        return x, (k_new, v_new, fre, fim, sfin)
    return x, None


def kernel(x_prompt, x_sample, cache_k, cache_v, state_s5_re, state_s5_im, state_rwkv, c, c_ctx,
           w_ada, b_ada, norm_pre_mix, norm_post_mix, norm_pre_ffn, norm_post_ffn, w_in,
           s5_lam_re, s5_lam_im, s5_log_step, s5_b_re, s5_b_im, s5_c_re, s5_c_im, s5_d, s5_w_glu,
           rwkv_conv, rwkv_w0, rwkv_w_up, rwkv_a0, rwkv_a_up, rwkv_g_up, rwkv_k_k, rwkv_k_a,
           rwkv_u, rwkv_ln_w, rwkv_ln_b, att_rpb, w_br_s5, w_br_rwkv, w_br_att, w_mix_out,
           dense_w1, dense_w3, dense_w2, moe_router_w, moe_router_b, moe_w1, moe_w3, moe_w2):
    bc, nc_, _ = x_prompt.shape
    bl, nl, _ = x_sample.shape
    tm_c = min(512, bc * nc_)
    tm_l = min(512, nl)

    P = dict(norm_pre_mix=norm_pre_mix, norm_post_mix=norm_post_mix, norm_pre_ffn=norm_pre_ffn,
             norm_post_ffn=norm_post_ffn, s5_d=s5_d, rwkv_conv=rwkv_conv, att_rpb=att_rpb)
    P['w_in_b'] = [_reorder_w_in(w_in[l]).astype(BF16) for l in range(DEPTH)]
    P['s5_w_glu_b'] = s5_w_glu.astype(BF16)
    P['dense_w1_b'], P['dense_w3_b'], P['dense_w2_b'] = (dense_w1.astype(BF16), dense_w3.astype(BF16),
                                                         dense_w2.astype(BF16))
    P['moe_w1_b'], P['moe_w3_b'], P['moe_w2_b'] = (moe_w1.astype(BF16), moe_w3.astype(BF16),
                                                   moe_w2.astype(BF16))
    n_moe = moe_router_w.shape[0]
    P['moe_rw'] = jnp.pad(moe_router_w, ((0, 0), (0, 0), (0, LANES - N_EXP)))
    P['moe_rb'] = jnp.pad(moe_router_b, ((0, 0), (0, LANES - N_EXP)),
                          constant_values=NEG).reshape(n_moe, 1, LANES)
    P['merge'] = [dict(br_s5=w_br_s5[l].astype(BF16), br_rw=w_br_rwkv[l].astype(BF16),
                       br_att=w_br_att[l].astype(BF16), mix=w_mix_out[l].astype(BF16))
                  for l in range(DEPTH)]
    P['s5'] = []
    P['rwkv'] = []
    for l in range(DEPTH):
        P['s5'].append(_s5_params(s5_lam_re[l], s5_lam_im[l], s5_log_step[l], s5_b_re[l], s5_b_im[l],
                                  s5_c_re[l], s5_c_im[l]))
        P['rwkv'].append(dict(
            w0=rwkv_w0[l].reshape(2, 1, DB), w_up=rwkv_w_up[l], a0=rwkv_a0[l].reshape(2, 1, DB),
            a_up=rwkv_a_up[l], k_k=rwkv_k_k[l].reshape(1, DB), k_a=rwkv_k_a[l].reshape(1, DB),
            g_up=rwkv_g_up[l].astype(BF16), u=rwkv_u[l].reshape(1, DB),
            ln_w=rwkv_ln_w[l].reshape(1, DB), ln_b=rwkv_ln_b[l].reshape(1, DB)))

    cvec = jnp.zeros((8, D), F32).at[0].set(c_ctx).at[1:1 + bl].set(c)
    mod = _modulation(cvec, w_ada, b_ada)

    xp = x_prompt.reshape(bc * nc_, D)
    ks, vs, s5re, s5im, rws = [], [], [], [], []
    for l in range(DEPTH):
        xp, (k_n, v_n, fre, fim, sfin) = _layer(xp, bc, nc_, l, mod[l], lambda i: 0, P, None, tm_c)
        ks.append(k_n.reshape(bc, nc_, HEADS, DH))
        vs.append(v_n.reshape(bc, nc_, HEADS, DH))
        s5re.append(fre)
        s5im.append(fim)
        rws.append(_pairs_to_state(sfin))
    new_k = jnp.stack(ks, axis=1)
    new_v = jnp.stack(vs, axis=1)
    new_s5_re = jnp.stack(s5re, axis=1)
    new_s5_im = jnp.stack(s5im, axis=1)
    new_rwkv = jnp.stack(rws, axis=1)

    xs = x_sample.reshape(bl * nl, D)
    tiles_per_seq = nl // tm_l
    for l in range(DEPTH):
        cache = (cache_k[:, l], cache_v[:, l], state_s5_re[:, l], state_s5_im[:, l], state_rwkv[:, l])
        xs, _ = _layer(xs, bl, nl, l, mod[l], lambda i: 1 + i // tiles_per_seq, P, cache, tm_l)

    return (xp.reshape(bc, nc_, D), xs.reshape(bl, nl, D), new_k, new_v, new_s5_re, new_s5_im,
            new_rwkv)
```

```python
import functools
import math

import numpy as np
import jax
import jax.numpy as jnp
from jax import lax
from jax.experimental import pallas as pl
from jax.experimental.pallas import tpu as pltpu

F32 = jnp.float32
BF16 = jnp.bfloat16
HIGHEST = lax.Precision.HIGHEST

D = 1024
DEPTH = 2
GRID_W = 64
DH = 64
HEADS = 8
DB = 512
WIN_ROWS = 8
WIN_COLS = 16
S5_CH = 16
S5_G = 32
S5_P = 64
S5_J = 8
S5_UG = 8
FF_DENSE = 2816
N_EXP = 8
FF_EXP = 1024
EPS = 1e-6
GN_EPS = 64e-5
NEG = -1e30

D_IN = 6912
TN_PROJ = 768
J_MAIN = 4
J_QKV = 7
W_MAIN = (J_QKV - J_MAIN) * TN_PROJ
C_RKV = 0
C_U = 1536
C_LORA = 2048

LANES = 128
RW_L = 64
NA_RB = 4
NA_KR = 12
VMEM_LIMIT = 56 * 1024 * 1024


def _cparams(n_axes, vmem=VMEM_LIMIT):
    return pltpu.CompilerParams(dimension_semantics=("arbitrary",) * n_axes,
                                vmem_limit_bytes=vmem)


def _dot(a, b):
    return jnp.dot(a, b, preferred_element_type=F32)


def _dot_nt(a, b):
    return lax.dot_general(a, b, (((1,), (1,)), ((), ())), preferred_element_type=F32)


def _dot_tn(a, b):
    return lax.dot_general(a, b, (((0,), (0,)), ((), ())), preferred_element_type=F32)


def _split2(x):
    hi = x.astype(BF16)
    lo = (x - hi.astype(F32)).astype(BF16)
    return hi, lo


def _split3(x):
    x1 = x.astype(BF16)
    r1 = x - x1.astype(F32)
    x2 = r1.astype(BF16)
    x3 = (r1 - x2.astype(F32)).astype(BF16)
    return x1, x2, x3


def _dot_rhs_exact(x, m):
    x1, x2, x3 = _split3(x)
    return _dot(x1, m) + _dot(x2, m) + _dot(x3, m)


def _dot_lhs_exact(m, x):
    x1, x2, x3 = _split3(x)
    return _dot(m, x1) + _dot(m, x2) + _dot(m, x3)


def _mm3(a, b):
    a1, a2 = _split2(a)
    b1, b2 = _split2(b)
    return _dot(a1, b1) + _dot(a1, b2) + _dot(a2, b1)


def _mm(a, b):
    return _dot(a.astype(BF16), b.astype(BF16))


_mm_inv = _mm
_mm_state = _mm


def _sigmoid(x):
    return 0.5 * (jnp.tanh(0.5 * x) + 1.0)


def _silu(x):
    return x * _sigmoid(x)


def _softplus(x):
    return jnp.maximum(x, 0.0) + jnp.log(1.0 + jnp.exp(-jnp.abs(x)))


def _rms(x, g):
    return x * lax.rsqrt(jnp.mean(x * x, axis=-1, keepdims=True) + EPS) * g


def _head_block_matrix(width, value):
    r = lax.broadcasted_iota(jnp.int32, (width, width), 0) // DH
    c = lax.broadcasted_iota(jnp.int32, (width, width), 1) // DH
    return jnp.where(r == c, value, 0.0).astype(BF16)


def _mod_kernel(c_ref, w_ref, b_ref, o_ref):
    s = _silu(c_ref[...]).astype(BF16)
    o_ref[...] = _dot(s, w_ref[...].astype(BF16)) + b_ref[...]


def _modulation(cvec, w_ada, b_ada):
    tn = 1536
    out = pl.pallas_call(
        _mod_kernel,
        grid=(DEPTH, 6 * D // tn),
        in_specs=[pl.BlockSpec((8, D), lambda l, j: (0, 0)),
                  pl.BlockSpec((None, D, tn), lambda l, j: (l, 0, j)),
                  pl.BlockSpec((None, 1, tn), lambda l, j: (l, 0, j))],
        out_specs=pl.BlockSpec((None, 8, tn), lambda l, j: (l, 0, j)),
        out_shape=jax.ShapeDtypeStruct((DEPTH, 8, 6 * D), F32),
        compiler_params=_cparams(2),
    )(cvec, w_ada, b_ada.reshape(DEPTH, 1, 6 * D))
    return out.reshape(DEPTH, 8, 1, 6 * D)


def _mod_spec(part, row_of_tile, n_grid):
    if n_grid == 1:
        return pl.BlockSpec((None, 1, D), lambda i: (row_of_tile(i), 0, part))
    return pl.BlockSpec((None, 1, D), lambda i, j: (row_of_tile(i), 0, part))


def _proj_kernel(x_ref, g_ref, sc_ref, sh_ref, w_ref, og_ref, om_ref, oq_ref, *rest, f32_qkv):
    h_scr = rest[-1]
    j = pl.program_id(1)

    @pl.when(j == 0)
    def _():
        h = _rms(x_ref[...], g_ref[...]) * (1.0 + sc_ref[...]) + sh_ref[...]
        h_scr[...] = h.astype(BF16)

    acc = _dot(h_scr[...], w_ref[...])

    @pl.when(j < J_MAIN)
    def _():
        og_ref[...] = acc.astype(BF16)

    @pl.when((j >= J_MAIN) & (j < J_QKV))
    def _():
        om_ref[...] = acc

    @pl.when(j >= J_QKV)
    def _():
        oq_ref[...] = acc.astype(BF16)
        if f32_qkv:
            rest[0][...] = acc


def _projection(x, mod_l, row_of_tile, g_pre, w_in_b, tm, f32_qkv):
    t = x.shape[0]
    n_tiles = D_IN // TN_PROJ
    tile = lambda j0, j1: pl.BlockSpec((tm, TN_PROJ), lambda i, j: (i, jnp.clip(j - j0, 0, j1 - j0 - 1)))
    out_specs = [tile(0, J_MAIN), tile(J_MAIN, J_QKV), tile(J_QKV, n_tiles)]
    out_shape = [jax.ShapeDtypeStruct((t, 3 * D), BF16), jax.ShapeDtypeStruct((t, W_MAIN), F32),
                 jax.ShapeDtypeStruct((t, 3 * DB), BF16)]
    if f32_qkv:
        out_specs.append(tile(J_QKV, n_tiles))
        out_shape.append(jax.ShapeDtypeStruct((t, 3 * DB), F32))
    return pl.pallas_call(
        functools.partial(_proj_kernel, f32_qkv=f32_qkv),
        grid=(t // tm, n_tiles),
        in_specs=[pl.BlockSpec((tm, D), lambda i, j: (i, 0)),
                  pl.BlockSpec((1, D), lambda i, j: (0, 0)),
                  _mod_spec(1, row_of_tile, 2),
                  _mod_spec(0, row_of_tile, 2),
                  pl.BlockSpec((D, TN_PROJ), lambda i, j: (0, j))],
        out_specs=out_specs,
        out_shape=out_shape,
        scratch_shapes=[pltpu.VMEM((tm, D), BF16)],
        compiler_params=_cparams(2),
    )(x, g_pre, mod_l, mod_l, w_in_b)


def _pair_masks():
    lane = lax.broadcasted_iota(jnp.int32, (1, LANES), 1)
    return lane < DH


def _softmax_pv(parts):
    m = None
    for s, _ in parts:
        mx = jnp.max(s, axis=-1, keepdims=True)
        m = mx if m is None else jnp.maximum(m, mx)
    es = [jnp.exp(s - m) for s, _ in parts]
    den = None
    for e in es:
        sm = jnp.sum(e, axis=-1, keepdims=True)
        den = sm if den is None else den + sm
    out = None
    for e, (_, v) in zip(es, parts):
        o = _dot(e.astype(BF16), v)
        out = o if out is None else out + o
    return out * (1.0 / den)


def _ctx_att_kernel(q_ref, k_ref, v_ref, o_ref):
    scale = DH ** -0.5
    m0 = _pair_masks()
    for p in range(HEADS // 2):
        sl = slice(p * LANES, (p + 1) * LANES)
        qp, kp, vp = q_ref[:, sl] * scale, k_ref[:, sl], v_ref[:, sl]
        o_pair = None
        for hh in range(2):
            msk = m0 if hh == 0 else jnp.logical_not(m0)
            qm = jnp.where(msk, qp, jnp.zeros_like(qp))
            o = _softmax_pv([(_dot_nt(qm, kp), vp)])
            o_pair = o if o_pair is None else jnp.where(m0, o_pair, o)
        o_ref[:, sl] = o_pair


def _ctx_attention(qkv, b, n):
    return pl.pallas_call(
        _ctx_att_kernel,
        grid=(b,),
        in_specs=[pl.BlockSpec((n, DB), lambda i: (i, 0)),
                  pl.BlockSpec((n, DB), lambda i: (i, 1)),
                  pl.BlockSpec((n, DB), lambda i: (i, 2))],
        out_specs=pl.BlockSpec((n, DB), lambda i: (i, 0)),
        out_shape=jax.ShapeDtypeStruct((b * n, DB), F32),
        compiler_params=_cparams(1),
    )(qkv, qkv, qkv)


def _na_key_start(rb, rows):
    return jnp.clip(rb * NA_RB - WIN_ROWS // 2, 0, rows - NA_KR)


def _na_kernel(q_ref, k_ref, v_ref, kc_ref, vc_ref, bias_ref, o_ref, *, rows):
    scale = DH ** -0.5
    m0 = _pair_masks()
    rb = pl.program_id(1)
    start = pl.multiple_of(_na_key_start(rb, rows) * GRID_W, GRID_W)
    nk = NA_KR * GRID_W
    for p in range(HEADS // 2):
        sl = slice(p * LANES, (p + 1) * LANES)
        qp = q_ref[:, sl] * scale
        kp = k_ref[pl.ds(start, nk), sl]
        vp = v_ref[pl.ds(start, nk), sl]
        kcp, vcp = kc_ref[:, sl], vc_ref[:, sl]
        o_pair = None
        for hh in range(2):
            msk = m0 if hh == 0 else jnp.logical_not(m0)
            qm = jnp.where(msk, qp, jnp.zeros_like(qp))
            s_loc = _dot_nt(qm, kp) + bias_ref[2 * p + hh]
            s_ctx = _dot_nt(qm, kcp)
            o = _softmax_pv([(s_loc, vp), (s_ctx, vcp)])
            o_pair = o if o_pair is None else jnp.where(m0, o_pair, o)
        o_ref[:, sl] = o_pair


def _na_bias_tables(rpb, rows):
    n_rb = rows // NA_RB
    wr = min(WIN_ROWS, rows)
    nrel_r, nrel_c = 2 * WIN_ROWS - 1, 2 * WIN_COLS - 1
    qc = np.arange(GRID_W)[:, None]
    kc = np.arange(GRID_W)[None, :]
    cs = np.clip(qc - WIN_COLS // 2, 0, GRID_W - WIN_COLS)
    col_ok = (kc >= cs) & (kc < cs + WIN_COLS)
    col_rel = np.clip(kc - qc + (WIN_COLS - 1), 0, nrel_c - 1)
    col_hot = (np.arange(nrel_c)[:, None, None] == col_rel[None]) & col_ok[None]
    col_hot = jnp.asarray(col_hot.reshape(nrel_c, GRID_W * GRID_W), F32)
    row_hot = np.zeros((3, NA_RB, NA_KR, nrel_r), np.float32)
    row_ok = np.zeros((3, NA_RB, NA_KR), bool)
    for vi, rb in enumerate((0, 1, n_rb - 1)):
        u0 = int(np.clip(rb * NA_RB - WIN_ROWS // 2, 0, rows - NA_KR))
        for qr in range(NA_RB):
            r = rb * NA_RB + qr
            rs = int(np.clip(r - wr // 2, 0, rows - wr))
            for kr in range(NA_KR):
                krow = u0 + kr
                if rs <= krow < rs + wr:
                    row_ok[vi, qr, kr] = True
                    row_hot[vi, qr, kr, krow - r + (WIN_ROWS - 1)] = 1.0
    t1 = jnp.einsum('vqkr,hrc->vhqkc', jnp.asarray(row_hot), rpb, precision=HIGHEST)
    t2 = jnp.dot(t1.reshape(-1, nrel_c), col_hot, precision=HIGHEST)
    t2 = t2.reshape(3, HEADS, NA_RB, NA_KR, GRID_W, GRID_W)
    ok = jnp.asarray(row_ok[:, None, :, :, None, None] & col_ok[None, None, None, None])
    t2 = jnp.where(ok, t2, NEG)
    return jnp.transpose(t2, (0, 1, 2, 4, 3, 5)).reshape(3, HEADS, NA_RB * GRID_W, NA_KR * GRID_W)


def _na_attention(qkv, k_ctx, v_ctx, bias, l, b, n):
    rows = n // GRID_W
    n_rb = rows // NA_RB
    tq = NA_RB * GRID_W
    lc = k_ctx.shape[0] // b

    def variant(j):
        return jnp.where(j == 0, 0, jnp.where(j == n_rb - 1, 2, 1))

    return pl.pallas_call(
        functools.partial(_na_kernel, rows=rows),
        grid=(b, n_rb),
        in_specs=[pl.BlockSpec((tq, DB), lambda i, j: (i * n_rb + j, 0)),
                  pl.BlockSpec((n, DB), lambda i, j: (i, 1)),
                  pl.BlockSpec((n, DB), lambda i, j: (i, 2)),
                  pl.BlockSpec((lc, DB), lambda i, j: (i, 0)),
                  pl.BlockSpec((lc, DB), lambda i, j: (i, 0)),
                  pl.BlockSpec((None, None, HEADS, tq, NA_KR * GRID_W),
                               lambda i, j: (l, variant(j), 0, 0, 0))],
        out_specs=pl.BlockSpec((tq, DB), lambda i, j: (i * n_rb + j, 0)),
        out_shape=jax.ShapeDtypeStruct((b * n, DB), F32),
        compiler_params=_cparams(2),
    )(qkv, qkv, qkv, k_ctx, v_ctx, bias)


def _cmul(a, b):
    return a[0] * b[0] - a[1] * b[1], a[0] * b[1] + a[1] * b[0]


def _cexp(re, im):
    e = jnp.exp(re)
    return e * jnp.cos(im), e * jnp.sin(im)


def _s5_params(lam_re, lam_im, log_step, b_re, b_im, c_re, c_im):
    hp = dict(precision=HIGHEST)
    J, nq, ug = S5_J, S5_G // S5_UG, S5_UG
    sw = 2 * ug * S5_P

    def block_diag(x2d, row_group, col_inner):
        rows = x2d.shape[-2]
        ci = np.arange(sw)
        src = (ci // (ug * col_inner)) * col_inner + ci % col_inner
        expand = jnp.asarray(np.arange(LANES)[:, None] == src[None, :], F32)
        keep = jnp.asarray(row_group(np.arange(rows))[:, None] == ((ci // col_inner) % ug)[None, :])
        return jnp.where(keep, jnp.matmul(x2d, expand, precision=HIGHEST), 0.0)

    grp16 = lambda r: (r // S5_CH) % ug
    grp64 = lambda r: (r // S5_P) % ug
    step = jnp.exp(log_step)[..., None]
    lam_bar = _cexp(lam_re * step, lam_im * step)
    den = lam_re * lam_re + lam_im * lam_im
    num = (lam_bar[0] - 1.0, lam_bar[1])
    coef = ((num[0] * lam_re + num[1] * lam_im) / den, (num[1] * lam_re - num[0] * lam_im) / den)
    b_bar = _cmul((coef[0][..., None], coef[1][..., None]), (b_re, b_im))
    dd = jnp.arange(J + 1, dtype=F32)[:, None, None, None]
    lam_pow = _cexp(dd * (lam_re * step)[None], dd * (lam_im * step)[None])

    def units(x, g_axis):
        return x.reshape(x.shape[:g_axis] + (nq, ug) + x.shape[g_axis + 1:])

    sel = [jnp.stack([c[:J][::-1, 0], c[:J][:, 1]], axis=0) for c in lam_pow]
    wb = _cmul((sel[0][..., None], sel[1][..., None]), (b_bar[0][:, None], b_bar[1][:, None]))
    wb = units(jnp.stack(wb, axis=0), 3)
    bcat = block_diag(jnp.transpose(wb, (3, 1, 2, 4, 6, 0, 5)).reshape(nq, 2, J * LANES, LANES),
                      grp16, S5_P)
    lp = [jnp.transpose(c[:J], (1, 2, 0, 3))[:, :, :, None, :] for c in lam_pow]
    cl = _cmul((c_re[:, :, None], c_im[:, :, None]), lp)
    kd = (jnp.einsum('dgjcp,dgpk->dgjck', cl[0], b_bar[0], **hp)
          - jnp.einsum('dgjcp,dgpk->dgjck', cl[1], b_bar[1], **hp))
    oi = np.arange(J)[:, None]
    oo = np.arange(J)[None, :]
    lag_hot = np.stack([(oo - oi)[..., None] == np.arange(J), (oi - oo)[..., None] == np.arange(J)])
    t5 = jnp.einsum('dioj,dgjce->dgioce', jnp.asarray(lag_hot, F32), kd, **hp)
    t5 = units(t5[0] + t5[1], 0)
    tsum = block_diag(jnp.transpose(t5, (0, 2, 1, 5, 3, 4)).reshape(nq, J * LANES, LANES),
                      grp16, S5_CH)
    lq = [jnp.transpose(c[1:], (1, 2, 0, 3))[:, :, :, None, :] for c in lam_pow]
    cm = _cmul((c_re[:, :, None], c_im[:, :, None]), lq)
    cc = jnp.stack([cm[0], -cm[1]], axis=0)
    cc = jnp.stack([cc[:, 0], cc[:, 1, :, ::-1]], axis=1)
    ccat = block_diag(jnp.transpose(units(cc, 2), (2, 1, 0, 3, 6, 4, 5)).reshape(nq, 2 * sw, LANES),
                      grp64, S5_CH)
    b1 = jnp.transpose(units(jnp.stack(b_bar, axis=0), 2), (2, 1, 3, 5, 0, 4))
    b1 = block_diag(b1.reshape(nq, 2, LANES, LANES), grp16, S5_P)

    def lanes(c):
        x = jnp.concatenate([c[0].reshape(2, nq, ug * S5_P), c[1].reshape(2, nq, ug * S5_P)], axis=-1)
        return jnp.transpose(x, (1, 0, 2))[:, :, None, :]

    return dict(bcat=bcat.astype(BF16), tsum=tsum.astype(BF16), ccat=ccat.astype(BF16), b1=b1,
                lj=lanes((lam_pow[0][J], lam_pow[1][J])), l1=lanes(lam_bar))


def _s5_kernel(u_ref, bcat_ref, tsum_ref, ccat_ref, lj_ref, l1_ref, b1_ref, x0_ref,
               y_ref, fin_ref, z_scr, *, nseq, n):
    J = S5_J
    nsub = n // J
    ns = nseq * nsub
    hw = S5_UG * S5_P
    if nseq == 1:
        ucat = jnp.concatenate([u_ref[pl.ds(o, ns, stride=J), :] for o in range(J)], axis=1)
    else:
        ucat = jnp.concatenate(
            [jnp.concatenate([u_ref[pl.ds(j * J + o, nseq, stride=n), :] for o in range(J)], axis=1)
             for j in range(nsub)], axis=0)
    ucat = ucat.astype(BF16)
    for d in range(2):
        z_scr[d] = _dot(ucat, bcat_ref[d])

    def cstep(d, x, z):
        lre, lim = lj_ref[d, :, 0:hw], lj_ref[d, :, hw:]
        return jnp.concatenate([lre * x[:, :hw] - lim * x[:, hw:] + z[:, :hw],
                                lre * x[:, hw:] + lim * x[:, :hw] + z[:, hw:]], axis=1)

    for d in range(2):
        tok = 0 if d == 0 else n - 1
        uf = u_ref[pl.ds(tok, nseq, stride=n), :] if nseq > 1 else u_ref[tok:tok + 1, :]
        bu = jnp.dot(uf, b1_ref[d], precision=HIGHEST, preferred_element_type=F32)
        x0 = x0_ref[d]
        lre, lim = l1_ref[d, :, 0:hw], l1_ref[d, :, hw:]
        fin_ref[d] = jnp.concatenate([lre * x0[:, :hw] - lim * x0[:, hw:] + bu[:, :hw],
                                      lre * x0[:, hw:] + lim * x0[:, :hw] + bu[:, hw:]], axis=1)

    if nseq == 1:
        def step(j, carry):
            out = []
            for d in range(2):
                row = pl.ds(j if d == 0 else nsub - 1 - j, 1)
                z = z_scr[d, row, :]
                z_scr[d, row, :] = carry[d]
                out.append(cstep(d, carry[d], z))
            return tuple(out)

        lax.fori_loop(0, nsub, step, (x0_ref[0], x0_ref[1]), unroll=4)
    else:
        xs = [x0_ref[0], x0_ref[1]]
        for j in range(nsub):
            for d in range(2):
                rows = pl.ds((j if d == 0 else nsub - 1 - j) * nseq, nseq)
                z = z_scr[d, rows, :]
                z_scr[d, rows, :] = xs[d]
                xs[d] = cstep(d, xs[d], z)

    xin = jnp.concatenate([z_scr[0], z_scr[1]], axis=1).astype(BF16)
    ycat = _dot(ucat, tsum_ref[...]) + _dot(xin, ccat_ref[...])
    for o in range(J):
        if nseq == 1:
            y_ref[pl.ds(o, ns, stride=J), :] = ycat[:, o * LANES:(o + 1) * LANES]
        else:
            for j in range(nsub):
                y_ref[pl.ds(j * J + o, nseq, stride=n), :] = ycat[j * nseq:(j + 1) * nseq,
                                                                  o * LANES:(o + 1) * LANES]


def _s5_scan(proj, b, n, prm, l, x0, nseq):
    nq = S5_G // S5_UG
    sw = 2 * S5_UG * S5_P
    jl = S5_J * LANES
    r = nseq * n
    ns = r // S5_J
    if nseq == 1:
        x0 = x0.reshape(nq, 2, b, 1, sw)
        st_spec = pl.BlockSpec((None, 2, None, 1, sw), lambda q, i: (q, 0, i, 0, 0))
    else:
        st_spec = pl.BlockSpec((None, 2, nseq, sw), lambda q, i: (q, 0, i, 0))
    wspec = lambda *shp: pl.BlockSpec((None, None) + shp, lambda q, i: (l, q) + (0,) * len(shp))
    y, fin = pl.pallas_call(
        functools.partial(_s5_kernel, nseq=nseq, n=n),
        grid=(nq, b // nseq),
        in_specs=[pl.BlockSpec((r, LANES), lambda q, i: (i, C_U // LANES + q)),
                  wspec(2, jl, sw), wspec(jl, jl), wspec(2 * sw, jl), wspec(2, 1, sw),
                  wspec(2, 1, sw), wspec(2, LANES, sw), st_spec],
        out_specs=[pl.BlockSpec((r, LANES), lambda q, i: (i, q)), st_spec],
        out_shape=[jax.ShapeDtypeStruct((b * n, DB), F32), jax.ShapeDtypeStruct(x0.shape, F32)],
        scratch_shapes=[pltpu.VMEM((2, ns, sw), F32)],
        compiler_params=_cparams(2),
    )(proj, prm['bcat'], prm['tsum'], prm['ccat'], prm['lj'], prm['l1'], prm['b1'], x0)
    return y, fin.reshape(nq, 2, b, sw)


def _s5_post(u, y_scan, d_skip, w_glu):
    y = jax.nn.gelu(d_skip * u + y_scan)
    return y * _sigmoid(_dot(y.astype(BF16), w_glu))


def _s5_state_to_lanes(s_re, s_im):
    b = s_re.shape[0]
    nq = S5_G // S5_UG
    x = jnp.concatenate([s_re.reshape(b, 2, nq, S5_UG * S5_P), s_im.reshape(b, 2, nq, S5_UG * S5_P)],
                        axis=-1)
    return jnp.transpose(x, (2, 1, 0, 3))


def _s5_lanes_to_state(x):
    nq, _, b, _ = x.shape
    hw = S5_UG * S5_P
    x = jnp.transpose(x, (2, 1, 0, 3))
    return x[..., :hw].reshape(b, 2, S5_G, S5_P), x[..., hw:].reshape(b, 2, S5_G, S5_P)


def _s5_mixer(proj, b, n, sp, l, x0):
    nseq = 8 if (n // S5_J <= 64 and b % 8 == 0) else 1
    return _s5_scan(proj, b, n, sp, l, x0, nseq)


def _conv_kernel(x_ref, prev_ref, next_ref, w_ref, o_ref, *, tiles_per_seq):
    i = pl.program_id(0)
    x = x_ref[...]
    tm = x.shape[0]
    row = lax.broadcasted_iota(jnp.int32, (tm, 1), 0)
    first = (i % tiles_per_seq) == 0
    last = (i % tiles_per_seq) == tiles_per_seq - 1
    prev_row = jnp.where(first, 0.0, prev_ref[7:8, :])
    next_row = jnp.where(last, 0.0, next_ref[0:1, :])
    x_dn = jnp.where(row == 0, prev_row, pltpu.roll(x, 1, axis=0))
    x_up = jnp.where(row == tm - 1, next_row, pltpu.roll(x, tm - 1, axis=0))
    o_ref[...] = x_dn * w_ref[0:1, :] + x * w_ref[1:2, :] + x_up * w_ref[2:3, :]


def _rwkv_conv(proj, conv_w, n, tm):
    t = proj.shape[0]
    w3 = 3 * DB
    cb = C_RKV // w3
    nb8 = t // 8
    return pl.pallas_call(
        functools.partial(_conv_kernel, tiles_per_seq=n // tm),
        grid=(t // tm,),
        in_specs=[pl.BlockSpec((tm, w3), lambda i: (i, cb)),
                  pl.BlockSpec((8, w3), lambda i: (jnp.maximum(i * (tm // 8) - 1, 0), cb)),
                  pl.BlockSpec((8, w3), lambda i: (jnp.minimum((i + 1) * (tm // 8), nb8 - 1), cb)),
                  pl.BlockSpec((3, w3), lambda i: (0, 0))],
        out_specs=pl.BlockSpec((tm, w3), lambda i: (i, 0)),
        out_shape=jax.ShapeDtypeStruct((t, w3), F32),
        compiler_params=_cparams(1),
    )(proj, proj, proj, conv_w)


def _rwkv_chunk_kernel(rc_ref, kc_ref, vc_ref, lora_ref, w0_ref, wup_ref, a0_ref, aup_ref,
                       kk_ref, ka_ref, s0_ref, y_ref, sfin_ref, z_scr, *, nc, nseq):
    d = pl.program_id(1)
    ci = pl.program_id(2)
    L = RW_L
    rows = nseq * L

    @pl.when(ci == 0)
    def _():
        z_scr[...] = s0_ref[...]

    flat = lambda ref: ref[...].reshape(rows, ref.shape[-1])
    rc, kc, vc, lora = flat(rc_ref), flat(kc_ref), flat(vc_ref), flat(lora_ref)
    wd = lora[:, 0:64]
    ad = lora[:, 64:128]
    lora_w = _mm3(jnp.tanh(wd), wup_ref[...])
    log_w = -_softplus(-(w0_ref[...] + lora_w)) - 0.5
    lw = -jnp.exp(log_w)
    a = _sigmoid(a0_ref[...] + _mm3(ad, aup_ref[...]))
    kd = kc * (1.0 + (a - 1.0) * ka_ref[...])
    kk = kc * kk_ref[...]
    kk = kk * lax.rsqrt(_dot_rhs_exact(kk * kk, _head_block_matrix(DB, 1.0)) + 1e-12)
    alpha = -kk
    beta = kk * a

    sgn = 1 - 2 * d
    tt = lax.broadcasted_iota(jnp.int32, (rows, rows), 0)
    ss = lax.broadcasted_iota(jnp.int32, (rows, rows), 1)
    tri = jnp.where(((tt % L - ss % L) * sgn >= 0) & (tt // L == ss // L), 1.0, 0.0).astype(BF16)
    c = _dot_lhs_exact(tri, lw)
    c_ex = c - lw
    ctot = jnp.concatenate(
        [jnp.broadcast_to(jnp.sum(lw[s * L:(s + 1) * L], axis=0, keepdims=True), (L, DB))
         for s in range(nseq)], axis=0)
    mid = 0.5 * ctot
    e_in = jnp.exp(c - mid)
    e_ex = jnp.exp(c_ex - mid)
    e_out = jnp.exp(mid - c)
    e_mid = jnp.exp(mid)
    al_t = alpha * e_ex
    r_t = rc * e_in
    be_t = beta * e_out
    k_t = kd * e_out
    a0s = al_t * e_mid
    r0s = r_t * e_mid
    bps = be_t * e_mid
    kps = k_t * e_mid
    p_l = e_mid * e_mid

    m0 = _pair_masks()
    t2 = lax.broadcasted_iota(jnp.int32, (2 * L, 2 * L), 0) % L
    s2 = lax.broadcasted_iota(jnp.int32, (2 * L, 2 * L), 1) % L
    strict = (t2 - s2) * sgn > 0
    incl = (t2 - s2) * sgn >= 0
    eye = (lax.broadcasted_iota(jnp.int32, (LANES, LANES), 0)
           == lax.broadcasted_iota(jnp.int32, (LANES, LANES), 1))
    eye_f = jnp.where(eye, 1.0, 0.0)
    zeros = jnp.zeros((LANES, LANES), F32)

    chains = [(s, p) for s in range(nseq) for p in range(HEADS // 2)]
    pairs = range(len(chains))
    rws = [slice(s * L, (s + 1) * L) for s, _ in chains]
    sls = [slice(p * LANES, (p + 1) * LANES) for _, p in chains]

    def stack(x, p):
        xp = x[rws[p], sls[p]]
        return jnp.concatenate([jnp.where(m0, xp, 0.0), jnp.where(m0, 0.0, xp)], axis=0)

    raws = [_dot_nt(jnp.concatenate([stack(al_t, p), stack(r_t, p)], axis=0).astype(BF16),
                    jnp.concatenate([stack(be_t, p), stack(k_t, p)], axis=0).astype(BF16))
            for p in pairs]
    amat = [jnp.where(strict, r[:LANES, :LANES], 0.0) for r in raws]
    bmat = [jnp.where(strict, r[:LANES, LANES:], 0.0) for r in raws]
    qbk = [jnp.concatenate([jnp.where(incl, r[LANES:, :LANES], 0.0),
                            jnp.where(incl, r[LANES:, LANES:], 0.0)], axis=1) for r in raws]

    smat = [eye_f + a for a in amat]
    pw = [_mm_inv(a, a) for a in amat]
    for _ in range(4):
        xs = [_mm_inv(pw[p], jnp.concatenate([pw[p], smat[p]], axis=1)) for p in pairs]
        smat = [smat[p] + xs[p][:, LANES:] for p in pairs]
        pw = [x[:, :LANES] for x in xs]
    tmat = [smat[p] + _mm_inv(pw[p], smat[p]) for p in pairs]

    vs = [stack(vc, p) for p in pairs]
    bv = [_mm(bmat[p], vs[p]) for p in pairs]
    wu = [_mm_inv(tmat[p], jnp.concatenate([stack(a0s, p), bv[p]], axis=1)) for p in pairs]
    rhs2 = [jnp.concatenate([wu[p], jnp.concatenate([zeros, vs[p]], axis=1)], axis=0).astype(BF16)
            for p in pairs]
    out_a = [_dot(qbk[p].astype(BF16), rhs2[p]) for p in pairs]
    out_b = [_dot_tn(jnp.concatenate([stack(bps, p), stack(kps, p)], axis=0).astype(BF16), rhs2[p])
             for p in pairs]
    hz = [_mm_state(jnp.concatenate(
        [stack(r0s, p) + out_a[p][:, :LANES],
         jnp.where(eye, p_l[rws[p], sls[p]][0:1], 0.0) + out_b[p][:, :LANES]], axis=0), z_scr[chains[p]])
          for p in pairs]
    for p in pairs:
        y = out_a[p][:, LANES:] + hz[p][:LANES]
        z_scr[chains[p]] = hz[p][LANES:] + out_b[p][:, LANES:]
        y_ref[chains[p][0], :, sls[p]] = y[:L] + y[L:]

    @pl.when(ci == nc - 1)
    def _():
        sfin_ref[...] = z_scr[...]


def _rwkv_post(y, rkv_ref, gd, g_up, u_bonus, ln_w, ln_b):
    avg = _head_block_matrix(DB, 1.0 / DH)
    mu = _dot_rhs_exact(y, avg)
    yc = y - mu
    var = _dot_rhs_exact(yc * yc, avg)
    yn = yc * lax.rsqrt(var + GN_EPS) * ln_w + ln_b
    rc, kc, vc = rkv_ref[:, 0:DB], rkv_ref[:, DB:2 * DB], rkv_ref[:, 2 * DB:3 * DB]
    bonus = _dot_rhs_exact(rc * kc * u_bonus, _head_block_matrix(DB, 1.0)) * vc
    g = _dot(_sigmoid(gd).astype(BF16), g_up)
    return (yn + bonus) * g


def _rwkv_mixer(proj, b, n, conv_w, prm, s0_pairs, tm):
    t = b * n
    nc = n // RW_L
    rkv = _rwkv_conv(proj, conv_w, n, min(tm, n))

    nseq = 4 if b % 4 == 0 else (2 if b % 2 == 0 else 1)
    hp = HEADS // 2

    def chunk(dd, ci):
        return jnp.where(dd == 0, ci, nc - 1 - ci)

    tok = lambda w, col: pl.BlockSpec((nseq, RW_L, w), lambda i, dd, ci: (i, chunk(dd, ci), col))
    dirp = lambda r: pl.BlockSpec((None, r, DB), lambda i, dd, ci: (dd, 0, 0))
    shared = pl.BlockSpec((1, DB), lambda i, dd, ci: (0, 0))
    state = pl.BlockSpec((nseq, None, hp, LANES, LANES), lambda i, dd, ci: (i, dd, 0, 0, 0))
    rkv3 = rkv.reshape(b, n, 3 * DB)
    y2, sfin = pl.pallas_call(
        functools.partial(_rwkv_chunk_kernel, nc=nc, nseq=nseq),
        grid=(b // nseq, 2, nc),
        in_specs=[tok(DB, 0), tok(DB, 1), tok(DB, 2), tok(256, C_LORA // 256),
                  dirp(1), dirp(64), dirp(1), dirp(64), shared, shared, state],
        out_specs=[pl.BlockSpec((None, nseq, RW_L, DB), lambda i, dd, ci: (dd, i, chunk(dd, ci), 0)),
                   state],
        out_shape=[jax.ShapeDtypeStruct((2, b, n, DB), F32),
                   jax.ShapeDtypeStruct((b, 2, hp, LANES, LANES), F32)],
        scratch_shapes=[pltpu.VMEM((nseq, hp, LANES, LANES), F32)],
        compiler_params=_cparams(3),
    )(rkv3, rkv3, rkv3, proj.reshape(b, n, W_MAIN), prm['w0'], prm['w_up'], prm['a0'], prm['a_up'],
      prm['k_k'], prm['k_a'], s0_pairs)
    return y2.reshape(2, t, DB), rkv, sfin


def _state_to_pairs(s):
    b = s.shape[0]
    st = jnp.swapaxes(s, -1, -2).reshape(b, 2, HEADS // 2, 2, DH, DH)
    z = jnp.zeros((b, 2, HEADS // 2, 2, DH, 2, DH), F32)
    z = z.at[:, :, :, 0, :, 0, :].set(st[:, :, :, 0])
    z = z.at[:, :, :, 1, :, 1, :].set(st[:, :, :, 1])
    return z.reshape(b, 2, HEADS // 2, LANES, LANES)


def _pairs_to_state(z):
    b = z.shape[0]
    z = z.reshape(b, 2, HEADS // 2, 2, DH, 2, DH)
    st = jnp.stack([z[:, :, :, 0, :, 0, :], z[:, :, :, 1, :, 1, :]], axis=3)
    return jnp.swapaxes(st.reshape(b, 2, HEADS, DH, DH), -1, -2)


def _merge_kernel(u_ref, ys5_ref, d_ref, wglu_ref,
                  yf_ref, yb_ref, rkv_ref, gd_ref, gup_ref, ub_ref, lnw_ref, lnb_ref,
                  oatt_ref, gs5_ref, grw_ref, gatt_ref, x_ref, ws5_ref, wrw_ref, watt_ref, wmix_ref,
                  gpost_ref, g1_ref, o_ref):
    o_s5 = _s5_post(u_ref[...], ys5_ref[...], d_ref[...], wglu_ref[...])
    o_rw = _rwkv_post(yf_ref[...] + yb_ref[...], rkv_ref, gd_ref[:, 128:256], gup_ref[...],
                      ub_ref[...], lnw_ref[...], lnb_ref[...])
    gate = lambda ref: _sigmoid(ref[...].astype(F32))
    merged = (gate(gs5_ref) * _dot(o_s5.astype(BF16), ws5_ref[...])
              + gate(grw_ref) * _dot(o_rw.astype(BF16), wrw_ref[...])
              + gate(gatt_ref) * _dot(oatt_ref[...].astype(BF16), watt_ref[...]))
    m = _dot(merged.astype(BF16), wmix_ref[...])
    o_ref[...] = x_ref[...] + g1_ref[...] * _rms(m, gpost_ref[...])


def _merge(main, y_s5, y_rw, rkv, o_att, gates, x, s5_d, w_glu_b, rwp, wts, mod_l, row_of_tile,
           g_post, tm):
    t = x.shape[0]
    tok = lambda w, c: pl.BlockSpec((tm, w), lambda i: (i, c))
    full = lambda r, c: pl.BlockSpec((r, c), lambda i: (0, 0))
    return pl.pallas_call(
        _merge_kernel,
        grid=(t // tm,),
        in_specs=[tok(DB, C_U // DB), tok(DB, 0), full(1, DB), full(DB, DB),
                  pl.BlockSpec((None, tm, DB), lambda i: (0, i, 0)),
                  pl.BlockSpec((None, tm, DB), lambda i: (1, i, 0)),
                  tok(3 * DB, 0), tok(256, C_LORA // 256), full(128, DB), full(1, DB), full(1, DB),
                  full(1, DB),
                  tok(DB, 0), tok(D, 0), tok(D, 1), tok(D, 2), tok(D, 0),
                  full(DB, D), full(DB, D), full(DB, D), full(D, D), full(1, D),
                  _mod_spec(2, row_of_tile, 1)],
        out_specs=tok(D, 0),
        out_shape=jax.ShapeDtypeStruct((t, D), F32),
        compiler_params=_cparams(1),
    )(main, y_s5, s5_d, w_glu_b, y_rw, y_rw, rkv, main, rwp['g_up'], rwp['u'], rwp['ln_w'],
      rwp['ln_b'], o_att, gates, gates, gates, x, wts['br_s5'], wts['br_rw'], wts['br_att'],
      wts['mix'], g_post, mod_l)


def _ffn_dense_kernel(x_ref, gpre_ref, sc_ref, sh_ref, w1_ref, w3_ref, w2_ref, gpost_ref, g2_ref,
                      o_ref, h_scr, acc_scr):
    f = pl.program_id(1)

    @pl.when(f == 0)
    def _():
        h = _rms(x_ref[...], gpre_ref[...]) * (1.0 + sc_ref[...]) + sh_ref[...]
        h_scr[...] = h.astype(BF16)
        acc_scr[...] = jnp.zeros_like(acc_scr)

    h = h_scr[...]
    hid = _silu(_dot(h, w1_ref[...])) * _dot(h, w3_ref[...])
    acc_scr[...] += _dot(hid.astype(BF16), w2_ref[...])

    @pl.when(f == pl.num_programs(1) - 1)
    def _():
        o_ref[...] = x_ref[...] + g2_ref[...] * _rms(acc_scr[...], gpost_ref[...])


def _ffn_dense(x, mod_l, row_of_tile, g_pre, g_post, w1, w3, w2, tm, tf):
    t = x.shape[0]
    return pl.pallas_call(
        _ffn_dense_kernel,
        grid=(t // tm, FF_DENSE // tf),
        in_specs=[pl.BlockSpec((tm, D), lambda i, f: (i, 0)),
                  pl.BlockSpec((1, D), lambda i, f: (0, 0)),
                  _mod_spec(4, row_of_tile, 2), _mod_spec(3, row_of_tile, 2),
                  pl.BlockSpec((D, tf), lambda i, f: (0, f)),
                  pl.BlockSpec((D, tf), lambda i, f: (0, f)),
                  pl.BlockSpec((tf, D), lambda i, f: (f, 0)),
                  pl.BlockSpec((1, D), lambda i, f: (0, 0)),
                  _mod_spec(5, row_of_tile, 2)],
        out_specs=pl.BlockSpec((tm, D), lambda i, f: (i, 0)),
        out_shape=jax.ShapeDtypeStruct((t, D), F32),
        scratch_shapes=[pltpu.VMEM((tm, D), BF16), pltpu.VMEM((tm, D), F32)],
        compiler_params=_cparams(2),
    )(x, g_pre, mod_l, mod_l, w1, w3, w2, g_post, mod_l)


def _ffn_moe_kernel(x_ref, gpre_ref, sc_ref, sh_ref, rw_ref, rb_ref, w1_ref, w3_ref, w2_ref,
                    gpost_ref, g2_ref, o_ref, h_scr, comb_scr, acc_scr):
    e = pl.program_id(1)
    lane = lax.broadcasted_iota(jnp.int32, (1, LANES), 1)

    @pl.when(e == 0)
    def _():
        h = _rms(x_ref[...], gpre_ref[...]) * (1.0 + sc_ref[...]) + sh_ref[...]
        h_scr[...] = h.astype(BF16)
        acc_scr[...] = jnp.zeros_like(acc_scr)
        logits = jnp.dot(h, rw_ref[...], precision=HIGHEST, preferred_element_type=F32) + rb_ref[...]
        ex = jnp.exp(logits - jnp.max(logits, axis=-1, keepdims=True))
        probs = ex / jnp.sum(ex, axis=-1, keepdims=True)
        p1 = jnp.max(probs, axis=-1, keepdims=True)
        i1 = jnp.min(jnp.where(probs == p1, lane, LANES), axis=-1, keepdims=True)
        rest = jnp.where(lane == i1, -1.0, probs)
        p2 = jnp.max(rest, axis=-1, keepdims=True)
        i2 = jnp.min(jnp.where(rest == p2, lane, LANES), axis=-1, keepdims=True)
        den = p1 + p2
        comb_scr[...] = jnp.where(lane == i1, p1 / den, 0.0) + jnp.where(lane == i2, p2 / den, 0.0)

    h = h_scr[...]
    cw = jnp.sum(jnp.where(lane == e, comb_scr[...], 0.0), axis=-1, keepdims=True)
    hid = _silu(_dot(h, w1_ref[...])) * _dot(h, w3_ref[...])
    acc_scr[...] += cw * _dot(hid.astype(BF16), w2_ref[...])

    @pl.when(e == N_EXP - 1)
    def _():
        o_ref[...] = x_ref[...] + g2_ref[...] * _rms(acc_scr[...], gpost_ref[...])


def _ffn_moe(x, mod_l, row_of_tile, g_pre, g_post, rw, rb, w1, w3, w2, tm):
    t = x.shape[0]
    return pl.pallas_call(
        _ffn_moe_kernel,
        grid=(t // tm, N_EXP),
        in_specs=[pl.BlockSpec((tm, D), lambda i, e: (i, 0)),
                  pl.BlockSpec((1, D), lambda i, e: (0, 0)),
                  _mod_spec(4, row_of_tile, 2), _mod_spec(3, row_of_tile, 2),
                  pl.BlockSpec((D, LANES), lambda i, e: (0, 0)),
                  pl.BlockSpec((1, LANES), lambda i, e: (0, 0)),
                  pl.BlockSpec((None, D, FF_EXP), lambda i, e: (e, 0, 0)),
                  pl.BlockSpec((None, D, FF_EXP), lambda i, e: (e, 0, 0)),
                  pl.BlockSpec((None, FF_EXP, D), lambda i, e: (e, 0, 0)),
                  pl.BlockSpec((1, D), lambda i, e: (0, 0)),
                  _mod_spec(5, row_of_tile, 2)],
        out_specs=pl.BlockSpec((tm, D), lambda i, e: (i, 0)),
        out_shape=jax.ShapeDtypeStruct((t, D), F32),
        scratch_shapes=[pltpu.VMEM((tm, D), BF16), pltpu.VMEM((tm, LANES), F32),
                        pltpu.VMEM((tm, D), F32)],
        compiler_params=_cparams(2),
    )(x, g_pre, mod_l, mod_l, rw, rb, w1, w3, w2, g_post, mod_l)


def _reorder_w_in(w):
    qkv, u, rkv, lora, gates = (w[:, 0:1536], w[:, 1536:2048], w[:, 2048:3584], w[:, 3584:3840],
                                w[:, 3840:6912])
    return jnp.concatenate([gates, rkv, u, lora, qkv], axis=1)


def _layer(x, b, n, l, mod_l, row_of_tile, P, cache, tm):
    row = lambda a: a.reshape(1, -1)
    tm_proj = min(2 * tm, x.shape[0], n) if cache is not None else min(2 * tm, x.shape[0])
    row_of_ptile = (lambda i: row_of_tile(i * (tm_proj // tm)))
    outs = _projection(x, mod_l, row_of_ptile, row(P['norm_pre_mix'][l]), P['w_in_b'][l], tm_proj,
                       cache is None)
    gates, proj, qkv = outs[:3]
    if cache is None:
        o_att = _ctx_attention(qkv, b, n)
        x0 = jnp.zeros((S5_G // S5_UG, 2, b, 2 * S5_UG * S5_P), F32)
        s0 = jnp.zeros((b, 2, HEADS // 2, LANES, LANES), F32)
    else:
        k_c, v_c, s5re, s5im, rw0 = cache
        o_att = _na_attention(qkv, k_c.reshape(-1, DB).astype(BF16), v_c.reshape(-1, DB).astype(BF16),
                              P['na_bias'], l, b, n)
        x0 = _s5_state_to_lanes(s5re, s5im)
        s0 = _state_to_pairs(rw0)
    y_s5, s5fin = _s5_mixer(proj, b, n, P['s5'], l, x0)
    y_rw, rkv, sfin = _rwkv_mixer(proj, b, n, P['rwkv_conv'][l], P['rwkv'][l], s0, tm)
    x = _merge(proj, y_s5, y_rw, rkv, o_att, gates, x, row(P['s5_d'][l]), P['s5_w_glu_b'][l],
               P['rwkv'][l], P['merge'][l], mod_l, row_of_tile, row(P['norm_post_mix'][l]), tm)
    i = l // 2
    if l % 2 == 0:
        x = _ffn_dense(x, mod_l, row_of_tile, row(P['norm_pre_ffn'][l]), row(P['norm_post_ffn'][l]),
                       P['dense_w1_b'][i], P['dense_w3_b'][i], P['dense_w2_b'][i], tm, FF_DENSE // 2)
    else:
        x = _ffn_moe(x, mod_l, row_of_ptile, row(P['norm_pre_ffn'][l]), row(P['norm_post_ffn'][l]),
                     P['moe_rw'][i], P['moe_rb'][i], P['moe_w1_b'][i], P['moe_w3_b'][i],
                     P['moe_w2_b'][i], tm_proj)
    if cache is None:
        k_new = outs[3][:, DB:2 * DB]
        v_new = outs[3][:, 2 * DB:3 * DB]
        fre, fim = _s5_lanes_to_state(s5fin)
        return x, (k_new, v_new, fre, fim, sfin)
    return x, None


def kernel(x_prompt, x_sample, cache_k, cache_v, state_s5_re, state_s5_im, state_rwkv, c, c_ctx,
           w_ada, b_ada, norm_pre_mix, norm_post_mix, norm_pre_ffn, norm_post_ffn, w_in,
           s5_lam_re, s5_lam_im, s5_log_step, s5_b_re, s5_b_im, s5_c_re, s5_c_im, s5_d, s5_w_glu,
           rwkv_conv, rwkv_w0, rwkv_w_up, rwkv_a0, rwkv_a_up, rwkv_g_up, rwkv_k_k, rwkv_k_a,
           rwkv_u, rwkv_ln_w, rwkv_ln_b, att_rpb, w_br_s5, w_br_rwkv, w_br_att, w_mix_out,
           dense_w1, dense_w3, dense_w2, moe_router_w, moe_router_b, moe_w1, moe_w3, moe_w2):
    bc, nc_, _ = x_prompt.shape
    bl, nl, _ = x_sample.shape
    tm_c = min(512, bc * nc_)
    tm_l = min(512, nl)

    P = dict(norm_pre_mix=norm_pre_mix, norm_post_mix=norm_post_mix, norm_pre_ffn=norm_pre_ffn,
             norm_post_ffn=norm_post_ffn, s5_d=s5_d, rwkv_conv=rwkv_conv, att_rpb=att_rpb)
    P['w_in_b'] = [_reorder_w_in(w_in[l]).astype(BF16) for l in range(DEPTH)]
    P['s5_w_glu_b'] = s5_w_glu.astype(BF16)
    P['dense_w1_b'], P['dense_w3_b'], P['dense_w2_b'] = (dense_w1.astype(BF16), dense_w3.astype(BF16),
                                                         dense_w2.astype(BF16))
    P['moe_w1_b'], P['moe_w3_b'], P['moe_w2_b'] = (moe_w1.astype(BF16), moe_w3.astype(BF16),
                                                   moe_w2.astype(BF16))
    n_moe = moe_router_w.shape[0]
    P['moe_rw'] = jnp.pad(moe_router_w, ((0, 0), (0, 0), (0, LANES - N_EXP)))
    P['moe_rb'] = jnp.pad(moe_router_b, ((0, 0), (0, LANES - N_EXP)),
                          constant_values=NEG).reshape(n_moe, 1, LANES)
    P['merge'] = [dict(br_s5=w_br_s5[l].astype(BF16), br_rw=w_br_rwkv[l].astype(BF16),
                       br_att=w_br_att[l].astype(BF16), mix=w_mix_out[l].astype(BF16))
                  for l in range(DEPTH)]
    P['s5'] = jax.vmap(_s5_params)(s5_lam_re, s5_lam_im, s5_log_step, s5_b_re, s5_b_im, s5_c_re,
                                   s5_c_im)
    P['na_bias'] = jax.vmap(functools.partial(_na_bias_tables, rows=nl // GRID_W))(att_rpb)
    P['rwkv'] = []
    for l in range(DEPTH):
        P['rwkv'].append(dict(
            w0=rwkv_w0[l].reshape(2, 1, DB), w_up=rwkv_w_up[l], a0=rwkv_a0[l].reshape(2, 1, DB),
            a_up=rwkv_a_up[l], k_k=rwkv_k_k[l].reshape(1, DB), k_a=rwkv_k_a[l].reshape(1, DB),
            g_up=rwkv_g_up[l].astype(BF16), u=rwkv_u[l].reshape(1, DB),
            ln_w=rwkv_ln_w[l].reshape(1, DB), ln_b=rwkv_ln_b[l].reshape(1, DB)))

    cvec = jnp.zeros((8, D), F32).at[0].set(c_ctx).at[1:1 + bl].set(c)
    mod = _modulation(cvec, w_ada, b_ada)

    xp = x_prompt.reshape(bc * nc_, D)
    ks, vs, s5re, s5im, rws = [], [], [], [], []
    for l in range(DEPTH):
        xp, (k_n, v_n, fre, fim, sfin) = _layer(xp, bc, nc_, l, mod[l], lambda i: 0, P, None, tm_c)
        ks.append(k_n.reshape(bc, nc_, HEADS, DH))
        vs.append(v_n.reshape(bc, nc_, HEADS, DH))
        s5re.append(fre)
        s5im.append(fim)
        rws.append(_pairs_to_state(sfin))
    new_k = jnp.stack(ks, axis=1)
    new_v = jnp.stack(vs, axis=1)
    new_s5_re = jnp.stack(s5re, axis=1)
    new_s5_im = jnp.stack(s5im, axis=1)
    new_rwkv = jnp.stack(rws, axis=1)

    xs = x_sample.reshape(bl * nl, D)
    tiles_per_seq = nl // tm_l
    for l in range(DEPTH):
        cache = (cache_k[:, l], cache_v[:, l], state_s5_re[:, l], state_s5_im[:, l], state_rwkv[:, l])
        xs, _ = _layer(xs, bl, nl, l, mod[l], lambda i: 1 + i // tiles_per_seq, P, cache, tm_l)

    return (xp.reshape(bc, nc_, D), xs.reshape(bl, nl, D), new_k, new_v, new_s5_re, new_s5_im,
            new_rwkv)
```

```python
import functools
import math

import numpy as np
import jax
import jax.numpy as jnp
from jax import lax
from jax.experimental import pallas as pl
from jax.experimental.pallas import tpu as pltpu

F32 = jnp.float32
BF16 = jnp.bfloat16
HIGHEST = lax.Precision.HIGHEST

D = 1024
DEPTH = 2
GRID_W = 64
DH = 64
HEADS = 8
DB = 512
WIN_ROWS = 8
WIN_COLS = 16
S5_CH = 16
S5_G = 32
S5_P = 64
S5_J = 8
S5_UG = 8
FF_DENSE = 2816
N_EXP = 8
FF_EXP = 1024
EPS = 1e-6
GN_EPS = 64e-5
NEG = -1e30

D_IN = 6912
TN_PROJ = 768
J_MAIN = 4
J_QKV = 7
J_RKV_TILES = 2
W_MAIN = (J_QKV - J_MAIN) * TN_PROJ
C_RKV = 0
C_U = 1536
C_LORA = 2048

LANES = 128
RW_L = 64
NA_RB = 4
NA_KR = 12
VMEM_LIMIT = 56 * 1024 * 1024


def _cparams(n_axes, vmem=VMEM_LIMIT):
    return pltpu.CompilerParams(dimension_semantics=("arbitrary",) * n_axes,
                                vmem_limit_bytes=vmem)


def _dot(a, b):
    return jnp.dot(a, b, preferred_element_type=F32)


def _dot_nt(a, b):
    return lax.dot_general(a, b, (((1,), (1,)), ((), ())), preferred_element_type=F32)


def _dot_tn(a, b):
    return lax.dot_general(a, b, (((0,), (0,)), ((), ())), preferred_element_type=F32)


def _split2(x):
    hi = x.astype(BF16)
    lo = (x - hi.astype(F32)).astype(BF16)
    return hi, lo


def _split3(x):
    x1 = x.astype(BF16)
    r1 = x - x1.astype(F32)
    x2 = r1.astype(BF16)
    x3 = (r1 - x2.astype(F32)).astype(BF16)
    return x1, x2, x3


def _dot_rhs_exact(x, m):
    x1, x2, x3 = _split3(x)
    return _dot(x1, m) + _dot(x2, m) + _dot(x3, m)


def _dot_lhs_exact(m, x):
    x1, x2, x3 = _split3(x)
    return _dot(m, x1) + _dot(m, x2) + _dot(m, x3)


def _mm3(a, b):
    a1, a2 = _split2(a)
    b1, b2 = _split2(b)
    return _dot(a1, b1) + _dot(a1, b2) + _dot(a2, b1)


def _mm(a, b):
    return _dot(a.astype(BF16), b.astype(BF16))


_mm_inv = _mm
_mm_state = _mm


def _sigmoid(x):
    return 0.5 * (jnp.tanh(0.5 * x) + 1.0)


def _silu(x):
    return x * _sigmoid(x)


def _softplus(x):
    return jnp.maximum(x, 0.0) + jnp.log(1.0 + jnp.exp(-jnp.abs(x)))


def _rms(x, g):
    return x * lax.rsqrt(jnp.mean(x * x, axis=-1, keepdims=True) + EPS) * g


def _head_block_matrix(width, value):
    r = lax.broadcasted_iota(jnp.int32, (width, width), 0) // DH
    c = lax.broadcasted_iota(jnp.int32, (width, width), 1) // DH
    return jnp.where(r == c, value, 0.0).astype(BF16)


def _mod_kernel(c_ref, w_ref, b_ref, o_ref):
    s = _silu(c_ref[...]).astype(BF16)
    o_ref[...] = _dot(s, w_ref[...].astype(BF16)) + b_ref[...]


def _modulation(cvec, w_ada, b_ada):
    tn = 1536
    out = pl.pallas_call(
        _mod_kernel,
        grid=(DEPTH, 6 * D // tn),
        in_specs=[pl.BlockSpec((8, D), lambda l, j: (0, 0)),
                  pl.BlockSpec((None, D, tn), lambda l, j: (l, 0, j)),
                  pl.BlockSpec((None, 1, tn), lambda l, j: (l, 0, j))],
        out_specs=pl.BlockSpec((None, 8, tn), lambda l, j: (l, 0, j)),
        out_shape=jax.ShapeDtypeStruct((DEPTH, 8, 6 * D), F32),
        compiler_params=_cparams(2),
    )(cvec, w_ada, b_ada.reshape(DEPTH, 1, 6 * D))
    return out.reshape(DEPTH, 8, 1, 6 * D)


def _mod_spec(part, row_of_tile, n_grid):
    if n_grid == 1:
        return pl.BlockSpec((None, 1, D), lambda i: (row_of_tile(i), 0, part))
    return pl.BlockSpec((None, 1, D), lambda i, j: (row_of_tile(i), 0, part))


def _proj_kernel(x_ref, xp_ref, xn_ref, g_ref, sc_ref, sh_ref, w_ref, cw_ref, og_ref, om_ref, oq_ref,
                 *rest, f32_qkv, n):
    h_scr, halo_scr = rest[-2], rest[-1]
    i = pl.program_id(0)
    j = pl.program_id(1)
    tm = x_ref.shape[0]

    @pl.when(j == 0)
    def _():
        norm = lambda x: (_rms(x, g_ref[...]) * (1.0 + sc_ref[...]) + sh_ref[...]).astype(BF16)
        h_scr[...] = norm(x_ref[...])
        halo_scr[0:8, :] = norm(xp_ref[...])
        halo_scr[8:16, :] = norm(xn_ref[...])

    acc = _dot(h_scr[...], w_ref[...])

    @pl.when(j < J_MAIN)
    def _():
        og_ref[...] = acc.astype(BF16)

    @pl.when((j >= J_MAIN) & (j < J_MAIN + J_RKV_TILES))
    def _():
        halo = _dot(halo_scr[...], w_ref[...])
        row = lax.broadcasted_iota(jnp.int32, (tm, 1), 0)
        pos = (i * tm + row) % n
        x_dn = jnp.where(pos == 0, 0.0, jnp.where(row == 0, halo[7:8, :], pltpu.roll(acc, 1, axis=0)))
        x_up = jnp.where(pos == n - 1, 0.0,
                         jnp.where(row == tm - 1, halo[8:9, :], pltpu.roll(acc, tm - 1, axis=0)))
        om_ref[...] = x_dn * cw_ref[0:1, :] + acc * cw_ref[1:2, :] + x_up * cw_ref[2:3, :]

    @pl.when((j >= J_MAIN + J_RKV_TILES) & (j < J_QKV))
    def _():
        om_ref[...] = acc

    @pl.when(j >= J_QKV)
    def _():
        oq_ref[...] = acc.astype(BF16)
        if f32_qkv:
            rest[0][...] = acc


def _projection(x, mod_l, row_of_tile, g_pre, w_in_b, conv_w, n, tm, f32_qkv):
    t = x.shape[0]
    n_tiles = D_IN // TN_PROJ
    nb8 = t // 8
    tile = lambda j0, j1: pl.BlockSpec((tm, TN_PROJ), lambda i, j: (i, jnp.clip(j - j0, 0, j1 - j0 - 1)))
    out_specs = [tile(0, J_MAIN), tile(J_MAIN, J_QKV), tile(J_QKV, n_tiles)]
    out_shape = [jax.ShapeDtypeStruct((t, 3 * D), BF16), jax.ShapeDtypeStruct((t, W_MAIN), F32),
                 jax.ShapeDtypeStruct((t, 3 * DB), BF16)]
    if f32_qkv:
        out_specs.append(tile(J_QKV, n_tiles))
        out_shape.append(jax.ShapeDtypeStruct((t, 3 * DB), F32))
    return pl.pallas_call(
        functools.partial(_proj_kernel, f32_qkv=f32_qkv, n=n),
        grid=(t // tm, n_tiles),
        in_specs=[pl.BlockSpec((tm, D), lambda i, j: (i, 0)),
                  pl.BlockSpec((8, D), lambda i, j: (jnp.maximum(i * (tm // 8) - 1, 0), 0)),
                  pl.BlockSpec((8, D), lambda i, j: (jnp.minimum((i + 1) * (tm // 8), nb8 - 1), 0)),
                  pl.BlockSpec((1, D), lambda i, j: (0, 0)),
                  _mod_spec(1, row_of_tile, 2),
                  _mod_spec(0, row_of_tile, 2),
                  pl.BlockSpec((D, TN_PROJ), lambda i, j: (0, j)),
                  pl.BlockSpec((3, TN_PROJ), lambda i, j: (0, jnp.clip(j - J_MAIN, 0, J_RKV_TILES - 1)))],
        out_specs=out_specs,
        out_shape=out_shape,
        scratch_shapes=[pltpu.VMEM((tm, D), BF16), pltpu.VMEM((16, D), BF16)],
        compiler_params=_cparams(2),
    )(x, x, x, g_pre, mod_l, mod_l, w_in_b, conv_w)


def _pair_masks():
    lane = lax.broadcasted_iota(jnp.int32, (1, LANES), 1)
    return lane < DH


def _softmax_pv(parts):
    m = None
    for s, _ in parts:
        mx = jnp.max(s, axis=-1, keepdims=True)
        m = mx if m is None else jnp.maximum(m, mx)
    es = [jnp.exp(s - m) for s, _ in parts]
    den = None
    for e in es:
        sm = jnp.sum(e, axis=-1, keepdims=True)
        den = sm if den is None else den + sm
    out = None
    for e, (_, v) in zip(es, parts):
        o = _dot(e.astype(BF16), v)
        out = o if out is None else out + o
    return out * (1.0 / den)


def _ctx_att_kernel(q_ref, k_ref, v_ref, o_ref):
    scale = DH ** -0.5
    m0 = _pair_masks()
    for p in range(HEADS // 2):
        sl = slice(p * LANES, (p + 1) * LANES)
        qp, kp, vp = q_ref[:, sl] * scale, k_ref[:, sl], v_ref[:, sl]
        o_pair = None
        for hh in range(2):
            msk = m0 if hh == 0 else jnp.logical_not(m0)
            qm = jnp.where(msk, qp, jnp.zeros_like(qp))
            o = _softmax_pv([(_dot_nt(qm, kp), vp)])
            o_pair = o if o_pair is None else jnp.where(m0, o_pair, o)
        o_ref[:, sl] = o_pair


def _ctx_attention(qkv, b, n):
    return pl.pallas_call(
        _ctx_att_kernel,
        grid=(b,),
        in_specs=[pl.BlockSpec((n, DB), lambda i: (i, 0)),
                  pl.BlockSpec((n, DB), lambda i: (i, 1)),
                  pl.BlockSpec((n, DB), lambda i: (i, 2))],
        out_specs=pl.BlockSpec((n, DB), lambda i: (i, 0)),
        out_shape=jax.ShapeDtypeStruct((b * n, DB), F32),
        compiler_params=_cparams(1),
    )(qkv, qkv, qkv)


def _na_key_start(rb, rows):
    return jnp.clip(rb * NA_RB - WIN_ROWS // 2, 0, rows - NA_KR)


def _na_kernel(q_ref, k_ref, v_ref, kc_ref, vc_ref, tp_ref, rm_ref, o_ref, *, rows):
    scale = DH ** -0.5
    m0 = _pair_masks()
    rb = pl.program_id(1)
    u0 = _na_key_start(rb, rows)
    start = pl.multiple_of(u0 * GRID_W, GRID_W)
    nk = NA_KR * GRID_W
    rel = [[jnp.clip(u0 + 2 * m - (rb * NA_RB + qr) + WIN_ROWS, 0, 2 * WIN_ROWS)
            for m in range(NA_KR // 2)] for qr in range(NA_RB)]
    row_mask = rm_ref[...]

    def bias(h):
        return jnp.concatenate(
            [jnp.concatenate([tp_ref[h, rel[qr][m]] for m in range(NA_KR // 2)], axis=1)
             for qr in range(NA_RB)], axis=0) + row_mask

    for p in range(HEADS // 2):
        sl = slice(p * LANES, (p + 1) * LANES)
        qp = q_ref[:, sl] * scale
        kp = k_ref[pl.ds(start, nk), sl]
        vp = v_ref[pl.ds(start, nk), sl]
        kcp, vcp = kc_ref[:, sl], vc_ref[:, sl]
        o_pair = None
        for hh in range(2):
            msk = m0 if hh == 0 else jnp.logical_not(m0)
            qm = jnp.where(msk, qp, jnp.zeros_like(qp))
            s_loc = _dot_nt(qm, kp) + bias(2 * p + hh)
            s_ctx = _dot_nt(qm, kcp)
            o = _softmax_pv([(s_loc, vp), (s_ctx, vcp)])
            o_pair = o if o_pair is None else jnp.where(m0, o_pair, o)
        o_ref[:, sl] = o_pair


def _na_bias_tables(rpb):
    nl = rpb.shape[0]
    nrel_r, nrel_c = 2 * WIN_ROWS - 1, 2 * WIN_COLS - 1
    qc = np.arange(GRID_W)[:, None]
    kc = np.arange(GRID_W)[None, :]
    cs = np.clip(qc - WIN_COLS // 2, 0, GRID_W - WIN_COLS)
    col_ok = (kc >= cs) & (kc < cs + WIN_COLS)
    col_rel = np.clip(kc - qc + (WIN_COLS - 1), 0, nrel_c - 1)
    col_hot = (np.arange(nrel_c)[:, None, None] == col_rel[None]) & col_ok[None]
    col_hot = jnp.asarray(col_hot.reshape(nrel_c, GRID_W * GRID_W), F32)
    tb = jnp.dot(rpb.reshape(-1, nrel_c), col_hot, precision=HIGHEST)
    tb = jnp.where(jnp.asarray(col_ok.reshape(1, -1)), tb, NEG)
    tb = tb.reshape(nl, HEADS, nrel_r, GRID_W, GRID_W)
    tb = jnp.pad(tb, ((0, 0), (0, 0), (1, 2), (0, 0), (0, 0)))
    return jnp.concatenate([tb[:, :, :-1], tb[:, :, 1:]], axis=-1)


def _na_row_masks(rows):
    n_rb = rows // NA_RB
    wr = min(WIN_ROWS, rows)
    out = np.full((3, NA_RB, GRID_W, NA_KR, GRID_W), NEG, np.float32)
    for vi, rb in enumerate((0, 1, n_rb - 1)):
        u0 = int(np.clip(rb * NA_RB - WIN_ROWS // 2, 0, rows - NA_KR))
        for qr in range(NA_RB):
            rs = int(np.clip(rb * NA_RB + qr - wr // 2, 0, rows - wr))
            for kr in range(NA_KR):
                if rs <= u0 + kr < rs + wr:
                    out[vi, qr, :, kr, :] = 0.0
    return jnp.asarray(out.reshape(3, NA_RB * GRID_W, NA_KR * GRID_W))


def _na_attention(qkv, k_ctx, v_ctx, bias, row_masks, l, b, n):
    rows = n // GRID_W
    n_rb = rows // NA_RB
    tq = NA_RB * GRID_W
    lc = k_ctx.shape[0] // b

    def variant(j):
        return jnp.where(j == 0, 0, jnp.where(j == n_rb - 1, 2, 1))

    return pl.pallas_call(
        functools.partial(_na_kernel, rows=rows),
        grid=(b, n_rb),
        in_specs=[pl.BlockSpec((tq, DB), lambda i, j: (i * n_rb + j, 0)),
                  pl.BlockSpec((n, DB), lambda i, j: (i, 1)),
                  pl.BlockSpec((n, DB), lambda i, j: (i, 2)),
                  pl.BlockSpec((lc, DB), lambda i, j: (i, 0)),
                  pl.BlockSpec((lc, DB), lambda i, j: (i, 0)),
                  pl.BlockSpec((None, HEADS, 2 * WIN_ROWS + 1, GRID_W, 2 * GRID_W),
                               lambda i, j: (l, 0, 0, 0, 0)),
                  pl.BlockSpec((None, tq, NA_KR * GRID_W), lambda i, j: (variant(j), 0, 0))],
        out_specs=pl.BlockSpec((tq, DB), lambda i, j: (i * n_rb + j, 0)),
        out_shape=jax.ShapeDtypeStruct((b * n, DB), F32),
        compiler_params=_cparams(2),
    )(qkv, qkv, qkv, k_ctx, v_ctx, bias, row_masks)


def _cmul(a, b):
    return a[0] * b[0] - a[1] * b[1], a[0] * b[1] + a[1] * b[0]


def _cexp(re, im):
    e = jnp.exp(re)
    return e * jnp.cos(im), e * jnp.sin(im)


def _s5_params(lam_re, lam_im, log_step, b_re, b_im, c_re, c_im):
    hp = dict(precision=HIGHEST)
    J, nq, ug = S5_J, S5_G // S5_UG, S5_UG
    sw = 2 * ug * S5_P

    def block_diag(x2d, row_group, col_inner):
        rows = x2d.shape[-2]
        ci = np.arange(sw)
        src = (ci // (ug * col_inner)) * col_inner + ci % col_inner
        expand = jnp.asarray(np.arange(LANES)[:, None] == src[None, :], F32)
        keep = jnp.asarray(row_group(np.arange(rows))[:, None] == ((ci // col_inner) % ug)[None, :])
        return jnp.where(keep, jnp.matmul(x2d, expand, precision=HIGHEST), 0.0)

    grp16 = lambda r: (r // S5_CH) % ug
    grp64 = lambda r: (r // S5_P) % ug
    step = jnp.exp(log_step)[..., None]
    lam_bar = _cexp(lam_re * step, lam_im * step)
    den = lam_re * lam_re + lam_im * lam_im
    num = (lam_bar[0] - 1.0, lam_bar[1])
    coef = ((num[0] * lam_re + num[1] * lam_im) / den, (num[1] * lam_re - num[0] * lam_im) / den)
    b_bar = _cmul((coef[0][..., None], coef[1][..., None]), (b_re, b_im))
    dd = jnp.arange(J + 1, dtype=F32)[:, None, None, None]
    lam_pow = _cexp(dd * (lam_re * step)[None], dd * (lam_im * step)[None])

    def units(x, g_axis):
        return x.reshape(x.shape[:g_axis] + (nq, ug) + x.shape[g_axis + 1:])

    sel = [jnp.stack([c[:J][::-1, 0], c[:J][:, 1]], axis=0) for c in lam_pow]
    wb = _cmul((sel[0][..., None], sel[1][..., None]), (b_bar[0][:, None], b_bar[1][:, None]))
    wb = units(jnp.stack(wb, axis=0), 3)
    bcat = block_diag(jnp.transpose(wb, (3, 1, 2, 4, 6, 0, 5)).reshape(nq, 2, J * LANES, LANES),
                      grp16, S5_P)
    lp = [jnp.transpose(c[:J], (1, 2, 0, 3))[:, :, :, None, :] for c in lam_pow]
    cl = _cmul((c_re[:, :, None], c_im[:, :, None]), lp)
    kd = (jnp.einsum('dgjcp,dgpk->dgjck', cl[0], b_bar[0], **hp)
          - jnp.einsum('dgjcp,dgpk->dgjck', cl[1], b_bar[1], **hp))
    oi = np.arange(J)[:, None]
    oo = np.arange(J)[None, :]
    lag_hot = np.stack([(oo - oi)[..., None] == np.arange(J), (oi - oo)[..., None] == np.arange(J)])
    t5 = jnp.einsum('dioj,dgjce->dgioce', jnp.asarray(lag_hot, F32), kd, **hp)
    t5 = units(t5[0] + t5[1], 0)
    tsum = block_diag(jnp.transpose(t5, (0, 2, 1, 5, 3, 4)).reshape(nq, J * LANES, LANES),
                      grp16, S5_CH)
    lq = [jnp.transpose(c[1:], (1, 2, 0, 3))[:, :, :, None, :] for c in lam_pow]
    cm = _cmul((c_re[:, :, None], c_im[:, :, None]), lq)
    cc = jnp.stack([cm[0], -cm[1]], axis=0)
    cc = jnp.stack([cc[:, 0], cc[:, 1, :, ::-1]], axis=1)
    ccat = block_diag(jnp.transpose(units(cc, 2), (2, 1, 0, 3, 6, 4, 5)).reshape(nq, 2 * sw, LANES),
                      grp64, S5_CH)
    b1 = jnp.transpose(units(jnp.stack(b_bar, axis=0), 2), (2, 1, 3, 5, 0, 4))
    b1 = block_diag(b1.reshape(nq, 2, LANES, LANES), grp16, S5_P)

    def lanes(c):
        x = jnp.concatenate([c[0].reshape(2, nq, ug * S5_P), c[1].reshape(2, nq, ug * S5_P)], axis=-1)
        return jnp.transpose(x, (1, 0, 2))[:, :, None, :]

    return dict(bcat=bcat.astype(BF16), tsum=tsum.astype(BF16), ccat=ccat.astype(BF16), b1=b1,
                lj=lanes((lam_pow[0][J], lam_pow[1][J])), l1=lanes(lam_bar))


def _s5_kernel(u_ref, bcat_ref, tsum_ref, ccat_ref, lj_ref, l1_ref, b1_ref, x0_ref,
               y_ref, fin_ref, z_scr, *, nseq, n):
    J = S5_J
    nsub = n // J
    ns = nseq * nsub
    hw = S5_UG * S5_P
    if nseq == 1:
        ucat = jnp.concatenate([u_ref[pl.ds(o, ns, stride=J), :] for o in range(J)], axis=1)
    else:
        ucat = jnp.concatenate(
            [jnp.concatenate([u_ref[pl.ds(j * J + o, nseq, stride=n), :] for o in range(J)], axis=1)
             for j in range(nsub)], axis=0)
    ucat = ucat.astype(BF16)
    for d in range(2):
        z_scr[d] = _dot(ucat, bcat_ref[d])

    def cstep(d, x, z):
        lre, lim = lj_ref[d, :, 0:hw], lj_ref[d, :, hw:]
        return jnp.concatenate([lre * x[:, :hw] - lim * x[:, hw:] + z[:, :hw],
                                lre * x[:, hw:] + lim * x[:, :hw] + z[:, hw:]], axis=1)

    for d in range(2):
        tok = 0 if d == 0 else n - 1
        uf = u_ref[pl.ds(tok, nseq, stride=n), :] if nseq > 1 else u_ref[tok:tok + 1, :]
        bu = jnp.dot(uf, b1_ref[d], precision=HIGHEST, preferred_element_type=F32)
        x0 = x0_ref[d]
        lre, lim = l1_ref[d, :, 0:hw], l1_ref[d, :, hw:]
        fin_ref[d] = jnp.concatenate([lre * x0[:, :hw] - lim * x0[:, hw:] + bu[:, :hw],
                                      lre * x0[:, hw:] + lim * x0[:, :hw] + bu[:, hw:]], axis=1)

    if nseq == 1:
        def step(j, carry):
            out = []
            for d in range(2):
                row = pl.ds(j if d == 0 else nsub - 1 - j, 1)
                z = z_scr[d, row, :]
                z_scr[d, row, :] = carry[d]
                out.append(cstep(d, carry[d], z))
            return tuple(out)

        lax.fori_loop(0, nsub, step, (x0_ref[0], x0_ref[1]), unroll=4)
    else:
        xs = [x0_ref[0], x0_ref[1]]
        for j in range(nsub):
            for d in range(2):
                rows = pl.ds((j if d == 0 else nsub - 1 - j) * nseq, nseq)
                z = z_scr[d, rows, :]
                z_scr[d, rows, :] = xs[d]
                xs[d] = cstep(d, xs[d], z)

    xin = jnp.concatenate([z_scr[0], z_scr[1]], axis=1).astype(BF16)
    ycat = _dot(ucat, tsum_ref[...]) + _dot(xin, ccat_ref[...])
    for o in range(J):
        if nseq == 1:
            y_ref[pl.ds(o, ns, stride=J), :] = ycat[:, o * LANES:(o + 1) * LANES]
        else:
            for j in range(nsub):
                y_ref[pl.ds(j * J + o, nseq, stride=n), :] = ycat[j * nseq:(j + 1) * nseq,
                                                                  o * LANES:(o + 1) * LANES]


def _s5_scan(proj, b, n, prm, l, x0, nseq):
    nq = S5_G // S5_UG
    sw = 2 * S5_UG * S5_P
    jl = S5_J * LANES
    r = nseq * n
    ns = r // S5_J
    if nseq == 1:
        x0 = x0.reshape(nq, 2, b, 1, sw)
        st_spec = pl.BlockSpec((None, 2, None, 1, sw), lambda q, i: (q, 0, i, 0, 0))
    else:
        st_spec = pl.BlockSpec((None, 2, nseq, sw), lambda q, i: (q, 0, i, 0))
    wspec = lambda *shp: pl.BlockSpec((None, None) + shp, lambda q, i: (l, q) + (0,) * len(shp))
    y, fin = pl.pallas_call(
        functools.partial(_s5_kernel, nseq=nseq, n=n),
        grid=(nq, b // nseq),
        in_specs=[pl.BlockSpec((r, LANES), lambda q, i: (i, C_U // LANES + q)),
                  wspec(2, jl, sw), wspec(jl, jl), wspec(2 * sw, jl), wspec(2, 1, sw),
                  wspec(2, 1, sw), wspec(2, LANES, sw), st_spec],
        out_specs=[pl.BlockSpec((r, LANES), lambda q, i: (i, q)), st_spec],
        out_shape=[jax.ShapeDtypeStruct((b * n, DB), F32), jax.ShapeDtypeStruct(x0.shape, F32)],
        scratch_shapes=[pltpu.VMEM((2, ns, sw), F32)],
        compiler_params=_cparams(2),
    )(proj, prm['bcat'], prm['tsum'], prm['ccat'], prm['lj'], prm['l1'], prm['b1'], x0)
    return y, fin.reshape(nq, 2, b, sw)


def _s5_post(u, y_scan, d_skip, w_glu):
    y = jax.nn.gelu(d_skip * u + y_scan)
    return y * _sigmoid(_dot(y.astype(BF16), w_glu))


def _s5_state_to_lanes(s_re, s_im):
    b = s_re.shape[0]
    nq = S5_G // S5_UG
    x = jnp.concatenate([s_re.reshape(b, 2, nq, S5_UG * S5_P), s_im.reshape(b, 2, nq, S5_UG * S5_P)],
                        axis=-1)
    return jnp.transpose(x, (2, 1, 0, 3))


def _s5_lanes_to_state(x):
    nq, _, b, _ = x.shape
    hw = S5_UG * S5_P
    x = jnp.transpose(x, (2, 1, 0, 3))
    return x[..., :hw].reshape(b, 2, S5_G, S5_P), x[..., hw:].reshape(b, 2, S5_G, S5_P)


def _s5_mixer(proj, b, n, sp, l, x0):
    nseq = 8 if (n // S5_J <= 64 and b % 8 == 0) else 1
    return _s5_scan(proj, b, n, sp, l, x0, nseq)


def _rwkv_chunk_kernel(rc_ref, kc_ref, vc_ref, lora_ref, w0_ref, wup_ref, a0_ref, aup_ref,
                       kk_ref, ka_ref, s0_ref, y_ref, sfin_ref, z_scr, *, nc, nseq):
    d = pl.program_id(1)
    ci = pl.program_id(2)
    L = RW_L
    rows = nseq * L

    @pl.when(ci == 0)
    def _():
        z_scr[...] = s0_ref[...]

    flat = lambda ref: ref[...].reshape(rows, ref.shape[-1])
    rc, kc, vc, lora = flat(rc_ref), flat(kc_ref), flat(vc_ref), flat(lora_ref)
    wd = lora[:, 0:64]
    ad = lora[:, 64:128]
    lora_w = _mm3(jnp.tanh(wd), wup_ref[...])
    log_w = -_softplus(-(w0_ref[...] + lora_w)) - 0.5
    lw = -jnp.exp(log_w)
    a = _sigmoid(a0_ref[...] + _mm3(ad, aup_ref[...]))
    kd = kc * (1.0 + (a - 1.0) * ka_ref[...])
    kk = kc * kk_ref[...]
    kk = kk * lax.rsqrt(_dot_rhs_exact(kk * kk, _head_block_matrix(DB, 1.0)) + 1e-12)
    alpha = -kk
    beta = kk * a

    sgn = 1 - 2 * d
    tt = lax.broadcasted_iota(jnp.int32, (rows, rows), 0)
    ss = lax.broadcasted_iota(jnp.int32, (rows, rows), 1)
    tri = jnp.where(((tt % L - ss % L) * sgn >= 0) & (tt // L == ss // L), 1.0, 0.0).astype(BF16)
    c = _dot_lhs_exact(tri, lw)
    c_ex = c - lw
    ctot = jnp.concatenate(
        [jnp.broadcast_to(jnp.sum(lw[s * L:(s + 1) * L], axis=0, keepdims=True), (L, DB))
         for s in range(nseq)], axis=0)
    mid = 0.5 * ctot
    e_in = jnp.exp(c - mid)
    e_ex = jnp.exp(c_ex - mid)
    e_out = jnp.exp(mid - c)
    e_mid = jnp.exp(mid)
    al_t = alpha * e_ex
    r_t = rc * e_in
    be_t = beta * e_out
    k_t = kd * e_out
    a0s = al_t * e_mid
    r0s = r_t * e_mid
    bps = be_t * e_mid
    kps = k_t * e_mid
    p_l = e_mid * e_mid

    m0 = _pair_masks()
    t2 = lax.broadcasted_iota(jnp.int32, (2 * L, 2 * L), 0) % L
    s2 = lax.broadcasted_iota(jnp.int32, (2 * L, 2 * L), 1) % L
    strict = (t2 - s2) * sgn > 0
    incl = (t2 - s2) * sgn >= 0
    eye = (lax.broadcasted_iota(jnp.int32, (LANES, LANES), 0)
           == lax.broadcasted_iota(jnp.int32, (LANES, LANES), 1))
    eye_f = jnp.where(eye, 1.0, 0.0)
    zeros = jnp.zeros((LANES, LANES), F32)

    chains = [(s, p) for s in range(nseq) for p in range(HEADS // 2)]
    pairs = range(len(chains))
    rws = [slice(s * L, (s + 1) * L) for s, _ in chains]
    sls = [slice(p * LANES, (p + 1) * LANES) for _, p in chains]

    def stack(x, p):
        xp = x[rws[p], sls[p]]
        return jnp.concatenate([jnp.where(m0, xp, 0.0), jnp.where(m0, 0.0, xp)], axis=0)

    raws = [_dot_nt(jnp.concatenate([stack(al_t, p), stack(r_t, p)], axis=0).astype(BF16),
                    jnp.concatenate([stack(be_t, p), stack(k_t, p)], axis=0).astype(BF16))
            for p in pairs]
    amat = [jnp.where(strict, r[:LANES, :LANES], 0.0) for r in raws]
    bmat = [jnp.where(strict, r[:LANES, LANES:], 0.0) for r in raws]
    qbk = [jnp.concatenate([jnp.where(incl, r[LANES:, :LANES], 0.0),
                            jnp.where(incl, r[LANES:, LANES:], 0.0)], axis=1) for r in raws]

    smat = [eye_f + a for a in amat]
    pw = [_mm_inv(a, a) for a in amat]
    for _ in range(4):
        xs = [_mm_inv(pw[p], jnp.concatenate([pw[p], smat[p]], axis=1)) for p in pairs]
        smat = [smat[p] + xs[p][:, LANES:] for p in pairs]
        pw = [x[:, :LANES] for x in xs]
    tmat = [smat[p] + _mm_inv(pw[p], smat[p]) for p in pairs]

    vs = [stack(vc, p) for p in pairs]
    bv = [_mm(bmat[p], vs[p]) for p in pairs]
    wu = [_mm_inv(tmat[p], jnp.concatenate([stack(a0s, p), bv[p]], axis=1)) for p in pairs]
    rhs2 = [jnp.concatenate([wu[p], jnp.concatenate([zeros, vs[p]], axis=1)], axis=0).astype(BF16)
            for p in pairs]
    out_a = [_dot(qbk[p].astype(BF16), rhs2[p]) for p in pairs]
    out_b = [_dot_tn(jnp.concatenate([stack(bps, p), stack(kps, p)], axis=0).astype(BF16), rhs2[p])
             for p in pairs]
    hz = [_mm_state(jnp.concatenate(
        [stack(r0s, p) + out_a[p][:, :LANES],
         jnp.where(eye, p_l[rws[p], sls[p]][0:1], 0.0) + out_b[p][:, :LANES]], axis=0), z_scr[chains[p]])
          for p in pairs]
    for p in pairs:
        y = out_a[p][:, LANES:] + hz[p][:LANES]
        z_scr[chains[p]] = hz[p][LANES:] + out_b[p][:, LANES:]
        y_ref[chains[p][0], :, sls[p]] = y[:L] + y[L:]

    @pl.when(ci == nc - 1)
    def _():
        sfin_ref[...] = z_scr[...]


def _rwkv_post(y, rkv_ref, gd, g_up, u_bonus, ln_w, ln_b):
    avg = _head_block_matrix(DB, 1.0 / DH)
    mu = _dot_rhs_exact(y, avg)
    yc = y - mu
    var = _dot_rhs_exact(yc * yc, avg)
    yn = yc * lax.rsqrt(var + GN_EPS) * ln_w + ln_b
    rc, kc, vc = rkv_ref[:, 0:DB], rkv_ref[:, DB:2 * DB], rkv_ref[:, 2 * DB:3 * DB]
    bonus = _dot_rhs_exact(rc * kc * u_bonus, _head_block_matrix(DB, 1.0)) * vc
    g = _dot(_sigmoid(gd).astype(BF16), g_up)
    return (yn + bonus) * g


def _rwkv_mixer(proj, b, n, prm, s0_pairs):
    t = b * n
    nc = n // RW_L
    nseq = 4 if b % 4 == 0 else (2 if b % 2 == 0 else 1)
    hp = HEADS // 2

    def chunk(dd, ci):
        return jnp.where(dd == 0, ci, nc - 1 - ci)

    tok = lambda w, col: pl.BlockSpec((nseq, RW_L, w), lambda i, dd, ci: (i, chunk(dd, ci), col))
    dirp = lambda r: pl.BlockSpec((None, r, DB), lambda i, dd, ci: (dd, 0, 0))
    shared = pl.BlockSpec((1, DB), lambda i, dd, ci: (0, 0))
    state = pl.BlockSpec((nseq, None, hp, LANES, LANES), lambda i, dd, ci: (i, dd, 0, 0, 0))
    main3 = proj.reshape(b, n, W_MAIN)
    y2, sfin = pl.pallas_call(
        functools.partial(_rwkv_chunk_kernel, nc=nc, nseq=nseq),
        grid=(b // nseq, 2, nc),
        in_specs=[tok(DB, 0), tok(DB, 1), tok(DB, 2), tok(256, C_LORA // 256),
                  dirp(1), dirp(64), dirp(1), dirp(64), shared, shared, state],
        out_specs=[pl.BlockSpec((None, nseq, RW_L, DB), lambda i, dd, ci: (dd, i, chunk(dd, ci), 0)),
                   state],
        out_shape=[jax.ShapeDtypeStruct((2, b, n, DB), F32),
                   jax.ShapeDtypeStruct((b, 2, hp, LANES, LANES), F32)],
        scratch_shapes=[pltpu.VMEM((nseq, hp, LANES, LANES), F32)],
        compiler_params=_cparams(3),
    )(main3, main3, main3, main3, prm['w0'], prm['w_up'], prm['a0'], prm['a_up'],
      prm['k_k'], prm['k_a'], s0_pairs)
    return y2.reshape(2, t, DB), sfin


def _state_to_pairs(s):
    b = s.shape[0]
    st = jnp.swapaxes(s, -1, -2).reshape(b, 2, HEADS // 2, 2, DH, DH)
    z = jnp.zeros((b, 2, HEADS // 2, 2, DH, 2, DH), F32)
    z = z.at[:, :, :, 0, :, 0, :].set(st[:, :, :, 0])
    z = z.at[:, :, :, 1, :, 1, :].set(st[:, :, :, 1])
    return z.reshape(b, 2, HEADS // 2, LANES, LANES)


def _pairs_to_state(z):
    b = z.shape[0]
    z = z.reshape(b, 2, HEADS // 2, 2, DH, 2, DH)
    st = jnp.stack([z[:, :, :, 0, :, 0, :], z[:, :, :, 1, :, 1, :]], axis=3)
    return jnp.swapaxes(st.reshape(b, 2, HEADS, DH, DH), -1, -2)


def _merge_kernel(u_ref, ys5_ref, d_ref, wglu_ref,
                  yf_ref, yb_ref, rkv_ref, gd_ref, gup_ref, ub_ref, lnw_ref, lnb_ref,
                  oatt_ref, gs5_ref, grw_ref, gatt_ref, x_ref, ws5_ref, wrw_ref, watt_ref, wmix_ref,
                  gpost_ref, g1_ref, o_ref):
    o_s5 = _s5_post(u_ref[...], ys5_ref[...], d_ref[...], wglu_ref[...])
    o_rw = _rwkv_post(yf_ref[...] + yb_ref[...], rkv_ref, gd_ref[:, 128:256], gup_ref[...],
                      ub_ref[...], lnw_ref[...], lnb_ref[...])
    gate = lambda ref: _sigmoid(ref[...].astype(F32))
    merged = (gate(gs5_ref) * _dot(o_s5.astype(BF16), ws5_ref[...])
              + gate(grw_ref) * _dot(o_rw.astype(BF16), wrw_ref[...])
              + gate(gatt_ref) * _dot(oatt_ref[...].astype(BF16), watt_ref[...]))
    m = _dot(merged.astype(BF16), wmix_ref[...])
    o_ref[...] = x_ref[...] + g1_ref[...] * _rms(m, gpost_ref[...])


def _merge(main, y_s5, y_rw, rkv, o_att, gates, x, s5_d, w_glu_b, rwp, wts, mod_l, row_of_tile,
           g_post, tm):
    t = x.shape[0]
    tok = lambda w, c: pl.BlockSpec((tm, w), lambda i: (i, c))
    full = lambda r, c: pl.BlockSpec((r, c), lambda i: (0, 0))
    return pl.pallas_call(
        _merge_kernel,
        grid=(t // tm,),
        in_specs=[tok(DB, C_U // DB), tok(DB, 0), full(1, DB), full(DB, DB),
                  pl.BlockSpec((None, tm, DB), lambda i: (0, i, 0)),
                  pl.BlockSpec((None, tm, DB), lambda i: (1, i, 0)),
                  tok(3 * DB, 0), tok(256, C_LORA // 256), full(128, DB), full(1, DB), full(1, DB),
                  full(1, DB),
                  tok(DB, 0), tok(D, 0), tok(D, 1), tok(D, 2), tok(D, 0),
                  full(DB, D), full(DB, D), full(DB, D), full(D, D), full(1, D),
                  _mod_spec(2, row_of_tile, 1)],
        out_specs=tok(D, 0),
        out_shape=jax.ShapeDtypeStruct((t, D), F32),
        compiler_params=_cparams(1),
    )(main, y_s5, s5_d, w_glu_b, y_rw, y_rw, rkv, main, rwp['g_up'], rwp['u'], rwp['ln_w'],
      rwp['ln_b'], o_att, gates, gates, gates, x, wts['br_s5'], wts['br_rw'], wts['br_att'],
      wts['mix'], g_post, mod_l)


def _ffn_dense_kernel(x_ref, gpre_ref, sc_ref, sh_ref, w1_ref, w3_ref, w2_ref, gpost_ref, g2_ref,
                      o_ref, h_scr, acc_scr):
    f = pl.program_id(1)

    @pl.when(f == 0)
    def _():
        h = _rms(x_ref[...], gpre_ref[...]) * (1.0 + sc_ref[...]) + sh_ref[...]
        h_scr[...] = h.astype(BF16)
        acc_scr[...] = jnp.zeros_like(acc_scr)

    h = h_scr[...]
    hid = _silu(_dot(h, w1_ref[...])) * _dot(h, w3_ref[...])
    acc_scr[...] += _dot(hid.astype(BF16), w2_ref[...])

    @pl.when(f == pl.num_programs(1) - 1)
    def _():
        o_ref[...] = x_ref[...] + g2_ref[...] * _rms(acc_scr[...], gpost_ref[...])


def _ffn_dense(x, mod_l, row_of_tile, g_pre, g_post, w1, w3, w2, tm, tf):
    t = x.shape[0]
    return pl.pallas_call(
        _ffn_dense_kernel,
        grid=(t // tm, FF_DENSE // tf),
        in_specs=[pl.BlockSpec((tm, D), lambda i, f: (i, 0)),
                  pl.BlockSpec((1, D), lambda i, f: (0, 0)),
                  _mod_spec(4, row_of_tile, 2), _mod_spec(3, row_of_tile, 2),
                  pl.BlockSpec((D, tf), lambda i, f: (0, f)),
                  pl.BlockSpec((D, tf), lambda i, f: (0, f)),
                  pl.BlockSpec((tf, D), lambda i, f: (f, 0)),
                  pl.BlockSpec((1, D), lambda i, f: (0, 0)),
                  _mod_spec(5, row_of_tile, 2)],
        out_specs=pl.BlockSpec((tm, D), lambda i, f: (i, 0)),
        out_shape=jax.ShapeDtypeStruct((t, D), F32),
        scratch_shapes=[pltpu.VMEM((tm, D), BF16), pltpu.VMEM((tm, D), F32)],
        compiler_params=_cparams(2),
    )(x, g_pre, mod_l, mod_l, w1, w3, w2, g_post, mod_l)


def _ffn_moe_kernel(x_ref, gpre_ref, sc_ref, sh_ref, rw_ref, rb_ref, w1_ref, w3_ref, w2_ref,
                    gpost_ref, g2_ref, o_ref, h_scr, comb_scr, acc_scr):
    e = pl.program_id(1)
    lane = lax.broadcasted_iota(jnp.int32, (1, LANES), 1)

    @pl.when(e == 0)
    def _():
        h = _rms(x_ref[...], gpre_ref[...]) * (1.0 + sc_ref[...]) + sh_ref[...]
        h_scr[...] = h.astype(BF16)
        acc_scr[...] = jnp.zeros_like(acc_scr)
        logits = jnp.dot(h, rw_ref[...], precision=HIGHEST, preferred_element_type=F32) + rb_ref[...]
        ex = jnp.exp(logits - jnp.max(logits, axis=-1, keepdims=True))
        probs = ex / jnp.sum(ex, axis=-1, keepdims=True)
        p1 = jnp.max(probs, axis=-1, keepdims=True)
        i1 = jnp.min(jnp.where(probs == p1, lane, LANES), axis=-1, keepdims=True)
        rest = jnp.where(lane == i1, -1.0, probs)
        p2 = jnp.max(rest, axis=-1, keepdims=True)
        i2 = jnp.min(jnp.where(rest == p2, lane, LANES), axis=-1, keepdims=True)
        den = p1 + p2
        comb_scr[...] = jnp.where(lane == i1, p1 / den, 0.0) + jnp.where(lane == i2, p2 / den, 0.0)

    h = h_scr[...]
    cw = jnp.sum(jnp.where(lane == e, comb_scr[...], 0.0), axis=-1, keepdims=True)
    hid = _silu(_dot(h, w1_ref[...])) * _dot(h, w3_ref[...])
    acc_scr[...] += cw * _dot(hid.astype(BF16), w2_ref[...])

    @pl.when(e == N_EXP - 1)
    def _():
        o_ref[...] = x_ref[...] + g2_ref[...] * _rms(acc_scr[...], gpost_ref[...])


def _ffn_moe(x, mod_l, row_of_tile, g_pre, g_post, rw, rb, w1, w3, w2, tm):
    t = x.shape[0]
    return pl.pallas_call(
        _ffn_moe_kernel,
        grid=(t // tm, N_EXP),
        in_specs=[pl.BlockSpec((tm, D), lambda i, e: (i, 0)),
                  pl.BlockSpec((1, D), lambda i, e: (0, 0)),
                  _mod_spec(4, row_of_tile, 2), _mod_spec(3, row_of_tile, 2),
                  pl.BlockSpec((D, LANES), lambda i, e: (0, 0)),
                  pl.BlockSpec((1, LANES), lambda i, e: (0, 0)),
                  pl.BlockSpec((None, D, FF_EXP), lambda i, e: (e, 0, 0)),
                  pl.BlockSpec((None, D, FF_EXP), lambda i, e: (e, 0, 0)),
                  pl.BlockSpec((None, FF_EXP, D), lambda i, e: (e, 0, 0)),
                  pl.BlockSpec((1, D), lambda i, e: (0, 0)),
                  _mod_spec(5, row_of_tile, 2)],
        out_specs=pl.BlockSpec((tm, D), lambda i, e: (i, 0)),
        out_shape=jax.ShapeDtypeStruct((t, D), F32),
        scratch_shapes=[pltpu.VMEM((tm, D), BF16), pltpu.VMEM((tm, LANES), F32),
                        pltpu.VMEM((tm, D), F32)],
        compiler_params=_cparams(2),
    )(x, g_pre, mod_l, mod_l, rw, rb, w1, w3, w2, g_post, mod_l)


def _reorder_w_in(w):
    qkv, u, rkv, lora, gates = (w[:, 0:1536], w[:, 1536:2048], w[:, 2048:3584], w[:, 3584:3840],
                                w[:, 3840:6912])
    return jnp.concatenate([gates, rkv, u, lora, qkv], axis=1)


def _layer(x, b, n, l, mod_l, row_of_tile, P, cache, tm):
    row = lambda a: a.reshape(1, -1)
    tm_proj = min(2 * tm, x.shape[0], n) if cache is not None else min(2 * tm, x.shape[0])
    row_of_ptile = (lambda i: row_of_tile(i * (tm_proj // tm)))
    outs = _projection(x, mod_l, row_of_ptile, row(P['norm_pre_mix'][l]), P['w_in_b'][l],
                       P['rwkv_conv'][l], n, tm_proj, cache is None)
    gates, proj, qkv = outs[:3]
    if cache is None:
        o_att = _ctx_attention(qkv, b, n)
        x0 = jnp.zeros((S5_G // S5_UG, 2, b, 2 * S5_UG * S5_P), F32)
        s0 = jnp.zeros((b, 2, HEADS // 2, LANES, LANES), F32)
    else:
        k_c, v_c, s5re, s5im, rw0 = cache
        o_att = _na_attention(qkv, k_c.reshape(-1, DB).astype(BF16), v_c.reshape(-1, DB).astype(BF16),
                              P['na_bias'], _na_row_masks(n // GRID_W), l, b, n)
        x0 = _s5_state_to_lanes(s5re, s5im)
        s0 = _state_to_pairs(rw0)
    y_s5, s5fin = _s5_mixer(proj, b, n, P['s5'], l, x0)
    y_rw, sfin = _rwkv_mixer(proj, b, n, P['rwkv'][l], s0)
    x = _merge(proj, y_s5, y_rw, proj, o_att, gates, x, row(P['s5_d'][l]), P['s5_w_glu_b'][l],
               P['rwkv'][l], P['merge'][l], mod_l, row_of_tile, row(P['norm_post_mix'][l]), tm)
    i = l // 2
    if l % 2 == 0:
        x = _ffn_dense(x, mod_l, row_of_tile, row(P['norm_pre_ffn'][l]), row(P['norm_post_ffn'][l]),
                       P['dense_w1_b'][i], P['dense_w3_b'][i], P['dense_w2_b'][i], tm, FF_DENSE // 2)
    else:
        x = _ffn_moe(x, mod_l, row_of_ptile, row(P['norm_pre_ffn'][l]), row(P['norm_post_ffn'][l]),
                     P['moe_rw'][i], P['moe_rb'][i], P['moe_w1_b'][i], P['moe_w3_b'][i],
                     P['moe_w2_b'][i], tm_proj)
    if cache is None:
        k_new = outs[3][:, DB:2 * DB]
        v_new = outs[3][:, 2 * DB:3 * DB]
        fre, fim = _s5_lanes_to_state(s5fin)
        return x, (k_new, v_new, fre, fim, sfin)
    return x, None


def kernel(x_prompt, x_sample, cache_k, cache_v, state_s5_re, state_s5_im, state_rwkv, c, c_ctx,
           w_ada, b_ada, norm_pre_mix, norm_post_mix, norm_pre_ffn, norm_post_ffn, w_in,
           s5_lam_re, s5_lam_im, s5_log_step, s5_b_re, s5_b_im, s5_c_re, s5_c_im, s5_d, s5_w_glu,
           rwkv_conv, rwkv_w0, rwkv_w_up, rwkv_a0, rwkv_a_up, rwkv_g_up, rwkv_k_k, rwkv_k_a,
           rwkv_u, rwkv_ln_w, rwkv_ln_b, att_rpb, w_br_s5, w_br_rwkv, w_br_att, w_mix_out,
           dense_w1, dense_w3, dense_w2, moe_router_w, moe_router_b, moe_w1, moe_w3, moe_w2):
    bc, nc_, _ = x_prompt.shape
    bl, nl, _ = x_sample.shape
    tm_c = min(512, bc * nc_)
    tm_l = min(512, nl)

    P = dict(norm_pre_mix=norm_pre_mix, norm_post_mix=norm_post_mix, norm_pre_ffn=norm_pre_ffn,
             norm_post_ffn=norm_post_ffn, s5_d=s5_d, rwkv_conv=rwkv_conv, att_rpb=att_rpb)
    P['w_in_b'] = [_reorder_w_in(w_in[l]).astype(BF16) for l in range(DEPTH)]
    P['s5_w_glu_b'] = s5_w_glu.astype(BF16)
    P['dense_w1_b'], P['dense_w3_b'], P['dense_w2_b'] = (dense_w1.astype(BF16), dense_w3.astype(BF16),
                                                         dense_w2.astype(BF16))
    P['moe_w1_b'], P['moe_w3_b'], P['moe_w2_b'] = (moe_w1.astype(BF16), moe_w3.astype(BF16),
                                                   moe_w2.astype(BF16))
    n_moe = moe_router_w.shape[0]
    P['moe_rw'] = jnp.pad(moe_router_w, ((0, 0), (0, 0), (0, LANES - N_EXP)))
    P['moe_rb'] = jnp.pad(moe_router_b, ((0, 0), (0, LANES - N_EXP)),
                          constant_values=NEG).reshape(n_moe, 1, LANES)
    P['merge'] = [dict(br_s5=w_br_s5[l].astype(BF16), br_rw=w_br_rwkv[l].astype(BF16),
                       br_att=w_br_att[l].astype(BF16), mix=w_mix_out[l].astype(BF16))
                  for l in range(DEPTH)]
    P['s5'] = jax.vmap(_s5_params)(s5_lam_re, s5_lam_im, s5_log_step, s5_b_re, s5_b_im, s5_c_re,
                                   s5_c_im)
    P['na_bias'] = _na_bias_tables(att_rpb)
    P['rwkv'] = []
    for l in range(DEPTH):
        P['rwkv'].append(dict(
            w0=rwkv_w0[l].reshape(2, 1, DB), w_up=rwkv_w_up[l], a0=rwkv_a0[l].reshape(2, 1, DB),
            a_up=rwkv_a_up[l], k_k=rwkv_k_k[l].reshape(1, DB), k_a=rwkv_k_a[l].reshape(1, DB),
            g_up=rwkv_g_up[l].astype(BF16), u=rwkv_u[l].reshape(1, DB),
            ln_w=rwkv_ln_w[l].reshape(1, DB), ln_b=rwkv_ln_b[l].reshape(1, DB)))

    cvec = jnp.zeros((8, D), F32).at[0].set(c_ctx).at[1:1 + bl].set(c)
    mod = _modulation(cvec, w_ada, b_ada)

    xp = x_prompt.reshape(bc * nc_, D)
    ks, vs, s5re, s5im, rws = [], [], [], [], []
    for l in range(DEPTH):
        xp, (k_n, v_n, fre, fim, sfin) = _layer(xp, bc, nc_, l, mod[l], lambda i: 0, P, None, tm_c)
        ks.append(k_n.reshape(bc, nc_, HEADS, DH))
        vs.append(v_n.reshape(bc, nc_, HEADS, DH))
        s5re.append(fre)
        s5im.append(fim)
        rws.append(_pairs_to_state(sfin))
    new_k = jnp.stack(ks, axis=1)
    new_v = jnp.stack(vs, axis=1)
    new_s5_re = jnp.stack(s5re, axis=1)
    new_s5_im = jnp.stack(s5im, axis=1)
    new_rwkv = jnp.stack(rws, axis=1)

    xs = x_sample.reshape(bl * nl, D)
    tiles_per_seq = nl // tm_l
    for l in range(DEPTH):
        cache = (cache_k[:, l], cache_v[:, l], state_s5_re[:, l], state_s5_im[:, l], state_rwkv[:, l])
        xs, _ = _layer(xs, bl, nl, l, mod[l], lambda i: 1 + i // tiles_per_seq, P, cache, tm_l)

    return (xp.reshape(bc, nc_, D), xs.reshape(bl, nl, D), new_k, new_v, new_s5_re, new_s5_im,
            new_rwkv)
```

```python
import functools
import math

import numpy as np
import jax
import jax.numpy as jnp
from jax import lax
from jax.experimental import pallas as pl
from jax.experimental.pallas import tpu as pltpu

F32 = jnp.float32
BF16 = jnp.bfloat16
HIGHEST = lax.Precision.HIGHEST

D = 1024
DEPTH = 2
GRID_W = 64
DH = 64
HEADS = 8
DB = 512
WIN_ROWS = 8
WIN_COLS = 16
S5_CH = 16
S5_G = 32
S5_P = 64
S5_J = 8
S5_UG = 8
FF_DENSE = 2816
N_EXP = 8
FF_EXP = 1024
EPS = 1e-6
GN_EPS = 64e-5
NEG = -1e30

D_IN = 6912
TN_PROJ = 768
J_MAIN = 4
J_QKV = 7
J_RKV_TILES = 2
W_MAIN = (J_QKV - J_MAIN) * TN_PROJ
C_RKV = 0
C_U = 1536
C_LORA = 2048

LANES = 128
RW_L = 64
NA_RB = 4
NA_KR = 12
VMEM_LIMIT = 56 * 1024 * 1024


def _cparams(n_axes, vmem=VMEM_LIMIT):
    return pltpu.CompilerParams(dimension_semantics=("arbitrary",) * n_axes,
                                vmem_limit_bytes=vmem)


def _dot(a, b):
    return jnp.dot(a, b, preferred_element_type=F32)


def _dot_nt(a, b):
    return lax.dot_general(a, b, (((1,), (1,)), ((), ())), preferred_element_type=F32)


def _dot_tn(a, b):
    return lax.dot_general(a, b, (((0,), (0,)), ((), ())), preferred_element_type=F32)


def _split2(x):
    hi = x.astype(BF16)
    lo = (x - hi.astype(F32)).astype(BF16)
    return hi, lo


def _dot_rhs_exact(x, m):
    hi, lo = _split2(x)
    return _dot(hi, m) + _dot(lo, m)


def _dot_lhs_exact(m, x):
    hi, lo = _split2(x)
    return _dot(m, hi) + _dot(m, lo)


def _mm3(a, b):
    a1, a2 = _split2(a)
    b1, b2 = _split2(b)
    return _dot(a1, b1) + _dot(a1, b2) + _dot(a2, b1)


def _mm(a, b):
    return _dot(a.astype(BF16), b.astype(BF16))


_mm_inv = _mm
_mm_state = _mm


def _sigmoid(x):
    return 0.5 * (jnp.tanh(0.5 * x) + 1.0)


def _silu(x):
    return x * _sigmoid(x)


def _softplus(x):
    return jnp.maximum(x, 0.0) + jnp.log(1.0 + jnp.exp(-jnp.abs(x)))


def _rms(x, g):
    return x * lax.rsqrt(jnp.mean(x * x, axis=-1, keepdims=True) + EPS) * g


def _head_block_matrix(width, value):
    r = lax.broadcasted_iota(jnp.int32, (width, width), 0) // DH
    c = lax.broadcasted_iota(jnp.int32, (width, width), 1) // DH
    return jnp.where(r == c, value, 0.0).astype(BF16)


def _mod_kernel(c_ref, w_ref, b_ref, o_ref):
    s = _silu(c_ref[...]).astype(BF16)
    o_ref[...] = _dot(s, w_ref[...].astype(BF16)) + b_ref[...]


def _modulation(cvec, w_ada, b_ada):
    tn = 1536
    out = pl.pallas_call(
        _mod_kernel,
        grid=(DEPTH, 6 * D // tn),
        in_specs=[pl.BlockSpec((8, D), lambda l, j: (0, 0)),
                  pl.BlockSpec((None, D, tn), lambda l, j: (l, 0, j)),
                  pl.BlockSpec((None, 1, tn), lambda l, j: (l, 0, j))],
        out_specs=pl.BlockSpec((None, 8, tn), lambda l, j: (l, 0, j)),
        out_shape=jax.ShapeDtypeStruct((DEPTH, 8, 6 * D), F32),
        compiler_params=_cparams(2),
    )(cvec, w_ada, b_ada.reshape(DEPTH, 1, 6 * D))
    return out.reshape(DEPTH, 8, 1, 6 * D)


def _mod_spec(part, row_of_tile, n_grid):
    if n_grid == 1:
        return pl.BlockSpec((None, 1, D), lambda i: (row_of_tile(i), 0, part))
    return pl.BlockSpec((None, 1, D), lambda i, j: (row_of_tile(i), 0, part))


def _proj_kernel(x_ref, xp_ref, xn_ref, g_ref, sc_ref, sh_ref, w_ref, cw_ref, og_ref, om_ref, oq_ref,
                 *rest, f32_qkv, n):
    h_scr = rest[-1]
    i = pl.program_id(0)
    j = pl.program_id(1)
    tm = x_ref.shape[0]

    @pl.when(j == 0)
    def _():
        norm = lambda x: (_rms(x, g_ref[...]) * (1.0 + sc_ref[...]) + sh_ref[...]).astype(BF16)
        h_scr[0:tm, :] = norm(x_ref[...])
        h_scr[tm:tm + 8, :] = norm(xp_ref[...])
        h_scr[tm + 8:tm + 16, :] = norm(xn_ref[...])

    acc_all = _dot(h_scr[...], w_ref[...])
    acc = acc_all[0:tm]

    @pl.when(j < J_MAIN)
    def _():
        og_ref[...] = acc.astype(BF16)

    @pl.when((j >= J_MAIN) & (j < J_MAIN + J_RKV_TILES))
    def _():
        row = lax.broadcasted_iota(jnp.int32, (tm, 1), 0)
        prev_row = acc_all[tm + 7:tm + 8]
        next_row = acc_all[tm + 8:tm + 9]
        x_dn = pltpu.roll(acc, 1, axis=0)
        x_up = pltpu.roll(acc, tm - 1, axis=0)
        if n >= tm:
            tiles = n // tm
            prev_row = jnp.where(i % tiles == 0, 0.0, prev_row)
            next_row = jnp.where(i % tiles == tiles - 1, 0.0, next_row)
            x_dn = jnp.where(row == 0, prev_row, x_dn)
            x_up = jnp.where(row == tm - 1, next_row, x_up)
        else:
            x_dn = jnp.where(row % n == 0, 0.0, x_dn)
            x_up = jnp.where(row % n == n - 1, 0.0, x_up)
        om_ref[...] = x_dn * cw_ref[0:1, :] + acc * cw_ref[1:2, :] + x_up * cw_ref[2:3, :]

    @pl.when((j >= J_MAIN + J_RKV_TILES) & (j < J_QKV))
    def _():
        om_ref[...] = acc

    @pl.when(j >= J_QKV)
    def _():
        oq_ref[...] = acc.astype(BF16)
        if f32_qkv:
            rest[0][...] = acc


def _projection(x, mod_l, row_of_tile, g_pre, w_in_b, conv_w, n, tm, f32_qkv):
    t = x.shape[0]
    n_tiles = D_IN // TN_PROJ
    nb8 = t // 8
    tile = lambda j0, j1: pl.BlockSpec((tm, TN_PROJ), lambda i, j: (i, jnp.clip(j - j0, 0, j1 - j0 - 1)))
    out_specs = [tile(0, J_MAIN), tile(J_MAIN, J_QKV), tile(J_QKV, n_tiles)]
    out_shape = [jax.ShapeDtypeStruct((t, 3 * D), BF16), jax.ShapeDtypeStruct((t, W_MAIN), F32),
                 jax.ShapeDtypeStruct((t, 3 * DB), BF16)]
    if f32_qkv:
        out_specs.append(tile(J_QKV, n_tiles))
        out_shape.append(jax.ShapeDtypeStruct((t, 3 * DB), F32))
    return pl.pallas_call(
        functools.partial(_proj_kernel, f32_qkv=f32_qkv, n=n),
        grid=(t // tm, n_tiles),
        in_specs=[pl.BlockSpec((tm, D), lambda i, j: (i, 0)),
                  pl.BlockSpec((8, D), lambda i, j: (jnp.maximum(i * (tm // 8) - 1, 0), 0)),
                  pl.BlockSpec((8, D), lambda i, j: (jnp.minimum((i + 1) * (tm // 8), nb8 - 1), 0)),
                  pl.BlockSpec((1, D), lambda i, j: (0, 0)),
                  _mod_spec(1, row_of_tile, 2),
                  _mod_spec(0, row_of_tile, 2),
                  pl.BlockSpec((D, TN_PROJ), lambda i, j: (0, j)),
                  pl.BlockSpec((3, TN_PROJ), lambda i, j: (0, jnp.clip(j - J_MAIN, 0, J_RKV_TILES - 1)))],
        out_specs=out_specs,
        out_shape=out_shape,
        scratch_shapes=[pltpu.VMEM((tm + 16, D), BF16)],
        compiler_params=_cparams(2),
    )(x, x, x, g_pre, mod_l, mod_l, w_in_b, conv_w)


def _pair_masks():
    lane = lax.broadcasted_iota(jnp.int32, (1, LANES), 1)
    return lane < DH


def _softmax_pv(parts):
    m = None
    for s, _ in parts:
        mx = jnp.max(s, axis=-1, keepdims=True)
        m = mx if m is None else jnp.maximum(m, mx)
    es = [jnp.exp(s - m) for s, _ in parts]
    den = None
    for e in es:
        sm = jnp.sum(e, axis=-1, keepdims=True)
        den = sm if den is None else den + sm
    out = None
    for e, (_, v) in zip(es, parts):
        o = _dot(e.astype(BF16), v)
        out = o if out is None else out + o
    return out * (1.0 / den)


def _ctx_att_kernel(q_ref, k_ref, v_ref, o_ref):
    scale = DH ** -0.5
    m0 = _pair_masks()
    for p in range(HEADS // 2):
        sl = slice(p * LANES, (p + 1) * LANES)
        qp, kp, vp = q_ref[:, sl] * scale, k_ref[:, sl], v_ref[:, sl]
        o_pair = None
        for hh in range(2):
            msk = m0 if hh == 0 else jnp.logical_not(m0)
            qm = jnp.where(msk, qp, jnp.zeros_like(qp))
            o = _softmax_pv([(_dot_nt(qm, kp), vp)])
            o_pair = o if o_pair is None else jnp.where(m0, o_pair, o)
        o_ref[:, sl] = o_pair


def _ctx_attention(qkv, b, n):
    return pl.pallas_call(
        _ctx_att_kernel,
        grid=(b,),
        in_specs=[pl.BlockSpec((n, DB), lambda i: (i, 0)),
                  pl.BlockSpec((n, DB), lambda i: (i, 1)),
                  pl.BlockSpec((n, DB), lambda i: (i, 2))],
        out_specs=pl.BlockSpec((n, DB), lambda i: (i, 0)),
        out_shape=jax.ShapeDtypeStruct((b * n, DB), F32),
        compiler_params=_cparams(1),
    )(qkv, qkv, qkv)


def _na_key_start(rb, rows):
    return jnp.clip(rb * NA_RB - WIN_ROWS // 2, 0, rows - NA_KR)


def _na_kernel(q_ref, k_ref, v_ref, kc_ref, vc_ref, tp_ref, rm_ref, o_ref, *, rows):
    scale = DH ** -0.5
    m0 = _pair_masks()
    rb = pl.program_id(1)
    u0 = _na_key_start(rb, rows)
    start = pl.multiple_of(u0 * GRID_W, GRID_W)
    nk = NA_KR * GRID_W
    rel = [[jnp.clip(u0 + 2 * m - (rb * NA_RB + qr) + WIN_ROWS, 0, 2 * WIN_ROWS)
            for m in range(NA_KR // 2)] for qr in range(NA_RB)]
    row_mask = rm_ref[...]

    def bias(h):
        return jnp.concatenate(
            [jnp.concatenate([tp_ref[h, rel[qr][m]] for m in range(NA_KR // 2)], axis=1)
             for qr in range(NA_RB)], axis=0) + row_mask

    for p in range(HEADS // 2):
        sl = slice(p * LANES, (p + 1) * LANES)
        qp = q_ref[:, sl] * scale
        kp = k_ref[pl.ds(start, nk), sl]
        vp = v_ref[pl.ds(start, nk), sl]
        kcp, vcp = kc_ref[:, sl], vc_ref[:, sl]
        o_pair = None
        for hh in range(2):
            msk = m0 if hh == 0 else jnp.logical_not(m0)
            qm = jnp.where(msk, qp, jnp.zeros_like(qp))
            s_loc = _dot_nt(qm, kp) + bias(2 * p + hh)
            s_ctx = _dot_nt(qm, kcp)
            o = _softmax_pv([(s_loc, vp), (s_ctx, vcp)])
            o_pair = o if o_pair is None else jnp.where(m0, o_pair, o)
        o_ref[:, sl] = o_pair


def _na_bias_tables(rpb):
    nl = rpb.shape[0]
    nrel_r, nrel_c = 2 * WIN_ROWS - 1, 2 * WIN_COLS - 1
    qc = np.arange(GRID_W)[:, None]
    kc = np.arange(GRID_W)[None, :]
    cs = np.clip(qc - WIN_COLS // 2, 0, GRID_W - WIN_COLS)
    col_ok = (kc >= cs) & (kc < cs + WIN_COLS)
    col_rel = np.clip(kc - qc + (WIN_COLS - 1), 0, nrel_c - 1)
    col_hot = (np.arange(nrel_c)[:, None, None] == col_rel[None]) & col_ok[None]
    col_hot = jnp.asarray(col_hot.reshape(nrel_c, GRID_W * GRID_W), F32)
    tb = jnp.dot(rpb.reshape(-1, nrel_c), col_hot, precision=HIGHEST)
    tb = jnp.where(jnp.asarray(col_ok.reshape(1, -1)), tb, NEG)
    tb = tb.reshape(nl, HEADS, nrel_r, GRID_W, GRID_W)
    tb = jnp.pad(tb, ((0, 0), (0, 0), (1, 2), (0, 0), (0, 0)))
    return jnp.concatenate([tb[:, :, :-1], tb[:, :, 1:]], axis=-1)


def _na_row_masks(rows):
    n_rb = rows // NA_RB
    wr = min(WIN_ROWS, rows)
    out = np.full((3, NA_RB, GRID_W, NA_KR, GRID_W), NEG, np.float32)
    for vi, rb in enumerate((0, 1, n_rb - 1)):
        u0 = int(np.clip(rb * NA_RB - WIN_ROWS // 2, 0, rows - NA_KR))
        for qr in range(NA_RB):
            rs = int(np.clip(rb * NA_RB + qr - wr // 2, 0, rows - wr))
            for kr in range(NA_KR):
                if rs <= u0 + kr < rs + wr:
                    out[vi, qr, :, kr, :] = 0.0
    return jnp.asarray(out.reshape(3, NA_RB * GRID_W, NA_KR * GRID_W))


def _na_attention(qkv, k_ctx, v_ctx, bias, row_masks, l, b, n):
    rows = n // GRID_W
    n_rb = rows // NA_RB
    tq = NA_RB * GRID_W
    lc = k_ctx.shape[0] // b

    def variant(j):
        return jnp.where(j == 0, 0, jnp.where(j == n_rb - 1, 2, 1))

    return pl.pallas_call(
        functools.partial(_na_kernel, rows=rows),
        grid=(b, n_rb),
        in_specs=[pl.BlockSpec((tq, DB), lambda i, j: (i * n_rb + j, 0)),
                  pl.BlockSpec((n, DB), lambda i, j: (i, 1)),
                  pl.BlockSpec((n, DB), lambda i, j: (i, 2)),
                  pl.BlockSpec((lc, DB), lambda i, j: (i, 0)),
                  pl.BlockSpec((lc, DB), lambda i, j: (i, 0)),
                  pl.BlockSpec((None, HEADS, 2 * WIN_ROWS + 1, GRID_W, 2 * GRID_W),
                               lambda i, j: (l, 0, 0, 0, 0)),
                  pl.BlockSpec((None, tq, NA_KR * GRID_W), lambda i, j: (variant(j), 0, 0))],
        out_specs=pl.BlockSpec((tq, DB), lambda i, j: (i * n_rb + j, 0)),
        out_shape=jax.ShapeDtypeStruct((b * n, DB), F32),
        compiler_params=_cparams(2),
    )(qkv, qkv, qkv, k_ctx, v_ctx, bias, row_masks)


def _cmul(a, b):
    return a[0] * b[0] - a[1] * b[1], a[0] * b[1] + a[1] * b[0]


def _cexp(re, im):
    e = jnp.exp(re)
    return e * jnp.cos(im), e * jnp.sin(im)


def _s5_params(lam_re, lam_im, log_step, b_re, b_im, c_re, c_im):
    hp = dict(precision=HIGHEST)
    J, nq, ug = S5_J, S5_G // S5_UG, S5_UG
    sw = 2 * ug * S5_P

    def block_diag(x2d, row_group, col_inner):
        rows = x2d.shape[-2]
        ci = np.arange(sw)
        src = (ci // (ug * col_inner)) * col_inner + ci % col_inner
        expand = jnp.asarray(np.arange(LANES)[:, None] == src[None, :], F32)
        keep = jnp.asarray(row_group(np.arange(rows))[:, None] == ((ci // col_inner) % ug)[None, :])
        return jnp.where(keep, jnp.matmul(x2d, expand, precision=HIGHEST), 0.0)

    grp16 = lambda r: (r // S5_CH) % ug
    grp64 = lambda r: (r // S5_P) % ug
    step = jnp.exp(log_step)[..., None]
    lam_bar = _cexp(lam_re * step, lam_im * step)
    den = lam_re * lam_re + lam_im * lam_im
    num = (lam_bar[0] - 1.0, lam_bar[1])
    coef = ((num[0] * lam_re + num[1] * lam_im) / den, (num[1] * lam_re - num[0] * lam_im) / den)
    b_bar = _cmul((coef[0][..., None], coef[1][..., None]), (b_re, b_im))
    dd = jnp.arange(J + 1, dtype=F32)[:, None, None, None]
    lam_pow = _cexp(dd * (lam_re * step)[None], dd * (lam_im * step)[None])

    def units(x, g_axis):
        return x.reshape(x.shape[:g_axis] + (nq, ug) + x.shape[g_axis + 1:])

    sel = [jnp.stack([c[:J][::-1, 0], c[:J][:, 1]], axis=0) for c in lam_pow]
    wb = _cmul((sel[0][..., None], sel[1][..., None]), (b_bar[0][:, None], b_bar[1][:, None]))
    wb = units(jnp.stack(wb, axis=0), 3)
    bcat = block_diag(jnp.transpose(wb, (3, 1, 2, 4, 6, 0, 5)).reshape(nq, 2, J * LANES, LANES),
                      grp16, S5_P)
    lp = [jnp.transpose(c[:J], (1, 2, 0, 3))[:, :, :, None, :] for c in lam_pow]
    cl = _cmul((c_re[:, :, None], c_im[:, :, None]), lp)
    kd = (jnp.einsum('dgjcp,dgpk->dgjck', cl[0], b_bar[0], **hp)
          - jnp.einsum('dgjcp,dgpk->dgjck', cl[1], b_bar[1], **hp))
    oi = np.arange(J)[:, None]
    oo = np.arange(J)[None, :]
    lag_hot = np.stack([(oo - oi)[..., None] == np.arange(J), (oi - oo)[..., None] == np.arange(J)])
    t5 = jnp.einsum('dioj,dgjce->dgioce', jnp.asarray(lag_hot, F32), kd, **hp)
    t5 = units(t5[0] + t5[1], 0)
    tsum = block_diag(jnp.transpose(t5, (0, 2, 1, 5, 3, 4)).reshape(nq, J * LANES, LANES),
                      grp16, S5_CH)
    lq = [jnp.transpose(c[1:], (1, 2, 0, 3))[:, :, :, None, :] for c in lam_pow]
    cm = _cmul((c_re[:, :, None], c_im[:, :, None]), lq)
    cc = jnp.stack([cm[0], -cm[1]], axis=0)
    cc = jnp.stack([cc[:, 0], cc[:, 1, :, ::-1]], axis=1)
    ccat = block_diag(jnp.transpose(units(cc, 2), (2, 1, 0, 3, 6, 4, 5)).reshape(nq, 2 * sw, LANES),
                      grp64, S5_CH)
    b1 = jnp.transpose(units(jnp.stack(b_bar, axis=0), 2), (2, 1, 3, 5, 0, 4))
    b1 = block_diag(b1.reshape(nq, 2, LANES, LANES), grp16, S5_P)

    def lanes(c):
        x = jnp.concatenate([c[0].reshape(2, nq, ug * S5_P), c[1].reshape(2, nq, ug * S5_P)], axis=-1)
        return jnp.transpose(x, (1, 0, 2))[:, :, None, :]

    return dict(bcat=bcat.astype(BF16), tsum=tsum.astype(BF16), ccat=ccat.astype(BF16), b1=b1,
                lj=lanes((lam_pow[0][J], lam_pow[1][J])), l1=lanes(lam_bar))


def _s5_kernel(u_ref, bcat_ref, tsum_ref, ccat_ref, lj_ref, l1_ref, b1_ref, x0_ref,
               y_ref, fin_ref, z_scr, *, nseq, n):
    J = S5_J
    nsub = n // J
    ns = nseq * nsub
    hw = S5_UG * S5_P
    if nseq == 1:
        ucat = jnp.concatenate([u_ref[pl.ds(o, ns, stride=J), :] for o in range(J)], axis=1)
    else:
        ucat = jnp.concatenate(
            [jnp.concatenate([u_ref[pl.ds(j * J + o, nseq, stride=n), :] for o in range(J)], axis=1)
             for j in range(nsub)], axis=0)
    ucat = ucat.astype(BF16)
    for d in range(2):
        z_scr[d] = _dot(ucat, bcat_ref[d])

    def cstep(d, x, z):
        lre, lim = lj_ref[d, :, 0:hw], lj_ref[d, :, hw:]
        return jnp.concatenate([lre * x[:, :hw] - lim * x[:, hw:] + z[:, :hw],
                                lre * x[:, hw:] + lim * x[:, :hw] + z[:, hw:]], axis=1)

    for d in range(2):
        tok = 0 if d == 0 else n - 1
        uf = u_ref[pl.ds(tok, nseq, stride=n), :] if nseq > 1 else u_ref[tok:tok + 1, :]
        bu = jnp.dot(uf, b1_ref[d], precision=HIGHEST, preferred_element_type=F32)
        x0 = x0_ref[d]
        lre, lim = l1_ref[d, :, 0:hw], l1_ref[d, :, hw:]
        fin_ref[d] = jnp.concatenate([lre * x0[:, :hw] - lim * x0[:, hw:] + bu[:, :hw],
                                      lre * x0[:, hw:] + lim * x0[:, :hw] + bu[:, hw:]], axis=1)

    if nseq == 1:
        def step(j, carry):
            out = []
            for d in range(2):
                row = pl.ds(j if d == 0 else nsub - 1 - j, 1)
                z = z_scr[d, row, :]
                z_scr[d, row, :] = carry[d]
                out.append(cstep(d, carry[d], z))
            return tuple(out)

        lax.fori_loop(0, nsub, step, (x0_ref[0], x0_ref[1]), unroll=4)
    else:
        xs = [x0_ref[0], x0_ref[1]]
        for j in range(nsub):
            for d in range(2):
                rows = pl.ds((j if d == 0 else nsub - 1 - j) * nseq, nseq)
                z = z_scr[d, rows, :]
                z_scr[d, rows, :] = xs[d]
                xs[d] = cstep(d, xs[d], z)

    xin = jnp.concatenate([z_scr[0], z_scr[1]], axis=1).astype(BF16)
    ycat = _dot(ucat, tsum_ref[...]) + _dot(xin, ccat_ref[...])
    for o in range(J):
        if nseq == 1:
            y_ref[pl.ds(o, ns, stride=J), :] = ycat[:, o * LANES:(o + 1) * LANES]
        else:
            for j in range(nsub):
                y_ref[pl.ds(j * J + o, nseq, stride=n), :] = ycat[j * nseq:(j + 1) * nseq,
                                                                  o * LANES:(o + 1) * LANES]


def _s5_scan(proj, b, n, prm, l, x0, nseq):
    nq = S5_G // S5_UG
    sw = 2 * S5_UG * S5_P
    jl = S5_J * LANES
    r = nseq * n
    ns = r // S5_J
    if nseq == 1:
        x0 = x0.reshape(nq, 2, b, 1, sw)
        st_spec = pl.BlockSpec((None, 2, None, 1, sw), lambda q, i: (q, 0, i, 0, 0))
    else:
        st_spec = pl.BlockSpec((None, 2, nseq, sw), lambda q, i: (q, 0, i, 0))
    wspec = lambda *shp: pl.BlockSpec((None, None) + shp, lambda q, i: (l, q) + (0,) * len(shp))
    y, fin = pl.pallas_call(
        functools.partial(_s5_kernel, nseq=nseq, n=n),
        grid=(nq, b // nseq),
        in_specs=[pl.BlockSpec((r, LANES), lambda q, i: (i, C_U // LANES + q)),
                  wspec(2, jl, sw), wspec(jl, jl), wspec(2 * sw, jl), wspec(2, 1, sw),
                  wspec(2, 1, sw), wspec(2, LANES, sw), st_spec],
        out_specs=[pl.BlockSpec((r, LANES), lambda q, i: (i, q)), st_spec],
        out_shape=[jax.ShapeDtypeStruct((b * n, DB), F32), jax.ShapeDtypeStruct(x0.shape, F32)],
        scratch_shapes=[pltpu.VMEM((2, ns, sw), F32)],
        compiler_params=_cparams(2),
    )(proj, prm['bcat'], prm['tsum'], prm['ccat'], prm['lj'], prm['l1'], prm['b1'], x0)
    return y, fin.reshape(nq, 2, b, sw)


def _s5_post(u, y_scan, d_skip, w_glu):
    y = jax.nn.gelu(d_skip * u + y_scan)
    return y * _sigmoid(_dot(y.astype(BF16), w_glu))


def _s5_state_to_lanes(s_re, s_im):
    b = s_re.shape[0]
    nq = S5_G // S5_UG
    x = jnp.concatenate([s_re.reshape(b, 2, nq, S5_UG * S5_P), s_im.reshape(b, 2, nq, S5_UG * S5_P)],
                        axis=-1)
    return jnp.transpose(x, (2, 1, 0, 3))


def _s5_lanes_to_state(x):
    nq, _, b, _ = x.shape
    hw = S5_UG * S5_P
    x = jnp.transpose(x, (2, 1, 0, 3))
    return x[..., :hw].reshape(b, 2, S5_G, S5_P), x[..., hw:].reshape(b, 2, S5_G, S5_P)


def _s5_mixer(proj, b, n, sp, l, x0):
    nseq = 8 if (n // S5_J <= 64 and b % 8 == 0) else 1
    return _s5_scan(proj, b, n, sp, l, x0, nseq)


def _rwkv_chunk_kernel(rc_ref, kc_ref, vc_ref, lora_ref, w0_ref, wup_ref, a0_ref, aup_ref,
                       kk_ref, ka_ref, s0_ref, y_ref, sfin_ref, z_scr, *, nc, nseq):
    d = pl.program_id(1)
    ci = pl.program_id(2)
    L = RW_L
    rows = nseq * L

    @pl.when(ci == 0)
    def _():
        z_scr[...] = s0_ref[...]

    flat = lambda ref: ref[...].reshape(rows, ref.shape[-1])
    rc, kc, vc, lora = flat(rc_ref), flat(kc_ref), flat(vc_ref), flat(lora_ref)
    wd = lora[:, 0:64]
    ad = lora[:, 64:128]
    lora_w = _mm3(jnp.tanh(wd), wup_ref[...])
    log_w = -_softplus(-(w0_ref[...] + lora_w)) - 0.5
    lw = -jnp.exp(log_w)
    a = _sigmoid(a0_ref[...] + _mm3(ad, aup_ref[...]))
    kd = kc * (1.0 + (a - 1.0) * ka_ref[...])
    kk = kc * kk_ref[...]
    kk = kk * lax.rsqrt(_mm(kk * kk, _head_block_matrix(DB, 1.0)) + 1e-12)
    alpha = -kk
    beta = kk * a

    sgn = 1 - 2 * d
    tt = lax.broadcasted_iota(jnp.int32, (rows, rows), 0)
    ss = lax.broadcasted_iota(jnp.int32, (rows, rows), 1)
    tri = jnp.where(((tt % L - ss % L) * sgn >= 0) & (tt // L == ss // L), 1.0, 0.0).astype(BF16)
    c = _dot_lhs_exact(tri, lw)
    c_ex = c - lw
    ctot = jnp.concatenate(
        [jnp.broadcast_to(jnp.sum(lw[s * L:(s + 1) * L], axis=0, keepdims=True), (L, DB))
         for s in range(nseq)], axis=0)
    mid = 0.5 * ctot
    e_in = jnp.exp(c - mid)
    e_ex = jnp.exp(c_ex - mid)
    e_out = jnp.exp(mid - c)
    e_mid = jnp.exp(mid)
    al_t = alpha * e_ex
    r_t = rc * e_in
    be_t = beta * e_out
    k_t = kd * e_out
    a0s = al_t * e_mid
    r0s = r_t * e_mid
    bps = be_t * e_mid
    kps = k_t * e_mid
    p_l = e_mid * e_mid

    m0 = _pair_masks()
    t2 = lax.broadcasted_iota(jnp.int32, (2 * L, 2 * L), 0) % L
    s2 = lax.broadcasted_iota(jnp.int32, (2 * L, 2 * L), 1) % L
    strict = (t2 - s2) * sgn > 0
    incl = (t2 - s2) * sgn >= 0
    eye = (lax.broadcasted_iota(jnp.int32, (LANES, LANES), 0)
           == lax.broadcasted_iota(jnp.int32, (LANES, LANES), 1))
    eye_f = jnp.where(eye, 1.0, 0.0)
    zeros = jnp.zeros((LANES, LANES), F32)

    chains = [(s, p) for s in range(nseq) for p in range(HEADS // 2)]
    pairs = range(len(chains))
    rws = [slice(s * L, (s + 1) * L) for s, _ in chains]
    sls = [slice(p * LANES, (p + 1) * LANES) for _, p in chains]

    def stack(x, p):
        xp = x[rws[p], sls[p]]
        return jnp.concatenate([jnp.where(m0, xp, 0.0), jnp.where(m0, 0.0, xp)], axis=0)

    raws = [_dot_nt(jnp.concatenate([stack(al_t, p), stack(r_t, p)], axis=0).astype(BF16),
                    jnp.concatenate([stack(be_t, p), stack(k_t, p)], axis=0).astype(BF16))
            for p in pairs]
    amat = [jnp.where(strict, r[:LANES, :LANES], 0.0) for r in raws]
    bmat = [jnp.where(strict, r[:LANES, LANES:], 0.0) for r in raws]
    qbk = [jnp.concatenate([jnp.where(incl, r[LANES:, :LANES], 0.0),
                            jnp.where(incl, r[LANES:, LANES:], 0.0)], axis=1) for r in raws]

    smat = [eye_f + a for a in amat]
    pw = [_mm_inv(a, a) for a in amat]
    for _ in range(4):
        xs = [_mm_inv(pw[p], jnp.concatenate([pw[p], smat[p]], axis=1)) for p in pairs]
        smat = [smat[p] + xs[p][:, LANES:] for p in pairs]
        pw = [x[:, :LANES] for x in xs]
    tmat = [smat[p] + _mm_inv(pw[p], smat[p]) for p in pairs]

    vs = [stack(vc, p) for p in pairs]
    bv = [_mm(bmat[p], vs[p]) for p in pairs]
    wu = [_mm_inv(tmat[p], jnp.concatenate([stack(a0s, p), bv[p]], axis=1)) for p in pairs]
    rhs2 = [jnp.concatenate([wu[p], jnp.concatenate([zeros, vs[p]], axis=1)], axis=0).astype(BF16)
            for p in pairs]
    out_a = [_dot(qbk[p].astype(BF16), rhs2[p]) for p in pairs]
    out_b = [_dot_tn(jnp.concatenate([stack(bps, p), stack(kps, p)], axis=0).astype(BF16), rhs2[p])
             for p in pairs]
    hz = [_mm_state(jnp.concatenate(
        [stack(r0s, p) + out_a[p][:, :LANES],
         jnp.where(eye, p_l[rws[p], sls[p]][0:1], 0.0) + out_b[p][:, :LANES]], axis=0), z_scr[chains[p]])
          for p in pairs]
    for p in pairs:
        y = out_a[p][:, LANES:] + hz[p][:LANES]
        z_scr[chains[p]] = hz[p][LANES:] + out_b[p][:, LANES:]
        y_ref[chains[p][0], :, sls[p]] = y[:L] + y[L:]

    @pl.when(ci == nc - 1)
    def _():
        sfin_ref[...] = z_scr[...]


def _rwkv_post(y, rkv_ref, gd, g_up, u_bonus, ln_w, ln_b):
    avg = _head_block_matrix(DB, 1.0 / DH)
    mu = _dot_rhs_exact(y, avg)
    yc = y - mu
    var = _dot_rhs_exact(yc * yc, avg)
    yn = yc * lax.rsqrt(var + GN_EPS) * ln_w + ln_b
    rc, kc, vc = rkv_ref[:, 0:DB], rkv_ref[:, DB:2 * DB], rkv_ref[:, 2 * DB:3 * DB]
    bonus = _mm(rc * kc * u_bonus, _head_block_matrix(DB, 1.0)) * vc
    g = _dot(_sigmoid(gd).astype(BF16), g_up)
    return (yn + bonus) * g


def _rwkv_mixer(proj, b, n, prm, s0_pairs):
    t = b * n
    nc = n // RW_L
    nseq = 4 if b % 4 == 0 else (2 if b % 2 == 0 else 1)
    hp = HEADS // 2

    def chunk(dd, ci):
        return jnp.where(dd == 0, ci, nc - 1 - ci)

    tok = lambda w, col: pl.BlockSpec((nseq, RW_L, w), lambda i, dd, ci: (i, chunk(dd, ci), col))
    dirp = lambda r: pl.BlockSpec((None, r, DB), lambda i, dd, ci: (dd, 0, 0))
    shared = pl.BlockSpec((1, DB), lambda i, dd, ci: (0, 0))
    state = pl.BlockSpec((nseq, None, hp, LANES, LANES), lambda i, dd, ci: (i, dd, 0, 0, 0))
    main3 = proj.reshape(b, n, W_MAIN)
    y2, sfin = pl.pallas_call(
        functools.partial(_rwkv_chunk_kernel, nc=nc, nseq=nseq),
        grid=(b // nseq, 2, nc),
        in_specs=[tok(DB, 0), tok(DB, 1), tok(DB, 2), tok(256, C_LORA // 256),
                  dirp(1), dirp(64), dirp(1), dirp(64), shared, shared, state],
        out_specs=[pl.BlockSpec((None, nseq, RW_L, DB), lambda i, dd, ci: (dd, i, chunk(dd, ci), 0)),
                   state],
        out_shape=[jax.ShapeDtypeStruct((2, b, n, DB), F32),
                   jax.ShapeDtypeStruct((b, 2, hp, LANES, LANES), F32)],
        scratch_shapes=[pltpu.VMEM((nseq, hp, LANES, LANES), F32)],
        compiler_params=_cparams(3),
    )(main3, main3, main3, main3, prm['w0'], prm['w_up'], prm['a0'], prm['a_up'],
      prm['k_k'], prm['k_a'], s0_pairs)
    return y2.reshape(2, t, DB), sfin


def _state_to_pairs(s):
    b = s.shape[0]
    st = jnp.swapaxes(s, -1, -2).reshape(b, 2, HEADS // 2, 2, DH, DH)
    z = jnp.zeros((b, 2, HEADS // 2, 2, DH, 2, DH), F32)
    z = z.at[:, :, :, 0, :, 0, :].set(st[:, :, :, 0])
    z = z.at[:, :, :, 1, :, 1, :].set(st[:, :, :, 1])
    return z.reshape(b, 2, HEADS // 2, LANES, LANES)


def _pairs_to_state(z):
    b = z.shape[0]
    z = z.reshape(b, 2, HEADS // 2, 2, DH, 2, DH)
    st = jnp.stack([z[:, :, :, 0, :, 0, :], z[:, :, :, 1, :, 1, :]], axis=3)
    return jnp.swapaxes(st.reshape(b, 2, HEADS, DH, DH), -1, -2)


def _merge_kernel(u_ref, ys5_ref, d_ref, wglu_ref,
                  yf_ref, yb_ref, rkv_ref, gd_ref, gup_ref, ub_ref, lnw_ref, lnb_ref,
                  oatt_ref, gs5_ref, grw_ref, gatt_ref, x_ref, ws5_ref, wrw_ref, watt_ref, wmix_ref,
                  gpost_ref, g1_ref, o_ref):
    o_s5 = _s5_post(u_ref[...], ys5_ref[...], d_ref[...], wglu_ref[...])
    o_rw = _rwkv_post(yf_ref[...] + yb_ref[...], rkv_ref, gd_ref[:, 128:256], gup_ref[...],
                      ub_ref[...], lnw_ref[...], lnb_ref[...])
    gate = lambda ref: _sigmoid(ref[...].astype(F32))
    merged = (gate(gs5_ref) * _dot(o_s5.astype(BF16), ws5_ref[...])
              + gate(grw_ref) * _dot(o_rw.astype(BF16), wrw_ref[...])
              + gate(gatt_ref) * _dot(oatt_ref[...].astype(BF16), watt_ref[...]))
    m = _dot(merged.astype(BF16), wmix_ref[...])
    o_ref[...] = x_ref[...] + g1_ref[...] * _rms(m, gpost_ref[...])


def _merge(main, y_s5, y_rw, rkv, o_att, gates, x, s5_d, w_glu_b, rwp, wts, mod_l, row_of_tile,
           g_post, tm):
    t = x.shape[0]
    tok = lambda w, c: pl.BlockSpec((tm, w), lambda i: (i, c))
    full = lambda r, c: pl.BlockSpec((r, c), lambda i: (0, 0))
    return pl.pallas_call(
        _merge_kernel,
        grid=(t // tm,),
        in_specs=[tok(DB, C_U // DB), tok(DB, 0), full(1, DB), full(DB, DB),
                  pl.BlockSpec((None, tm, DB), lambda i: (0, i, 0)),
                  pl.BlockSpec((None, tm, DB), lambda i: (1, i, 0)),
                  tok(3 * DB, 0), tok(256, C_LORA // 256), full(128, DB), full(1, DB), full(1, DB),
                  full(1, DB),
                  tok(DB, 0), tok(D, 0), tok(D, 1), tok(D, 2), tok(D, 0),
                  full(DB, D), full(DB, D), full(DB, D), full(D, D), full(1, D),
                  _mod_spec(2, row_of_tile, 1)],
        out_specs=tok(D, 0),
        out_shape=jax.ShapeDtypeStruct((t, D), F32),
        compiler_params=_cparams(1),
    )(main, y_s5, s5_d, w_glu_b, y_rw, y_rw, rkv, main, rwp['g_up'], rwp['u'], rwp['ln_w'],
      rwp['ln_b'], o_att, gates, gates, gates, x, wts['br_s5'], wts['br_rw'], wts['br_att'],
      wts['mix'], g_post, mod_l)


def _ffn_dense_kernel(x_ref, gpre_ref, sc_ref, sh_ref, w1_ref, w3_ref, w2_ref, gpost_ref, g2_ref,
                      o_ref, h_scr, acc_scr):
    f = pl.program_id(1)

    @pl.when(f == 0)
    def _():
        h = _rms(x_ref[...], gpre_ref[...]) * (1.0 + sc_ref[...]) + sh_ref[...]
        h_scr[...] = h.astype(BF16)
        acc_scr[...] = jnp.zeros_like(acc_scr)

    h = h_scr[...]
    hid = _silu(_dot(h, w1_ref[...])) * _dot(h, w3_ref[...])
    acc_scr[...] += _dot(hid.astype(BF16), w2_ref[...])

    @pl.when(f == pl.num_programs(1) - 1)
    def _():
        o_ref[...] = x_ref[...] + g2_ref[...] * _rms(acc_scr[...], gpost_ref[...])


def _ffn_dense(x, mod_l, row_of_tile, g_pre, g_post, w1, w3, w2, tm, tf):
    t = x.shape[0]
    return pl.pallas_call(
        _ffn_dense_kernel,
        grid=(t // tm, FF_DENSE // tf),
        in_specs=[pl.BlockSpec((tm, D), lambda i, f: (i, 0)),
                  pl.BlockSpec((1, D), lambda i, f: (0, 0)),
                  _mod_spec(4, row_of_tile, 2), _mod_spec(3, row_of_tile, 2),
                  pl.BlockSpec((D, tf), lambda i, f: (0, f)),
                  pl.BlockSpec((D, tf), lambda i, f: (0, f)),
                  pl.BlockSpec((tf, D), lambda i, f: (f, 0)),
                  pl.BlockSpec((1, D), lambda i, f: (0, 0)),
                  _mod_spec(5, row_of_tile, 2)],
        out_specs=pl.BlockSpec((tm, D), lambda i, f: (i, 0)),
        out_shape=jax.ShapeDtypeStruct((t, D), F32),
        scratch_shapes=[pltpu.VMEM((tm, D), BF16), pltpu.VMEM((tm, D), F32)],
        compiler_params=_cparams(2),
    )(x, g_pre, mod_l, mod_l, w1, w3, w2, g_post, mod_l)


def _ffn_moe_kernel(x_ref, gpre_ref, sc_ref, sh_ref, rw_ref, rb_ref, w1_ref, w3_ref, w2_ref,
                    gpost_ref, g2_ref, o_ref, h_scr, comb_scr, acc_scr):
    e = pl.program_id(1)
    lane = lax.broadcasted_iota(jnp.int32, (1, LANES), 1)

    @pl.when(e == 0)
    def _():
        h = _rms(x_ref[...], gpre_ref[...]) * (1.0 + sc_ref[...]) + sh_ref[...]
        h_scr[...] = h.astype(BF16)
        acc_scr[...] = jnp.zeros_like(acc_scr)
        logits = jnp.dot(h, rw_ref[...], precision=HIGHEST, preferred_element_type=F32) + rb_ref[...]
        ex = jnp.exp(logits - jnp.max(logits, axis=-1, keepdims=True))
        probs = ex / jnp.sum(ex, axis=-1, keepdims=True)
        p1 = jnp.max(probs, axis=-1, keepdims=True)
        i1 = jnp.min(jnp.where(probs == p1, lane, LANES), axis=-1, keepdims=True)
        rest = jnp.where(lane == i1, -1.0, probs)
        p2 = jnp.max(rest, axis=-1, keepdims=True)
        i2 = jnp.min(jnp.where(rest == p2, lane, LANES), axis=-1, keepdims=True)
        den = p1 + p2
        comb_scr[...] = jnp.where(lane == i1, p1 / den, 0.0) + jnp.where(lane == i2, p2 / den, 0.0)

    h = h_scr[...]
    cw = jnp.sum(jnp.where(lane == e, comb_scr[...], 0.0), axis=-1, keepdims=True)
    hid = _silu(_dot(h, w1_ref[...])) * _dot(h, w3_ref[...])
    acc_scr[...] += cw * _dot(hid.astype(BF16), w2_ref[...])

    @pl.when(e == N_EXP - 1)
    def _():
        o_ref[...] = x_ref[...] + g2_ref[...] * _rms(acc_scr[...], gpost_ref[...])


def _ffn_moe(x, mod_l, row_of_tile, g_pre, g_post, rw, rb, w1, w3, w2, tm):
    t = x.shape[0]
    return pl.pallas_call(
        _ffn_moe_kernel,
        grid=(t // tm, N_EXP),
        in_specs=[pl.BlockSpec((tm, D), lambda i, e: (i, 0)),
                  pl.BlockSpec((1, D), lambda i, e: (0, 0)),
                  _mod_spec(4, row_of_tile, 2), _mod_spec(3, row_of_tile, 2),
                  pl.BlockSpec((D, LANES), lambda i, e: (0, 0)),
                  pl.BlockSpec((1, LANES), lambda i, e: (0, 0)),
                  pl.BlockSpec((None, D, FF_EXP), lambda i, e: (e, 0, 0)),
                  pl.BlockSpec((None, D, FF_EXP), lambda i, e: (e, 0, 0)),
                  pl.BlockSpec((None, FF_EXP, D), lambda i, e: (e, 0, 0)),
                  pl.BlockSpec((1, D), lambda i, e: (0, 0)),
                  _mod_spec(5, row_of_tile, 2)],
        out_specs=pl.BlockSpec((tm, D), lambda i, e: (i, 0)),
        out_shape=jax.ShapeDtypeStruct((t, D), F32),
        scratch_shapes=[pltpu.VMEM((tm, D), BF16), pltpu.VMEM((tm, LANES), F32),
                        pltpu.VMEM((tm, D), F32)],
        compiler_params=_cparams(2),
    )(x, g_pre, mod_l, mod_l, rw, rb, w1, w3, w2, g_post, mod_l)


def _reorder_w_in(w):
    qkv, u, rkv, lora, gates = (w[:, 0:1536], w[:, 1536:2048], w[:, 2048:3584], w[:, 3584:3840],
                                w[:, 3840:6912])
    return jnp.concatenate([gates, rkv, u, lora, qkv], axis=1)


def _layer(x, b, n, l, mod_l, row_of_tile, P, cache, tm):
    row = lambda a: a.reshape(1, -1)
    tm_proj = min(2 * tm, x.shape[0], n) if cache is not None else min(2 * tm, x.shape[0])
    row_of_ptile = (lambda i: row_of_tile(i * (tm_proj // tm)))
    outs = _projection(x, mod_l, row_of_ptile, row(P['norm_pre_mix'][l]), P['w_in_b'][l],
                       P['rwkv_conv'][l], n, tm_proj, cache is None)
    gates, proj, qkv = outs[:3]
    if cache is None:
        o_att = _ctx_attention(qkv, b, n)
        x0 = jnp.zeros((S5_G // S5_UG, 2, b, 2 * S5_UG * S5_P), F32)
        s0 = jnp.zeros((b, 2, HEADS // 2, LANES, LANES), F32)
    else:
        k_c, v_c, s5re, s5im, rw0 = cache
        o_att = _na_attention(qkv, k_c.reshape(-1, DB).astype(BF16), v_c.reshape(-1, DB).astype(BF16),
                              P['na_bias'], _na_row_masks(n // GRID_W), l, b, n)
        x0 = _s5_state_to_lanes(s5re, s5im)
        s0 = _state_to_pairs(rw0)
    y_s5, s5fin = _s5_mixer(proj, b, n, P['s5'], l, x0)
    y_rw, sfin = _rwkv_mixer(proj, b, n, P['rwkv'][l], s0)
    x = _merge(proj, y_s5, y_rw, proj, o_att, gates, x, row(P['s5_d'][l]), P['s5_w_glu_b'][l],
               P['rwkv'][l], P['merge'][l], mod_l, row_of_tile, row(P['norm_post_mix'][l]), tm)
    i = l // 2
    if l % 2 == 0:
        x = _ffn_dense(x, mod_l, row_of_tile, row(P['norm_pre_ffn'][l]), row(P['norm_post_ffn'][l]),
                       P['dense_w1_b'][i], P['dense_w3_b'][i], P['dense_w2_b'][i], tm, FF_DENSE // 2)
    else:
        x = _ffn_moe(x, mod_l, row_of_ptile, row(P['norm_pre_ffn'][l]), row(P['norm_post_ffn'][l]),
                     P['moe_rw'][i], P['moe_rb'][i], P['moe_w1_b'][i], P['moe_w3_b'][i],
                     P['moe_w2_b'][i], tm_proj)
    if cache is None:
        k_new = outs[3][:, DB:2 * DB]
        v_new = outs[3][:, 2 * DB:3 * DB]
        fre, fim = _s5_lanes_to_state(s5fin)
        return x, (k_new, v_new, fre, fim, sfin)
    return x, None


def kernel(x_prompt, x_sample, cache_k, cache_v, state_s5_re, state_s5_im, state_rwkv, c, c_ctx,
           w_ada, b_ada, norm_pre_mix, norm_post_mix, norm_pre_ffn, norm_post_ffn, w_in,
           s5_lam_re, s5_lam_im, s5_log_step, s5_b_re, s5_b_im, s5_c_re, s5_c_im, s5_d, s5_w_glu,
           rwkv_conv, rwkv_w0, rwkv_w_up, rwkv_a0, rwkv_a_up, rwkv_g_up, rwkv_k_k, rwkv_k_a,
           rwkv_u, rwkv_ln_w, rwkv_ln_b, att_rpb, w_br_s5, w_br_rwkv, w_br_att, w_mix_out,
           dense_w1, dense_w3, dense_w2, moe_router_w, moe_router_b, moe_w1, moe_w3, moe_w2):
    bc, nc_, _ = x_prompt.shape
    bl, nl, _ = x_sample.shape
    tm_c = min(512, bc * nc_)
    tm_l = min(512, nl)

    P = dict(norm_pre_mix=norm_pre_mix, norm_post_mix=norm_post_mix, norm_pre_ffn=norm_pre_ffn,
             norm_post_ffn=norm_post_ffn, s5_d=s5_d, rwkv_conv=rwkv_conv, att_rpb=att_rpb)
    P['w_in_b'] = [_reorder_w_in(w_in[l]).astype(BF16) for l in range(DEPTH)]
    P['s5_w_glu_b'] = s5_w_glu.astype(BF16)
    P['dense_w1_b'], P['dense_w3_b'], P['dense_w2_b'] = (dense_w1.astype(BF16), dense_w3.astype(BF16),
                                                         dense_w2.astype(BF16))
    P['moe_w1_b'], P['moe_w3_b'], P['moe_w2_b'] = (moe_w1.astype(BF16), moe_w3.astype(BF16),
                                                   moe_w2.astype(BF16))
    n_moe = moe_router_w.shape[0]
    P['moe_rw'] = jnp.pad(moe_router_w, ((0, 0), (0, 0), (0, LANES - N_EXP)))
    P['moe_rb'] = jnp.pad(moe_router_b, ((0, 0), (0, LANES - N_EXP)),
                          constant_values=NEG).reshape(n_moe, 1, LANES)
    P['merge'] = [dict(br_s5=w_br_s5[l].astype(BF16), br_rw=w_br_rwkv[l].astype(BF16),
                       br_att=w_br_att[l].astype(BF16), mix=w_mix_out[l].astype(BF16))
                  for l in range(DEPTH)]
    P['s5'] = jax.vmap(_s5_params)(s5_lam_re, s5_lam_im, s5_log_step, s5_b_re, s5_b_im, s5_c_re,
                                   s5_c_im)
    P['na_bias'] = _na_bias_tables(att_rpb)
    P['rwkv'] = []
    for l in range(DEPTH):
        P['rwkv'].append(dict(
            w0=rwkv_w0[l].reshape(2, 1, DB), w_up=rwkv_w_up[l], a0=rwkv_a0[l].reshape(2, 1, DB),
            a_up=rwkv_a_up[l], k_k=rwkv_k_k[l].reshape(1, DB), k_a=rwkv_k_a[l].reshape(1, DB),
            g_up=rwkv_g_up[l].astype(BF16), u=rwkv_u[l].reshape(1, DB),
            ln_w=rwkv_ln_w[l].reshape(1, DB), ln_b=rwkv_ln_b[l].reshape(1, DB)))

    cvec = jnp.zeros((8, D), F32).at[0].set(c_ctx).at[1:1 + bl].set(c)
    mod = _modulation(cvec, w_ada, b_ada)

    xp = x_prompt.reshape(bc * nc_, D)
    ks, vs, s5re, s5im, rws = [], [], [], [], []
    for l in range(DEPTH):
        xp, (k_n, v_n, fre, fim, sfin) = _layer(xp, bc, nc_, l, mod[l], lambda i: 0, P, None, tm_c)
        ks.append(k_n.reshape(bc, nc_, HEADS, DH))
        vs.append(v_n.reshape(bc, nc_, HEADS, DH))
        s5re.append(fre)
        s5im.append(fim)
        rws.append(_pairs_to_state(sfin))
    new_k = jnp.stack(ks, axis=1)
    new_v = jnp.stack(vs, axis=1)
    new_s5_re = jnp.stack(s5re, axis=1)
    new_s5_im = jnp.stack(s5im, axis=1)
    new_rwkv = jnp.stack(rws, axis=1)

    xs = x_sample.reshape(bl * nl, D)
    tiles_per_seq = nl // tm_l
    for l in range(DEPTH):
        cache = (cache_k[:, l], cache_v[:, l], state_s5_re[:, l], state_s5_im[:, l], state_rwkv[:, l])
        xs, _ = _layer(xs, bl, nl, l, mod[l], lambda i: 1 + i // tiles_per_seq, P, cache, tm_l)

    return (xp.reshape(bc, nc_, D), xs.reshape(bl, nl, D), new_k, new_v, new_s5_re, new_s5_im,
            new_rwkv)
```

```python
import functools

import numpy as np
import jax
import jax.numpy as jnp
from jax import lax
from jax.experimental import pallas as pl
from jax.experimental.pallas import tpu as pltpu

F32 = jnp.float32
BF16 = jnp.bfloat16
HIGHEST = lax.Precision.HIGHEST

D = 1024
DEPTH = 2
GRID_W = 64
DH = 64
HEADS = 8
DB = 512
WIN_ROWS = 8
WIN_COLS = 16
S5_CH = 16
S5_G = 32
S5_P = 64
S5_J = 8
S5_UG = 8
FF_DENSE = 2816
N_EXP = 8
FF_EXP = 1024
EPS = 1e-6
GN_EPS = 64e-5
NEG = -1e30

D_IN = 6912
TN_PROJ = 768
J_MAIN = 4
J_QKV = 7
J_RKV_TILES = 2
W_MAIN = (J_QKV - J_MAIN) * TN_PROJ
C_RKV = 0
C_U = 1536
C_LORA = 2048
W_LORA, A_LORA, G_LORA = 64, 64, 128
LORA_W = W_LORA + A_LORA + G_LORA
SUBLANES = 8
HALO = 2 * SUBLANES

LANES = 128
RW_L = 64
NA_RB = 4
NA_KR = 12
VMEM_LIMIT = 56 * 1024 * 1024


def _cparams(n_axes, vmem=VMEM_LIMIT):
    return pltpu.CompilerParams(dimension_semantics=("arbitrary",) * n_axes,
                                vmem_limit_bytes=vmem)


def _dot(a, b):
    return jnp.dot(a, b, preferred_element_type=F32)


def _dot_nt(a, b):
    return lax.dot_general(a, b, (((1,), (1,)), ((), ())), preferred_element_type=F32)


def _dot_tn(a, b):
    return lax.dot_general(a, b, (((0,), (0,)), ((), ())), preferred_element_type=F32)


def _split2(x):
    hi = x.astype(BF16)
    lo = (x - hi.astype(F32)).astype(BF16)
    return hi, lo


def _dot_rhs_exact(x, m):
    hi, lo = _split2(x)
    return _dot(hi, m) + _dot(lo, m)


def _dot_lhs_exact(m, x):
    hi, lo = _split2(x)
    return _dot(m, hi) + _dot(m, lo)


def _mm3(a, b):
    a1, a2 = _split2(a)
    b1, b2 = _split2(b)
    return _dot(a1, b1) + _dot(a1, b2) + _dot(a2, b1)


def _mm(a, b):
    return _dot(a.astype(BF16), b.astype(BF16))


def _sigmoid(x):
    return 0.5 * (jnp.tanh(0.5 * x) + 1.0)


def _silu(x):
    return x * _sigmoid(x)


def _softplus(x):
    return jnp.maximum(x, 0.0) + jnp.log(1.0 + jnp.exp(-jnp.abs(x)))


def _rms(x, g):
    return x * lax.rsqrt(jnp.mean(x * x, axis=-1, keepdims=True) + EPS) * g


def _head_block_matrix(width, value):
    r = lax.broadcasted_iota(jnp.int32, (width, width), 0) // DH
    c = lax.broadcasted_iota(jnp.int32, (width, width), 1) // DH
    return jnp.where(r == c, value, 0.0).astype(BF16)


def _mod_kernel(c_ref, w_ref, b_ref, o_ref):
    s = _silu(c_ref[...]).astype(BF16)
    o_ref[...] = _dot(s, w_ref[...].astype(BF16)) + b_ref[...]


def _modulation(cvec, w_ada, b_ada):
    tn = 1536
    out = pl.pallas_call(
        _mod_kernel,
        grid=(DEPTH, 6 * D // tn),
        in_specs=[pl.BlockSpec((8, D), lambda l, j: (0, 0)),
                  pl.BlockSpec((None, D, tn), lambda l, j: (l, 0, j)),
                  pl.BlockSpec((None, 1, tn), lambda l, j: (l, 0, j))],
        out_specs=pl.BlockSpec((None, 8, tn), lambda l, j: (l, 0, j)),
        out_shape=jax.ShapeDtypeStruct((DEPTH, 8, 6 * D), F32),
        compiler_params=_cparams(2),
    )(cvec, w_ada, b_ada.reshape(DEPTH, 1, 6 * D))
    return out.reshape(DEPTH, 8, 1, 6 * D)


def _mod_spec(part, row_of_tile, n_grid):
    if n_grid == 1:
        return pl.BlockSpec((None, 1, D), lambda i: (row_of_tile(i), 0, part))
    return pl.BlockSpec((None, 1, D), lambda i, j: (row_of_tile(i), 0, part))


def _proj_kernel(x_ref, xp_ref, xn_ref, g_ref, sc_ref, sh_ref, w_ref, cw_ref, og_ref, om_ref, oq_ref,
                 *rest, f32_qkv, n):
    h_scr = rest[-1]
    i = pl.program_id(0)
    j = pl.program_id(1)
    tm = x_ref.shape[0]

    @pl.when(j == 0)
    def _():
        norm = lambda x: (_rms(x, g_ref[...]) * (1.0 + sc_ref[...]) + sh_ref[...]).astype(BF16)
        h_scr[0:tm, :] = norm(x_ref[...])
        h_scr[tm:tm + SUBLANES, :] = norm(xp_ref[...])
        h_scr[tm + SUBLANES:tm + HALO, :] = norm(xn_ref[...])

    acc_all = _dot(h_scr[...], w_ref[...])
    acc = acc_all[0:tm]

    @pl.when(j < J_MAIN)
    def _():
        og_ref[...] = acc.astype(BF16)

    @pl.when((j >= J_MAIN) & (j < J_MAIN + J_RKV_TILES))
    def _():
        row = lax.broadcasted_iota(jnp.int32, (tm, 1), 0)
        prev_row = acc_all[tm + SUBLANES - 1:tm + SUBLANES]
        next_row = acc_all[tm + SUBLANES:tm + SUBLANES + 1]
        x_dn = pltpu.roll(acc, 1, axis=0)
        x_up = pltpu.roll(acc, tm - 1, axis=0)
        if n >= tm:
            tiles = n // tm
            prev_row = jnp.where(i % tiles == 0, 0.0, prev_row)
            next_row = jnp.where(i % tiles == tiles - 1, 0.0, next_row)
            x_dn = jnp.where(row == 0, prev_row, x_dn)
            x_up = jnp.where(row == tm - 1, next_row, x_up)
        else:
            x_dn = jnp.where(row % n == 0, 0.0, x_dn)
            x_up = jnp.where(row % n == n - 1, 0.0, x_up)
        om_ref[...] = x_dn * cw_ref[0:1, :] + acc * cw_ref[1:2, :] + x_up * cw_ref[2:3, :]

    @pl.when((j >= J_MAIN + J_RKV_TILES) & (j < J_QKV))
    def _():
        om_ref[...] = acc

    @pl.when(j >= J_QKV)
    def _():
        oq_ref[...] = acc.astype(BF16)
        if f32_qkv:
            rest[0][...] = acc


def _projection(x, mod_l, row_of_tile, g_pre, w_in_b, conv_w, n, tm, f32_qkv):
    t = x.shape[0]
    n_tiles = D_IN // TN_PROJ
    n_halo_blocks = t // SUBLANES
    per_tile = tm // SUBLANES
    tile = lambda j0, j1: pl.BlockSpec((tm, TN_PROJ), lambda i, j: (i, jnp.clip(j - j0, 0, j1 - j0 - 1)))
    out_specs = [tile(0, J_MAIN), tile(J_MAIN, J_QKV), tile(J_QKV, n_tiles)]
    out_shape = [jax.ShapeDtypeStruct((t, 3 * D), BF16), jax.ShapeDtypeStruct((t, W_MAIN), F32),
                 jax.ShapeDtypeStruct((t, 3 * DB), BF16)]
    if f32_qkv:
        out_specs.append(tile(J_QKV, n_tiles))
        out_shape.append(jax.ShapeDtypeStruct((t, 3 * DB), F32))
    return pl.pallas_call(
        functools.partial(_proj_kernel, f32_qkv=f32_qkv, n=n),
        grid=(t // tm, n_tiles),
        in_specs=[pl.BlockSpec((tm, D), lambda i, j: (i, 0)),
                  pl.BlockSpec((SUBLANES, D), lambda i, j: (jnp.maximum(i * per_tile - 1, 0), 0)),
                  pl.BlockSpec((SUBLANES, D),
                               lambda i, j: (jnp.minimum((i + 1) * per_tile, n_halo_blocks - 1), 0)),
                  pl.BlockSpec((1, D), lambda i, j: (0, 0)),
                  _mod_spec(1, row_of_tile, 2),
                  _mod_spec(0, row_of_tile, 2),
                  pl.BlockSpec((D, TN_PROJ), lambda i, j: (0, j)),
                  pl.BlockSpec((3, TN_PROJ), lambda i, j: (0, jnp.clip(j - J_MAIN, 0, J_RKV_TILES - 1)))],
        out_specs=out_specs,
        out_shape=out_shape,
        scratch_shapes=[pltpu.VMEM((tm + HALO, D), BF16)],
        compiler_params=_cparams(2),
    )(x, x, x, g_pre, mod_l, mod_l, w_in_b, conv_w)


def _pair_masks():
    lane = lax.broadcasted_iota(jnp.int32, (1, LANES), 1)
    return lane < DH


def _softmax_pv(parts):
    m = None
    for s, _ in parts:
        mx = jnp.max(s, axis=-1, keepdims=True)
        m = mx if m is None else jnp.maximum(m, mx)
    es = [jnp.exp(s - m) for s, _ in parts]
    den = None
    for e in es:
        sm = jnp.sum(e, axis=-1, keepdims=True)
        den = sm if den is None else den + sm
    out = None
    for e, (_, v) in zip(es, parts):
        o = _dot(e.astype(BF16), v)
        out = o if out is None else out + o
    return out * (1.0 / den)


def _ctx_att_kernel(q_ref, k_ref, v_ref, o_ref):
    scale = DH ** -0.5
    m0 = _pair_masks()
    for p in range(HEADS // 2):
        sl = slice(p * LANES, (p + 1) * LANES)
        qp, kp, vp = q_ref[:, sl] * scale, k_ref[:, sl], v_ref[:, sl]
        o_pair = None
        for hh in range(2):
            msk = m0 if hh == 0 else jnp.logical_not(m0)
            qm = jnp.where(msk, qp, jnp.zeros_like(qp))
            o = _softmax_pv([(_dot_nt(qm, kp), vp)])
            o_pair = o if o_pair is None else jnp.where(m0, o_pair, o)
        o_ref[:, sl] = o_pair


def _ctx_attention(qkv, b, n):
    return pl.pallas_call(
        _ctx_att_kernel,
        grid=(b,),
        in_specs=[pl.BlockSpec((n, DB), lambda i: (i, 0)),
                  pl.BlockSpec((n, DB), lambda i: (i, 1)),
                  pl.BlockSpec((n, DB), lambda i: (i, 2))],
        out_specs=pl.BlockSpec((n, DB), lambda i: (i, 0)),
        out_shape=jax.ShapeDtypeStruct((b * n, DB), F32),
        compiler_params=_cparams(1),
    )(qkv, qkv, qkv)


def _na_key_start(rb, rows):
    return jnp.clip(rb * NA_RB - WIN_ROWS // 2, 0, rows - NA_KR)


def _na_kernel(q_ref, k_ref, v_ref, kc_ref, vc_ref, tp_ref, rm_ref, o_ref, *, rows):
    scale = DH ** -0.5
    m0 = _pair_masks()
    rb = pl.program_id(1)
    u0 = _na_key_start(rb, rows)
    start = pl.multiple_of(u0 * GRID_W, GRID_W)
    nk = NA_KR * GRID_W
    rel = [[jnp.clip(u0 + 2 * m - (rb * NA_RB + qr) + WIN_ROWS, 0, 2 * WIN_ROWS)
            for m in range(NA_KR // 2)] for qr in range(NA_RB)]
    row_mask = rm_ref[...]

    def bias(h):
        return jnp.concatenate(
            [jnp.concatenate([tp_ref[h, rel[qr][m]] for m in range(NA_KR // 2)], axis=1)
             for qr in range(NA_RB)], axis=0) + row_mask

    for p in range(HEADS // 2):
        sl = slice(p * LANES, (p + 1) * LANES)
        qp = q_ref[:, sl] * scale
        kp = k_ref[pl.ds(start, nk), sl]
        vp = v_ref[pl.ds(start, nk), sl]
        kcp, vcp = kc_ref[:, sl], vc_ref[:, sl]
        o_pair = None
        for hh in range(2):
            msk = m0 if hh == 0 else jnp.logical_not(m0)
            qm = jnp.where(msk, qp, jnp.zeros_like(qp))
            s_loc = _dot_nt(qm, kp) + bias(2 * p + hh)
            s_ctx = _dot_nt(qm, kcp)
            o = _softmax_pv([(s_loc, vp), (s_ctx, vcp)])
            o_pair = o if o_pair is None else jnp.where(m0, o_pair, o)
        o_ref[:, sl] = o_pair


def _na_bias_tables(rpb):
    nl = rpb.shape[0]
    nrel_r, nrel_c = 2 * WIN_ROWS - 1, 2 * WIN_COLS - 1
    qc = np.arange(GRID_W)[:, None]
    kc = np.arange(GRID_W)[None, :]
    cs = np.clip(qc - WIN_COLS // 2, 0, GRID_W - WIN_COLS)
    col_ok = (kc >= cs) & (kc < cs + WIN_COLS)
    col_rel = np.clip(kc - qc + (WIN_COLS - 1), 0, nrel_c - 1)
    col_hot = (np.arange(nrel_c)[:, None, None] == col_rel[None]) & col_ok[None]
    col_hot = jnp.asarray(col_hot.reshape(nrel_c, GRID_W * GRID_W), F32)
    tb = jnp.dot(rpb.reshape(-1, nrel_c), col_hot, precision=HIGHEST)
    tb = jnp.where(jnp.asarray(col_ok.reshape(1, -1)), tb, NEG)
    tb = tb.reshape(nl, HEADS, nrel_r, GRID_W, GRID_W)
    tb = jnp.pad(tb, ((0, 0), (0, 0), (1, 2), (0, 0), (0, 0)))
    return jnp.concatenate([tb[:, :, :-1], tb[:, :, 1:]], axis=-1)


def _na_row_masks(rows):
    n_rb = rows // NA_RB
    wr = min(WIN_ROWS, rows)
    out = np.full((3, NA_RB, GRID_W, NA_KR, GRID_W), NEG, np.float32)
    for vi, rb in enumerate((0, 1, n_rb - 1)):
        u0 = int(np.clip(rb * NA_RB - WIN_ROWS // 2, 0, rows - NA_KR))
        for qr in range(NA_RB):
            rs = int(np.clip(rb * NA_RB + qr - wr // 2, 0, rows - wr))
            for kr in range(NA_KR):
                if rs <= u0 + kr < rs + wr:
                    out[vi, qr, :, kr, :] = 0.0
    return jnp.asarray(out.reshape(3, NA_RB * GRID_W, NA_KR * GRID_W))


def _na_attention(qkv, k_ctx, v_ctx, bias, row_masks, l, b, n):
    rows = n // GRID_W
    n_rb = rows // NA_RB
    tq = NA_RB * GRID_W
    lc = k_ctx.shape[0] // b

    def variant(j):
        return jnp.where(j == 0, 0, jnp.where(j == n_rb - 1, 2, 1))

    return pl.pallas_call(
        functools.partial(_na_kernel, rows=rows),
        grid=(b, n_rb),
        in_specs=[pl.BlockSpec((tq, DB), lambda i, j: (i * n_rb + j, 0)),
                  pl.BlockSpec((n, DB), lambda i, j: (i, 1)),
                  pl.BlockSpec((n, DB), lambda i, j: (i, 2)),
                  pl.BlockSpec((lc, DB), lambda i, j: (i, 0)),
                  pl.BlockSpec((lc, DB), lambda i, j: (i, 0)),
                  pl.BlockSpec((None, HEADS, 2 * WIN_ROWS + 1, GRID_W, 2 * GRID_W),
                               lambda i, j: (l, 0, 0, 0, 0)),
                  pl.BlockSpec((None, tq, NA_KR * GRID_W), lambda i, j: (variant(j), 0, 0))],
        out_specs=pl.BlockSpec((tq, DB), lambda i, j: (i * n_rb + j, 0)),
        out_shape=jax.ShapeDtypeStruct((b * n, DB), F32),
        compiler_params=_cparams(2),
    )(qkv, qkv, qkv, k_ctx, v_ctx, bias, row_masks)


def _cmul(a, b):
    return a[0] * b[0] - a[1] * b[1], a[0] * b[1] + a[1] * b[0]


def _cexp(re, im):
    e = jnp.exp(re)
    return e * jnp.cos(im), e * jnp.sin(im)


def _s5_params(lam_re, lam_im, log_step, b_re, b_im, c_re, c_im):
    hp = dict(precision=HIGHEST)
    J, nq, ug = S5_J, S5_G // S5_UG, S5_UG
    sw = 2 * ug * S5_P

    def block_diag(x2d, row_group, col_inner):
        rows = x2d.shape[-2]
        ci = np.arange(sw)
        src = (ci // (ug * col_inner)) * col_inner + ci % col_inner
        expand = jnp.asarray(np.arange(LANES)[:, None] == src[None, :], F32)
        keep = jnp.asarray(row_group(np.arange(rows))[:, None] == ((ci // col_inner) % ug)[None, :])
        return jnp.where(keep, jnp.matmul(x2d, expand, precision=HIGHEST), 0.0)

    grp16 = lambda r: (r // S5_CH) % ug
    grp64 = lambda r: (r // S5_P) % ug
    step = jnp.exp(log_step)[..., None]
    lam_bar = _cexp(lam_re * step, lam_im * step)
    den = lam_re * lam_re + lam_im * lam_im
    num = (lam_bar[0] - 1.0, lam_bar[1])
    coef = ((num[0] * lam_re + num[1] * lam_im) / den, (num[1] * lam_re - num[0] * lam_im) / den)
    b_bar = _cmul((coef[0][..., None], coef[1][..., None]), (b_re, b_im))
    dd = jnp.arange(J + 1, dtype=F32)[:, None, None, None]
    lam_pow = _cexp(dd * (lam_re * step)[None], dd * (lam_im * step)[None])

    def units(x, g_axis):
        return x.reshape(x.shape[:g_axis] + (nq, ug) + x.shape[g_axis + 1:])

    sel = [jnp.stack([c[:J][::-1, 0], c[:J][:, 1]], axis=0) for c in lam_pow]
    wb = _cmul((sel[0][..., None], sel[1][..., None]), (b_bar[0][:, None], b_bar[1][:, None]))
    wb = units(jnp.stack(wb, axis=0), 3)
    bcat = block_diag(jnp.transpose(wb, (3, 1, 2, 4, 6, 0, 5)).reshape(nq, 2, J * LANES, LANES),
                      grp16, S5_P)
    lp = [jnp.transpose(c[:J], (1, 2, 0, 3))[:, :, :, None, :] for c in lam_pow]
    cl = _cmul((c_re[:, :, None], c_im[:, :, None]), lp)
    kd = (jnp.einsum('dgjcp,dgpk->dgjck', cl[0], b_bar[0], **hp)
          - jnp.einsum('dgjcp,dgpk->dgjck', cl[1], b_bar[1], **hp))
    oi = np.arange(J)[:, None]
    oo = np.arange(J)[None, :]
    lag_hot = np.stack([(oo - oi)[..., None] == np.arange(J), (oi - oo)[..., None] == np.arange(J)])
    t5 = jnp.einsum('dioj,dgjce->dgioce', jnp.asarray(lag_hot, F32), kd, **hp)
    t5 = units(t5[0] + t5[1], 0)
    tsum = block_diag(jnp.transpose(t5, (0, 2, 1, 5, 3, 4)).reshape(nq, J * LANES, LANES),
                      grp16, S5_CH)
    lq = [jnp.transpose(c[1:], (1, 2, 0, 3))[:, :, :, None, :] for c in lam_pow]
    cm = _cmul((c_re[:, :, None], c_im[:, :, None]), lq)
    cc = jnp.stack([cm[0], -cm[1]], axis=0)
    cc = jnp.stack([cc[:, 0], cc[:, 1, :, ::-1]], axis=1)
    ccat = block_diag(jnp.transpose(units(cc, 2), (2, 1, 0, 3, 6, 4, 5)).reshape(nq, 2 * sw, LANES),
                      grp64, S5_CH)
    b1 = jnp.transpose(units(jnp.stack(b_bar, axis=0), 2), (2, 1, 3, 5, 0, 4))
    b1 = block_diag(b1.reshape(nq, 2, LANES, LANES), grp16, S5_P)

    def lanes(c):
        x = jnp.concatenate([c[0].reshape(2, nq, ug * S5_P), c[1].reshape(2, nq, ug * S5_P)], axis=-1)
        return jnp.transpose(x, (1, 0, 2))[:, :, None, :]

    return dict(bcat=bcat.astype(BF16), tsum=tsum.astype(BF16), ccat=ccat.astype(BF16), b1=b1,
                lj=lanes((lam_pow[0][J], lam_pow[1][J])), l1=lanes(lam_bar))


def _s5_kernel(u_ref, bcat_ref, tsum_ref, ccat_ref, lj_ref, l1_ref, b1_ref, x0_ref,
               y_ref, fin_ref, z_scr, *, nseq, n):
    J = S5_J
    nsub = n // J
    ns = nseq * nsub
    hw = S5_UG * S5_P
    if nseq == 1:
        ucat = jnp.concatenate([u_ref[pl.ds(o, ns, stride=J), :] for o in range(J)], axis=1)
    else:
        ucat = jnp.concatenate(
            [jnp.concatenate([u_ref[pl.ds(j * J + o, nseq, stride=n), :] for o in range(J)], axis=1)
             for j in range(nsub)], axis=0)
    ucat = ucat.astype(BF16)
    for d in range(2):
        z_scr[d] = _dot(ucat, bcat_ref[d])

    def cstep(d, x, z):
        lre, lim = lj_ref[d, :, 0:hw], lj_ref[d, :, hw:]
        return jnp.concatenate([lre * x[:, :hw] - lim * x[:, hw:] + z[:, :hw],
                                lre * x[:, hw:] + lim * x[:, :hw] + z[:, hw:]], axis=1)

    for d in range(2):
        tok = 0 if d == 0 else n - 1
        uf = u_ref[pl.ds(tok, nseq, stride=n), :] if nseq > 1 else u_ref[tok:tok + 1, :]
        bu = jnp.dot(uf, b1_ref[d], precision=HIGHEST, preferred_element_type=F32)
        x0 = x0_ref[d]
        lre, lim = l1_ref[d, :, 0:hw], l1_ref[d, :, hw:]
        fin_ref[d] = jnp.concatenate([lre * x0[:, :hw] - lim * x0[:, hw:] + bu[:, :hw],
                                      lre * x0[:, hw:] + lim * x0[:, :hw] + bu[:, hw:]], axis=1)

    if nseq == 1:
        def step(j, carry):
            out = []
            for d in range(2):
                row = pl.ds(j if d == 0 else nsub - 1 - j, 1)
                z = z_scr[d, row, :]
                z_scr[d, row, :] = carry[d]
                out.append(cstep(d, carry[d], z))
            return tuple(out)

        lax.fori_loop(0, nsub, step, (x0_ref[0], x0_ref[1]), unroll=4)
    else:
        xs = [x0_ref[0], x0_ref[1]]
        for j in range(nsub):
            for d in range(2):
                rows = pl.ds((j if d == 0 else nsub - 1 - j) * nseq, nseq)
                z = z_scr[d, rows, :]
                z_scr[d, rows, :] = xs[d]
                xs[d] = cstep(d, xs[d], z)

    xin = jnp.concatenate([z_scr[0], z_scr[1]], axis=1).astype(BF16)
    ycat = _dot(ucat, tsum_ref[...]) + _dot(xin, ccat_ref[...])
    for o in range(J):
        if nseq == 1:
            y_ref[pl.ds(o, ns, stride=J), :] = ycat[:, o * LANES:(o + 1) * LANES]
        else:
            for j in range(nsub):
                y_ref[pl.ds(j * J + o, nseq, stride=n), :] = ycat[j * nseq:(j + 1) * nseq,
                                                                  o * LANES:(o + 1) * LANES]


def _s5_scan(proj, b, n, prm, l, x0, nseq):
    nq = S5_G // S5_UG
    sw = 2 * S5_UG * S5_P
    jl = S5_J * LANES
    r = nseq * n
    ns = r // S5_J
    if nseq == 1:
        x0 = x0.reshape(nq, 2, b, 1, sw)
        st_spec = pl.BlockSpec((None, 2, None, 1, sw), lambda q, i: (q, 0, i, 0, 0))
    else:
        st_spec = pl.BlockSpec((None, 2, nseq, sw), lambda q, i: (q, 0, i, 0))
    wspec = lambda *shp: pl.BlockSpec((None, None) + shp, lambda q, i: (l, q) + (0,) * len(shp))
    y, fin = pl.pallas_call(
        functools.partial(_s5_kernel, nseq=nseq, n=n),
        grid=(nq, b // nseq),
        in_specs=[pl.BlockSpec((r, LANES), lambda q, i: (i, C_U // LANES + q)),
                  wspec(2, jl, sw), wspec(jl, jl), wspec(2 * sw, jl), wspec(2, 1, sw),
                  wspec(2, 1, sw), wspec(2, LANES, sw), st_spec],
        out_specs=[pl.BlockSpec((r, LANES), lambda q, i: (i, q)), st_spec],
        out_shape=[jax.ShapeDtypeStruct((b * n, DB), F32), jax.ShapeDtypeStruct(x0.shape, F32)],
        scratch_shapes=[pltpu.VMEM((2, ns, sw), F32)],
        compiler_params=_cparams(2),
    )(proj, prm['bcat'], prm['tsum'], prm['ccat'], prm['lj'], prm['l1'], prm['b1'], x0)
    return y, fin.reshape(nq, 2, b, sw)


def _s5_post(u, y_scan, d_skip, w_glu):
    y = jax.nn.gelu(d_skip * u + y_scan)
    return y * _sigmoid(_dot(y.astype(BF16), w_glu))


def _s5_state_to_lanes(s_re, s_im):
    b = s_re.shape[0]
    nq = S5_G // S5_UG
    x = jnp.concatenate([s_re.reshape(b, 2, nq, S5_UG * S5_P), s_im.reshape(b, 2, nq, S5_UG * S5_P)],
                        axis=-1)
    return jnp.transpose(x, (2, 1, 0, 3))


def _s5_lanes_to_state(x):
    nq, _, b, _ = x.shape
    hw = S5_UG * S5_P
    x = jnp.transpose(x, (2, 1, 0, 3))
    return x[..., :hw].reshape(b, 2, S5_G, S5_P), x[..., hw:].reshape(b, 2, S5_G, S5_P)


def _s5_mixer(proj, b, n, sp, l, x0):
    nseq = 8 if (n // S5_J <= 64 and b % 8 == 0) else 1
    return _s5_scan(proj, b, n, sp, l, x0, nseq)


def _rwkv_chunk_kernel(rc_ref, kc_ref, vc_ref, lora_ref, w0_ref, wup_ref, a0_ref, aup_ref,
                       kk_ref, ka_ref, s0_ref, y_ref, sfin_ref, z_scr, *, nc, nseq):
    d = pl.program_id(1)
    ci = pl.program_id(2)
    L = RW_L
    rows = nseq * L

    @pl.when(ci == 0)
    def _():
        z_scr[...] = s0_ref[...]

    flat = lambda ref: ref[...].reshape(rows, ref.shape[-1])
    rc, kc, vc, lora = flat(rc_ref), flat(kc_ref), flat(vc_ref), flat(lora_ref)
    wd = lora[:, 0:W_LORA]
    ad = lora[:, W_LORA:W_LORA + A_LORA]
    lora_w = _mm3(jnp.tanh(wd), wup_ref[...])
    log_w = -_softplus(-(w0_ref[...] + lora_w)) - 0.5
    lw = -jnp.exp(log_w)
    a = _sigmoid(a0_ref[...] + _mm3(ad, aup_ref[...]))
    kd = kc * (1.0 + (a - 1.0) * ka_ref[...])
    kk = kc * kk_ref[...]
    kk = kk * lax.rsqrt(_mm(kk * kk, _head_block_matrix(DB, 1.0)) + 1e-12)
    alpha = -kk
    beta = kk * a

    sgn = 1 - 2 * d
    tt = lax.broadcasted_iota(jnp.int32, (rows, rows), 0)
    ss = lax.broadcasted_iota(jnp.int32, (rows, rows), 1)
    tri = jnp.where(((tt % L - ss % L) * sgn >= 0) & (tt // L == ss // L), 1.0, 0.0).astype(BF16)
    c = _dot_lhs_exact(tri, lw)
    c_ex = c - lw
    ctot = jnp.concatenate(
        [jnp.broadcast_to(jnp.sum(lw[s * L:(s + 1) * L], axis=0, keepdims=True), (L, DB))
         for s in range(nseq)], axis=0)
    mid = 0.5 * ctot
    e_in = jnp.exp(c - mid)
    e_ex = jnp.exp(c_ex - mid)
    e_out = jnp.exp(mid - c)
    e_mid = jnp.exp(mid)
    al_t = alpha * e_ex
    r_t = rc * e_in
    be_t = beta * e_out
    k_t = kd * e_out
    a0s = al_t * e_mid
    r0s = r_t * e_mid
    bps = be_t * e_mid
    kps = k_t * e_mid
    p_l = e_mid * e_mid

    m0 = _pair_masks()
    t2 = lax.broadcasted_iota(jnp.int32, (2 * L, 2 * L), 0) % L
    s2 = lax.broadcasted_iota(jnp.int32, (2 * L, 2 * L), 1) % L
    strict = (t2 - s2) * sgn > 0
    incl = (t2 - s2) * sgn >= 0
    eye = (lax.broadcasted_iota(jnp.int32, (LANES, LANES), 0)
           == lax.broadcasted_iota(jnp.int32, (LANES, LANES), 1))
    eye_f = jnp.where(eye, 1.0, 0.0)
    zeros = jnp.zeros((LANES, LANES), F32)

    chains = [(s, p) for s in range(nseq) for p in range(HEADS // 2)]
    pairs = range(len(chains))
    rws = [slice(s * L, (s + 1) * L) for s, _ in chains]
    sls = [slice(p * LANES, (p + 1) * LANES) for _, p in chains]

    def stack(x, p):
        xp = x[rws[p], sls[p]]
        return jnp.concatenate([jnp.where(m0, xp, 0.0), jnp.where(m0, 0.0, xp)], axis=0)

    raws = [_dot_nt(jnp.concatenate([stack(al_t, p), stack(r_t, p)], axis=0).astype(BF16),
                    jnp.concatenate([stack(be_t, p), stack(k_t, p)], axis=0).astype(BF16))
            for p in pairs]
    amat = [jnp.where(strict, r[:LANES, :LANES], 0.0) for r in raws]
    bmat = [jnp.where(strict, r[:LANES, LANES:], 0.0) for r in raws]
    qbk = [jnp.concatenate([jnp.where(incl, r[LANES:, :LANES], 0.0),
                            jnp.where(incl, r[LANES:, LANES:], 0.0)], axis=1) for r in raws]

    smat = [eye_f + a for a in amat]
    pw = [_mm(a, a) for a in amat]
    for _ in range(4):
        xs = [_mm(pw[p], jnp.concatenate([pw[p], smat[p]], axis=1)) for p in pairs]
        smat = [smat[p] + xs[p][:, LANES:] for p in pairs]
        pw = [x[:, :LANES] for x in xs]
    tmat = [smat[p] + _mm(pw[p], smat[p]) for p in pairs]

    vs = [stack(vc, p) for p in pairs]
    bv = [_mm(bmat[p], vs[p]) for p in pairs]
    wu = [_mm(tmat[p], jnp.concatenate([stack(a0s, p), bv[p]], axis=1)) for p in pairs]
    rhs2 = [jnp.concatenate([wu[p], jnp.concatenate([zeros, vs[p]], axis=1)], axis=0).astype(BF16)
            for p in pairs]
    out_a = [_dot(qbk[p].astype(BF16), rhs2[p]) for p in pairs]
    out_b = [_dot_tn(jnp.concatenate([stack(bps, p), stack(kps, p)], axis=0).astype(BF16), rhs2[p])
             for p in pairs]
    hz = [_mm(jnp.concatenate(
        [stack(r0s, p) + out_a[p][:, :LANES],
         jnp.where(eye, p_l[rws[p], sls[p]][0:1], 0.0) + out_b[p][:, :LANES]], axis=0), z_scr[chains[p]])
          for p in pairs]
    for p in pairs:
        y = out_a[p][:, LANES:] + hz[p][:LANES]
        z_scr[chains[p]] = hz[p][LANES:] + out_b[p][:, LANES:]
        y_ref[chains[p][0], :, sls[p]] = y[:L] + y[L:]

    @pl.when(ci == nc - 1)
    def _():
        sfin_ref[...] = z_scr[...]


def _rwkv_post(y, rkv_ref, gd, g_up, u_bonus, ln_w, ln_b):
    avg = _head_block_matrix(DB, 1.0 / DH)
    mu = _dot_rhs_exact(y, avg)
    yc = y - mu
    var = _dot_rhs_exact(yc * yc, avg)
    yn = yc * lax.rsqrt(var + GN_EPS) * ln_w + ln_b
    rc, kc, vc = rkv_ref[:, 0:DB], rkv_ref[:, DB:2 * DB], rkv_ref[:, 2 * DB:3 * DB]
    bonus = _mm(rc * kc * u_bonus, _head_block_matrix(DB, 1.0)) * vc
    g = _dot(_sigmoid(gd).astype(BF16), g_up)
    return (yn + bonus) * g


def _rwkv_mixer(proj, b, n, prm, s0_pairs):
    t = b * n
    nc = n // RW_L
    nseq = 4 if b % 4 == 0 else (2 if b % 2 == 0 else 1)
    hp = HEADS // 2

    def chunk(dd, ci):
        return jnp.where(dd == 0, ci, nc - 1 - ci)

    tok = lambda w, col: pl.BlockSpec((nseq, RW_L, w), lambda i, dd, ci: (i, chunk(dd, ci), col))
    dirp = lambda r: pl.BlockSpec((None, r, DB), lambda i, dd, ci: (dd, 0, 0))
    shared = pl.BlockSpec((1, DB), lambda i, dd, ci: (0, 0))
    state = pl.BlockSpec((nseq, None, hp, LANES, LANES), lambda i, dd, ci: (i, dd, 0, 0, 0))
    main3 = proj.reshape(b, n, W_MAIN)
    y2, sfin = pl.pallas_call(
        functools.partial(_rwkv_chunk_kernel, nc=nc, nseq=nseq),
        grid=(b // nseq, 2, nc),
        in_specs=[tok(DB, C_RKV // DB), tok(DB, C_RKV // DB + 1), tok(DB, C_RKV // DB + 2),
                  tok(LORA_W, C_LORA // LORA_W),
                  dirp(1), dirp(W_LORA), dirp(1), dirp(A_LORA), shared, shared, state],
        out_specs=[pl.BlockSpec((None, nseq, RW_L, DB), lambda i, dd, ci: (dd, i, chunk(dd, ci), 0)),
                   state],
        out_shape=[jax.ShapeDtypeStruct((2, b, n, DB), F32),
                   jax.ShapeDtypeStruct((b, 2, hp, LANES, LANES), F32)],
        scratch_shapes=[pltpu.VMEM((nseq, hp, LANES, LANES), F32)],
        compiler_params=_cparams(3),
    )(main3, main3, main3, main3, prm['w0'], prm['w_up'], prm['a0'], prm['a_up'],
      prm['k_k'], prm['k_a'], s0_pairs)
    return y2.reshape(2, t, DB), sfin


def _state_to_pairs(s):
    b = s.shape[0]
    st = jnp.swapaxes(s, -1, -2).reshape(b, 2, HEADS // 2, 2, DH, DH)
    z = jnp.zeros((b, 2, HEADS // 2, 2, DH, 2, DH), F32)
    z = z.at[:, :, :, 0, :, 0, :].set(st[:, :, :, 0])
    z = z.at[:, :, :, 1, :, 1, :].set(st[:, :, :, 1])
    return z.reshape(b, 2, HEADS // 2, LANES, LANES)


def _pairs_to_state(z):
    b = z.shape[0]
    z = z.reshape(b, 2, HEADS // 2, 2, DH, 2, DH)
    st = jnp.stack([z[:, :, :, 0, :, 0, :], z[:, :, :, 1, :, 1, :]], axis=3)
    return jnp.swapaxes(st.reshape(b, 2, HEADS, DH, DH), -1, -2)


def _merge_kernel(u_ref, ys5_ref, d_ref, wglu_ref,
                  yf_ref, yb_ref, rkv_ref, gd_ref, gup_ref, ub_ref, lnw_ref, lnb_ref,
                  oatt_ref, gs5_ref, grw_ref, gatt_ref, x_ref, ws5_ref, wrw_ref, watt_ref, wmix_ref,
                  gpost_ref, g1_ref, o_ref):
    o_s5 = _s5_post(u_ref[...], ys5_ref[...], d_ref[...], wglu_ref[...])
    o_rw = _rwkv_post(yf_ref[...] + yb_ref[...], rkv_ref, gd_ref[:, W_LORA + A_LORA:LORA_W], gup_ref[...],
                      ub_ref[...], lnw_ref[...], lnb_ref[...])
    gate = lambda ref: _sigmoid(ref[...].astype(F32))
    merged = (gate(gs5_ref) * _dot(o_s5.astype(BF16), ws5_ref[...])
              + gate(grw_ref) * _dot(o_rw.astype(BF16), wrw_ref[...])
              + gate(gatt_ref) * _dot(oatt_ref[...].astype(BF16), watt_ref[...]))
    m = _dot(merged.astype(BF16), wmix_ref[...])
    o_ref[...] = x_ref[...] + g1_ref[...] * _rms(m, gpost_ref[...])


def _merge(main, y_s5, y_rw, rkv, o_att, gates, x, s5_d, w_glu_b, rwp, wts, mod_l, row_of_tile,
           g_post, tm):
    t = x.shape[0]
    tok = lambda w, c: pl.BlockSpec((tm, w), lambda i: (i, c))
    full = lambda r, c: pl.BlockSpec((r, c), lambda i: (0, 0))
    return pl.pallas_call(
        _merge_kernel,
        grid=(t // tm,),
        in_specs=[tok(DB, C_U // DB), tok(DB, 0), full(1, DB), full(DB, DB),
                  pl.BlockSpec((None, tm, DB), lambda i: (0, i, 0)),
                  pl.BlockSpec((None, tm, DB), lambda i: (1, i, 0)),
                  tok(3 * DB, C_RKV // (3 * DB)), tok(LORA_W, C_LORA // LORA_W), full(G_LORA, DB),
                  full(1, DB), full(1, DB),
                  full(1, DB),
                  tok(DB, 0), tok(D, 0), tok(D, 1), tok(D, 2), tok(D, 0),
                  full(DB, D), full(DB, D), full(DB, D), full(D, D), full(1, D),
                  _mod_spec(2, row_of_tile, 1)],
        out_specs=tok(D, 0),
        out_shape=jax.ShapeDtypeStruct((t, D), F32),
        compiler_params=_cparams(1),
    )(main, y_s5, s5_d, w_glu_b, y_rw, y_rw, rkv, main, rwp['g_up'], rwp['u'], rwp['ln_w'],
      rwp['ln_b'], o_att, gates, gates, gates, x, wts['br_s5'], wts['br_rw'], wts['br_att'],
      wts['mix'], g_post, mod_l)


def _ffn_dense_kernel(x_ref, gpre_ref, sc_ref, sh_ref, w1_ref, w3_ref, w2_ref, gpost_ref, g2_ref,
                      o_ref, h_scr, acc_scr):
    f = pl.program_id(1)

    @pl.when(f == 0)
    def _():
        h = _rms(x_ref[...], gpre_ref[...]) * (1.0 + sc_ref[...]) + sh_ref[...]
        h_scr[...] = h.astype(BF16)
        acc_scr[...] = jnp.zeros_like(acc_scr)

    h = h_scr[...]
    hid = _silu(_dot(h, w1_ref[...])) * _dot(h, w3_ref[...])
    acc_scr[...] += _dot(hid.astype(BF16), w2_ref[...])

    @pl.when(f == pl.num_programs(1) - 1)
    def _():
        o_ref[...] = x_ref[...] + g2_ref[...] * _rms(acc_scr[...], gpost_ref[...])


def _ffn_dense(x, mod_l, row_of_tile, g_pre, g_post, w1, w3, w2, tm, tf):
    t = x.shape[0]
    return pl.pallas_call(
        _ffn_dense_kernel,
        grid=(t // tm, FF_DENSE // tf),
        in_specs=[pl.BlockSpec((tm, D), lambda i, f: (i, 0)),
                  pl.BlockSpec((1, D), lambda i, f: (0, 0)),
                  _mod_spec(4, row_of_tile, 2), _mod_spec(3, row_of_tile, 2),
                  pl.BlockSpec((D, tf), lambda i, f: (0, f)),
                  pl.BlockSpec((D, tf), lambda i, f: (0, f)),
                  pl.BlockSpec((tf, D), lambda i, f: (f, 0)),
                  pl.BlockSpec((1, D), lambda i, f: (0, 0)),
                  _mod_spec(5, row_of_tile, 2)],
        out_specs=pl.BlockSpec((tm, D), lambda i, f: (i, 0)),
        out_shape=jax.ShapeDtypeStruct((t, D), F32),
        scratch_shapes=[pltpu.VMEM((tm, D), BF16), pltpu.VMEM((tm, D), F32)],
        compiler_params=_cparams(2),
    )(x, g_pre, mod_l, mod_l, w1, w3, w2, g_post, mod_l)


def _ffn_moe_kernel(x_ref, gpre_ref, sc_ref, sh_ref, rw_ref, rb_ref, w1_ref, w3_ref, w2_ref,
                    gpost_ref, g2_ref, o_ref, h_scr, comb_scr, acc_scr):
    e = pl.program_id(1)
    lane = lax.broadcasted_iota(jnp.int32, (1, LANES), 1)

    @pl.when(e == 0)
    def _():
        h = _rms(x_ref[...], gpre_ref[...]) * (1.0 + sc_ref[...]) + sh_ref[...]
        h_scr[...] = h.astype(BF16)
        acc_scr[...] = jnp.zeros_like(acc_scr)
        logits = jnp.dot(h, rw_ref[...], precision=HIGHEST, preferred_element_type=F32) + rb_ref[...]
        ex = jnp.exp(logits - jnp.max(logits, axis=-1, keepdims=True))
        probs = ex / jnp.sum(ex, axis=-1, keepdims=True)
        p1 = jnp.max(probs, axis=-1, keepdims=True)
        i1 = jnp.min(jnp.where(probs == p1, lane, LANES), axis=-1, keepdims=True)
        rest = jnp.where(lane == i1, -1.0, probs)
        p2 = jnp.max(rest, axis=-1, keepdims=True)
        i2 = jnp.min(jnp.where(rest == p2, lane, LANES), axis=-1, keepdims=True)
        den = p1 + p2
        comb_scr[...] = jnp.where(lane == i1, p1 / den, 0.0) + jnp.where(lane == i2, p2 / den, 0.0)

    h = h_scr[...]
    cw = jnp.sum(jnp.where(lane == e, comb_scr[...], 0.0), axis=-1, keepdims=True)
    hid = _silu(_dot(h, w1_ref[...])) * _dot(h, w3_ref[...])
    acc_scr[...] += cw * _dot(hid.astype(BF16), w2_ref[...])

    @pl.when(e == N_EXP - 1)
    def _():
        o_ref[...] = x_ref[...] + g2_ref[...] * _rms(acc_scr[...], gpost_ref[...])


def _ffn_moe(x, mod_l, row_of_tile, g_pre, g_post, rw, rb, w1, w3, w2, tm):
    t = x.shape[0]
    return pl.pallas_call(
        _ffn_moe_kernel,
        grid=(t // tm, N_EXP),
        in_specs=[pl.BlockSpec((tm, D), lambda i, e: (i, 0)),
                  pl.BlockSpec((1, D), lambda i, e: (0, 0)),
                  _mod_spec(4, row_of_tile, 2), _mod_spec(3, row_of_tile, 2),
                  pl.BlockSpec((D, LANES), lambda i, e: (0, 0)),
                  pl.BlockSpec((1, LANES), lambda i, e: (0, 0)),
                  pl.BlockSpec((None, D, FF_EXP), lambda i, e: (e, 0, 0)),
                  pl.BlockSpec((None, D, FF_EXP), lambda i, e: (e, 0, 0)),
                  pl.BlockSpec((None, FF_EXP, D), lambda i, e: (e, 0, 0)),
                  pl.BlockSpec((1, D), lambda i, e: (0, 0)),
                  _mod_spec(5, row_of_tile, 2)],
        out_specs=pl.BlockSpec((tm, D), lambda i, e: (i, 0)),
        out_shape=jax.ShapeDtypeStruct((t, D), F32),
        scratch_shapes=[pltpu.VMEM((tm, D), BF16), pltpu.VMEM((tm, LANES), F32),
                        pltpu.VMEM((tm, D), F32)],
        compiler_params=_cparams(2),
    )(x, g_pre, mod_l, mod_l, rw, rb, w1, w3, w2, g_post, mod_l)


def _reorder_w_in(w):
    qkv, u, rkv, lora, gates = (w[:, 0:1536], w[:, 1536:2048], w[:, 2048:3584], w[:, 3584:3840],
                                w[:, 3840:6912])
    return jnp.concatenate([gates, rkv, u, lora, qkv], axis=1)


def _layer(x, b, n, l, mod_l, row_of_tile, P, cache, tm):
    row = lambda a: a.reshape(1, -1)
    tm_proj = min(2 * tm, x.shape[0], n) if cache is not None else min(2 * tm, x.shape[0])
    row_of_ptile = (lambda i: row_of_tile(i * (tm_proj // tm)))
    outs = _projection(x, mod_l, row_of_ptile, row(P['norm_pre_mix'][l]), P['w_in_b'][l],
                       P['rwkv_conv'][l], n, tm_proj, cache is None)
    gates, proj, qkv = outs[:3]
    if cache is None:
        o_att = _ctx_attention(qkv, b, n)
        x0 = jnp.zeros((S5_G // S5_UG, 2, b, 2 * S5_UG * S5_P), F32)
        s0 = jnp.zeros((b, 2, HEADS // 2, LANES, LANES), F32)
    else:
        k_c, v_c, s5re, s5im, rw0 = cache
        o_att = _na_attention(qkv, k_c.reshape(-1, DB).astype(BF16), v_c.reshape(-1, DB).astype(BF16),
                              P['na_bias'], _na_row_masks(n // GRID_W), l, b, n)
        x0 = _s5_state_to_lanes(s5re, s5im)
        s0 = _state_to_pairs(rw0)
    y_s5, s5fin = _s5_mixer(proj, b, n, P['s5'], l, x0)
    y_rw, sfin = _rwkv_mixer(proj, b, n, P['rwkv'][l], s0)
    x = _merge(proj, y_s5, y_rw, proj, o_att, gates, x, row(P['s5_d'][l]), P['s5_w_glu_b'][l],
               P['rwkv'][l], P['merge'][l], mod_l, row_of_tile, row(P['norm_post_mix'][l]), tm)
    i = l // 2
    if l % 2 == 0:
        x = _ffn_dense(x, mod_l, row_of_tile, row(P['norm_pre_ffn'][l]), row(P['norm_post_ffn'][l]),
                       P['dense_w1_b'][i], P['dense_w3_b'][i], P['dense_w2_b'][i], tm, FF_DENSE // 2)
    else:
        x = _ffn_moe(x, mod_l, row_of_ptile, row(P['norm_pre_ffn'][l]), row(P['norm_post_ffn'][l]),
                     P['moe_rw'][i], P['moe_rb'][i], P['moe_w1_b'][i], P['moe_w3_b'][i],
                     P['moe_w2_b'][i], tm_proj)
    if cache is None:
        k_new = outs[3][:, DB:2 * DB]
        v_new = outs[3][:, 2 * DB:3 * DB]
        fre, fim = _s5_lanes_to_state(s5fin)
        return x, (k_new, v_new, fre, fim, sfin)
    return x, None


def kernel(x_prompt, x_sample, cache_k, cache_v, state_s5_re, state_s5_im, state_rwkv, c, c_ctx,
           w_ada, b_ada, norm_pre_mix, norm_post_mix, norm_pre_ffn, norm_post_ffn, w_in,
           s5_lam_re, s5_lam_im, s5_log_step, s5_b_re, s5_b_im, s5_c_re, s5_c_im, s5_d, s5_w_glu,
           rwkv_conv, rwkv_w0, rwkv_w_up, rwkv_a0, rwkv_a_up, rwkv_g_up, rwkv_k_k, rwkv_k_a,
           rwkv_u, rwkv_ln_w, rwkv_ln_b, att_rpb, w_br_s5, w_br_rwkv, w_br_att, w_mix_out,
           dense_w1, dense_w3, dense_w2, moe_router_w, moe_router_b, moe_w1, moe_w3, moe_w2):
    bc, nc_, _ = x_prompt.shape
    bl, nl, _ = x_sample.shape
    tm_c = min(512, bc * nc_)
    tm_l = min(512, nl)

    P = dict(norm_pre_mix=norm_pre_mix, norm_post_mix=norm_post_mix, norm_pre_ffn=norm_pre_ffn,
             norm_post_ffn=norm_post_ffn, s5_d=s5_d, rwkv_conv=rwkv_conv, att_rpb=att_rpb)
    P['w_in_b'] = [_reorder_w_in(w_in[l]).astype(BF16) for l in range(DEPTH)]
    P['s5_w_glu_b'] = s5_w_glu.astype(BF16)
    P['dense_w1_b'], P['dense_w3_b'], P['dense_w2_b'] = (dense_w1.astype(BF16), dense_w3.astype(BF16),
                                                         dense_w2.astype(BF16))
    P['moe_w1_b'], P['moe_w3_b'], P['moe_w2_b'] = (moe_w1.astype(BF16), moe_w3.astype(BF16),
                                                   moe_w2.astype(BF16))
    n_moe = moe_router_w.shape[0]
    P['moe_rw'] = jnp.pad(moe_router_w, ((0, 0), (0, 0), (0, LANES - N_EXP)))
    P['moe_rb'] = jnp.pad(moe_router_b, ((0, 0), (0, LANES - N_EXP)),
                          constant_values=NEG).reshape(n_moe, 1, LANES)
    P['merge'] = [dict(br_s5=w_br_s5[l].astype(BF16), br_rw=w_br_rwkv[l].astype(BF16),
                       br_att=w_br_att[l].astype(BF16), mix=w_mix_out[l].astype(BF16))
                  for l in range(DEPTH)]
    P['s5'] = jax.vmap(_s5_params)(s5_lam_re, s5_lam_im, s5_log_step, s5_b_re, s5_b_im, s5_c_re,
                                   s5_c_im)
    P['na_bias'] = _na_bias_tables(att_rpb)
    P['rwkv'] = []
    for l in range(DEPTH):
        P['rwkv'].append(dict(
            w0=rwkv_w0[l].reshape(2, 1, DB), w_up=rwkv_w_up[l], a0=rwkv_a0[l].reshape(2, 1, DB),
            a_up=rwkv_a_up[l], k_k=rwkv_k_k[l].reshape(1, DB), k_a=rwkv_k_a[l].reshape(1, DB),
            g_up=rwkv_g_up[l].astype(BF16), u=rwkv_u[l].reshape(1, DB),
            ln_w=rwkv_ln_w[l].reshape(1, DB), ln_b=rwkv_ln_b[l].reshape(1, DB)))

    cvec = jnp.zeros((8, D), F32).at[0].set(c_ctx).at[1:1 + bl].set(c)
    mod = _modulation(cvec, w_ada, b_ada)

    xp = x_prompt.reshape(bc * nc_, D)
    ks, vs, s5re, s5im, rws = [], [], [], [], []
    for l in range(DEPTH):
        xp, (k_n, v_n, fre, fim, sfin) = _layer(xp, bc, nc_, l, mod[l], lambda i: 0, P, None, tm_c)
        ks.append(k_n.reshape(bc, nc_, HEADS, DH))
        vs.append(v_n.reshape(bc, nc_, HEADS, DH))
        s5re.append(fre)
        s5im.append(fim)
        rws.append(_pairs_to_state(sfin))
    new_k = jnp.stack(ks, axis=1)
    new_v = jnp.stack(vs, axis=1)
    new_s5_re = jnp.stack(s5re, axis=1)
    new_s5_im = jnp.stack(s5im, axis=1)
    new_rwkv = jnp.stack(rws, axis=1)

    xs = x_sample.reshape(bl * nl, D)
    tiles_per_seq = nl // tm_l
    for l in range(DEPTH):
        cache = (cache_k[:, l], cache_v[:, l], state_s5_re[:, l], state_s5_im[:, l], state_rwkv[:, l])
        xs, _ = _layer(xs, bl, nl, l, mod[l], lambda i: 1 + i // tiles_per_seq, P, cache, tm_l)

    return (xp.reshape(bc, nc_, D), xs.reshape(bl, nl, D), new_k, new_v, new_s5_re, new_s5_im,
            new_rwkv)
```

```python
import functools

import numpy as np
import jax
import jax.numpy as jnp
from jax import lax
from jax.experimental import pallas as pl
from jax.experimental.pallas import tpu as pltpu

F32 = jnp.float32
BF16 = jnp.bfloat16
HIGHEST = lax.Precision.HIGHEST

D = 1024
DEPTH = 2
GRID_W = 64
DH = 64
HEADS = 8
DB = 512
WIN_ROWS = 8
WIN_COLS = 16
S5_CH = 16
S5_G = 32
S5_P = 64
S5_J = 8
S5_UG = 8
FF_DENSE = 2816
N_EXP = 8
FF_EXP = 1024
EPS = 1e-6
GN_EPS = 64e-5
NEG = -1e30

D_IN = 6912
TN_PROJ = 768
J_MAIN = 4
J_QKV = 7
J_RKV_TILES = 2
W_MAIN = (J_QKV - J_MAIN) * TN_PROJ
C_RKV = 0
C_U = 1536
C_LORA = 2048
W_LORA, A_LORA, G_LORA = 64, 64, 128
LORA_W = W_LORA + A_LORA + G_LORA
SUBLANES = 8
HALO = 2 * SUBLANES

LANES = 128
RW_L = 64
NA_RB = 4
NA_KR = 12
VMEM_LIMIT = 56 * 1024 * 1024


def _cparams(n_axes, vmem=VMEM_LIMIT):
    return pltpu.CompilerParams(dimension_semantics=("arbitrary",) * n_axes,
                                vmem_limit_bytes=vmem)


def _dot(a, b):
    return jnp.dot(a, b, preferred_element_type=F32)


def _dot_nt(a, b):
    return lax.dot_general(a, b, (((1,), (1,)), ((), ())), preferred_element_type=F32)


def _dot_tn(a, b):
    return lax.dot_general(a, b, (((0,), (0,)), ((), ())), preferred_element_type=F32)


def _split2(x):
    hi = x.astype(BF16)
    lo = (x - hi.astype(F32)).astype(BF16)
    return hi, lo


def _dot_rhs_exact(x, m):
    hi, lo = _split2(x)
    return _dot(hi, m) + _dot(lo, m)


def _dot_lhs_exact(m, x):
    hi, lo = _split2(x)
    return _dot(m, hi) + _dot(m, lo)


def _mm3(a, b):
    a1, a2 = _split2(a)
    b1, b2 = _split2(b)
    return _dot(a1, b1) + _dot(a1, b2) + _dot(a2, b1)


def _mm(a, b):
    return _dot(a.astype(BF16), b.astype(BF16))


def _sigmoid(x):
    return 0.5 * (jnp.tanh(0.5 * x) + 1.0)


def _silu(x):
    return x * _sigmoid(x)


def _softplus(x):
    return jnp.maximum(x, 0.0) + jnp.log(1.0 + jnp.exp(-jnp.abs(x)))


def _rms(x, g):
    return x * lax.rsqrt(jnp.mean(x * x, axis=-1, keepdims=True) + EPS) * g


def _head_block_matrix(width, value):
    r = lax.broadcasted_iota(jnp.int32, (width, width), 0) // DH
    c = lax.broadcasted_iota(jnp.int32, (width, width), 1) // DH
    return jnp.where(r == c, value, 0.0).astype(BF16)


def _mod_kernel(c_ref, w_ref, b_ref, o_ref):
    s = _silu(c_ref[...]).astype(BF16)
    o_ref[...] = _dot(s, w_ref[...].astype(BF16)) + b_ref[...]


def _modulation(cvec, w_ada, b_ada):
    tn = 1536
    out = pl.pallas_call(
        _mod_kernel,
        grid=(DEPTH, 6 * D // tn),
        in_specs=[pl.BlockSpec((8, D), lambda l, j: (0, 0)),
                  pl.BlockSpec((None, D, tn), lambda l, j: (l, 0, j)),
                  pl.BlockSpec((None, 1, tn), lambda l, j: (l, 0, j))],
        out_specs=pl.BlockSpec((None, 8, tn), lambda l, j: (l, 0, j)),
        out_shape=jax.ShapeDtypeStruct((DEPTH, 8, 6 * D), F32),
        compiler_params=_cparams(2),
    )(cvec, w_ada, b_ada.reshape(DEPTH, 1, 6 * D))
    return out.reshape(DEPTH, 8, 1, 6 * D)


def _mod_spec(part, row_of_tile, n_grid):
    if n_grid == 1:
        return pl.BlockSpec((None, 1, D), lambda i: (row_of_tile(i), 0, part))
    return pl.BlockSpec((None, 1, D), lambda i, j: (row_of_tile(i), 0, part))


def _proj_kernel(x_ref, xp_ref, xn_ref, g_ref, sc_ref, sh_ref, w_ref, cw_ref, og_ref, om_ref, oq_ref,
                 *rest, f32_qkv, n):
    h_scr = rest[-1]
    i = pl.program_id(0)
    j = pl.program_id(1)
    tm = x_ref.shape[0]

    @pl.when(j == 0)
    def _():
        norm = lambda x: (_rms(x, g_ref[...]) * (1.0 + sc_ref[...]) + sh_ref[...]).astype(BF16)
        h_scr[0:tm, :] = norm(x_ref[...])
        h_scr[tm:tm + SUBLANES, :] = norm(xp_ref[...])
        h_scr[tm + SUBLANES:tm + HALO, :] = norm(xn_ref[...])

    acc_all = _dot(h_scr[...], w_ref[...])
    acc = acc_all[0:tm]

    @pl.when(j < J_MAIN)
    def _():
        og_ref[...] = acc.astype(BF16)

    @pl.when((j >= J_MAIN) & (j < J_MAIN + J_RKV_TILES))
    def _():
        row = lax.broadcasted_iota(jnp.int32, (tm, 1), 0)
        prev_row = acc_all[tm + SUBLANES - 1:tm + SUBLANES]
        next_row = acc_all[tm + SUBLANES:tm + SUBLANES + 1]
        x_dn = pltpu.roll(acc, 1, axis=0)
        x_up = pltpu.roll(acc, tm - 1, axis=0)
        if n >= tm:
            tiles = n // tm
            prev_row = jnp.where(i % tiles == 0, 0.0, prev_row)
            next_row = jnp.where(i % tiles == tiles - 1, 0.0, next_row)
            x_dn = jnp.where(row == 0, prev_row, x_dn)
            x_up = jnp.where(row == tm - 1, next_row, x_up)
        else:
            x_dn = jnp.where(row % n == 0, 0.0, x_dn)
            x_up = jnp.where(row % n == n - 1, 0.0, x_up)
        om_ref[...] = x_dn * cw_ref[0:1, :] + acc * cw_ref[1:2, :] + x_up * cw_ref[2:3, :]

    @pl.when((j >= J_MAIN + J_RKV_TILES) & (j < J_QKV))
    def _():
        om_ref[...] = acc

    @pl.when(j >= J_QKV)
    def _():
        oq_ref[...] = acc.astype(BF16)
        if f32_qkv:
            rest[0][...] = acc


def _projection(x, mod_l, row_of_tile, g_pre, w_in_b, conv_w, n, tm, f32_qkv):
    t = x.shape[0]
    n_tiles = D_IN // TN_PROJ
    n_halo_blocks = t // SUBLANES
    per_tile = tm // SUBLANES
    tile = lambda j0, j1: pl.BlockSpec((tm, TN_PROJ), lambda i, j: (i, jnp.clip(j - j0, 0, j1 - j0 - 1)))
    out_specs = [tile(0, J_MAIN), tile(J_MAIN, J_QKV), tile(J_QKV, n_tiles)]
    out_shape = [jax.ShapeDtypeStruct((t, 3 * D), BF16), jax.ShapeDtypeStruct((t, W_MAIN), F32),
                 jax.ShapeDtypeStruct((t, 3 * DB), BF16)]
    if f32_qkv:
        out_specs.append(tile(J_QKV, n_tiles))
        out_shape.append(jax.ShapeDtypeStruct((t, 3 * DB), F32))
    return pl.pallas_call(
        functools.partial(_proj_kernel, f32_qkv=f32_qkv, n=n),
        grid=(t // tm, n_tiles),
        in_specs=[pl.BlockSpec((tm, D), lambda i, j: (i, 0)),
                  pl.BlockSpec((SUBLANES, D), lambda i, j: (jnp.maximum(i * per_tile - 1, 0), 0)),
                  pl.BlockSpec((SUBLANES, D),
                               lambda i, j: (jnp.minimum((i + 1) * per_tile, n_halo_blocks - 1), 0)),
                  pl.BlockSpec((1, D), lambda i, j: (0, 0)),
                  _mod_spec(1, row_of_tile, 2),
                  _mod_spec(0, row_of_tile, 2),
                  pl.BlockSpec((D, TN_PROJ), lambda i, j: (0, j)),
                  pl.BlockSpec((3, TN_PROJ), lambda i, j: (0, jnp.clip(j - J_MAIN, 0, J_RKV_TILES - 1)))],
        out_specs=out_specs,
        out_shape=out_shape,
        scratch_shapes=[pltpu.VMEM((tm + HALO, D), BF16)],
        compiler_params=_cparams(2),
    )(x, x, x, g_pre, mod_l, mod_l, w_in_b, conv_w)


def _pair_masks():
    lane = lax.broadcasted_iota(jnp.int32, (1, LANES), 1)
    return lane < DH


def _softmax_pv(parts):
    m = None
    for s, _ in parts:
        mx = jnp.max(s, axis=-1, keepdims=True)
        m = mx if m is None else jnp.maximum(m, mx)
    es = [jnp.exp(s - m) for s, _ in parts]
    den = None
    for e in es:
        sm = jnp.sum(e, axis=-1, keepdims=True)
        den = sm if den is None else den + sm
    out = None
    for e, (_, v) in zip(es, parts):
        o = _dot(e.astype(BF16), v)
        out = o if out is None else out + o
    return out * (1.0 / den)


def _ctx_att_kernel(q_ref, k_ref, v_ref, o_ref):
    scale = DH ** -0.5
    m0 = _pair_masks()
    for p in range(HEADS // 2):
        sl = slice(p * LANES, (p + 1) * LANES)
        qp, kp, vp = q_ref[:, sl] * scale, k_ref[:, sl], v_ref[:, sl]
        o_pair = None
        for hh in range(2):
            msk = m0 if hh == 0 else jnp.logical_not(m0)
            qm = jnp.where(msk, qp, jnp.zeros_like(qp))
            o = _softmax_pv([(_dot_nt(qm, kp), vp)])
            o_pair = o if o_pair is None else jnp.where(m0, o_pair, o)
        o_ref[:, sl] = o_pair


def _ctx_attention(qkv, b, n):
    return pl.pallas_call(
        _ctx_att_kernel,
        grid=(b,),
        in_specs=[pl.BlockSpec((n, DB), lambda i: (i, 0)),
                  pl.BlockSpec((n, DB), lambda i: (i, 1)),
                  pl.BlockSpec((n, DB), lambda i: (i, 2))],
        out_specs=pl.BlockSpec((n, DB), lambda i: (i, 0)),
        out_shape=jax.ShapeDtypeStruct((b * n, DB), F32),
        compiler_params=_cparams(1),
    )(qkv, qkv, qkv)


def _na_key_start(rb, rows):
    return jnp.clip(rb * NA_RB - WIN_ROWS // 2, 0, rows - NA_KR)


def _na_kernel(q_ref, k_ref, v_ref, kc_ref, vc_ref, tp_ref, rm_ref, o_ref, *, rows):
    scale = DH ** -0.5
    m0 = _pair_masks()
    rb = pl.program_id(1)
    u0 = _na_key_start(rb, rows)
    start = pl.multiple_of(u0 * GRID_W, GRID_W)
    nk = NA_KR * GRID_W
    rel = [[jnp.clip(u0 + 2 * m - (rb * NA_RB + qr) + WIN_ROWS, 0, 2 * WIN_ROWS)
            for m in range(NA_KR // 2)] for qr in range(NA_RB)]
    row_mask = rm_ref[...]

    def bias(h):
        return jnp.concatenate(
            [jnp.concatenate([tp_ref[h, rel[qr][m]] for m in range(NA_KR // 2)], axis=1)
             for qr in range(NA_RB)], axis=0) + row_mask

    for p in range(HEADS // 2):
        sl = slice(p * LANES, (p + 1) * LANES)
        qp = q_ref[:, sl] * scale
        kp = k_ref[pl.ds(start, nk), sl]
        vp = v_ref[pl.ds(start, nk), sl]
        kcp, vcp = kc_ref[:, sl], vc_ref[:, sl]
        o_pair = None
        for hh in range(2):
            msk = m0 if hh == 0 else jnp.logical_not(m0)
            qm = jnp.where(msk, qp, jnp.zeros_like(qp))
            s_loc = _dot_nt(qm, kp) + bias(2 * p + hh)
            s_ctx = _dot_nt(qm, kcp)
            o = _softmax_pv([(s_loc, vp), (s_ctx, vcp)])
            o_pair = o if o_pair is None else jnp.where(m0, o_pair, o)
        o_ref[:, sl] = o_pair


def _na_bias_tables(rpb):
    nl = rpb.shape[0]
    nrel_r, nrel_c = 2 * WIN_ROWS - 1, 2 * WIN_COLS - 1
    qc = np.arange(GRID_W)[:, None]
    kc = np.arange(GRID_W)[None, :]
    cs = np.clip(qc - WIN_COLS // 2, 0, GRID_W - WIN_COLS)
    col_ok = (kc >= cs) & (kc < cs + WIN_COLS)
    col_rel = np.clip(kc - qc + (WIN_COLS - 1), 0, nrel_c - 1)
    col_hot = (np.arange(nrel_c)[:, None, None] == col_rel[None]) & col_ok[None]
    col_hot = jnp.asarray(col_hot.reshape(nrel_c, GRID_W * GRID_W), F32)
    tb = jnp.dot(rpb.reshape(-1, nrel_c), col_hot, precision=HIGHEST)
    tb = jnp.where(jnp.asarray(col_ok.reshape(1, -1)), tb, NEG)
    tb = tb.reshape(nl, HEADS, nrel_r, GRID_W, GRID_W)
    tb = jnp.pad(tb, ((0, 0), (0, 0), (1, 2), (0, 0), (0, 0)))
    return jnp.concatenate([tb[:, :, :-1], tb[:, :, 1:]], axis=-1)


def _na_row_masks(rows):
    n_rb = rows // NA_RB
    wr = min(WIN_ROWS, rows)
    out = np.full((3, NA_RB, GRID_W, NA_KR, GRID_W), NEG, np.float32)
    for vi, rb in enumerate((0, 1, n_rb - 1)):
        u0 = int(np.clip(rb * NA_RB - WIN_ROWS // 2, 0, rows - NA_KR))
        for qr in range(NA_RB):
            rs = int(np.clip(rb * NA_RB + qr - wr // 2, 0, rows - wr))
            for kr in range(NA_KR):
                if rs <= u0 + kr < rs + wr:
                    out[vi, qr, :, kr, :] = 0.0
    return jnp.asarray(out.reshape(3, NA_RB * GRID_W, NA_KR * GRID_W))


def _na_attention(qkv, k_ctx, v_ctx, bias, row_masks, l, b, n):
    rows = n // GRID_W
    n_rb = rows // NA_RB
    tq = NA_RB * GRID_W
    lc = k_ctx.shape[0] // b

    def variant(j):
        return jnp.where(j == 0, 0, jnp.where(j == n_rb - 1, 2, 1))

    return pl.pallas_call(
        functools.partial(_na_kernel, rows=rows),
        grid=(b, n_rb),
        in_specs=[pl.BlockSpec((tq, DB), lambda i, j: (i * n_rb + j, 0)),
                  pl.BlockSpec((n, DB), lambda i, j: (i, 1)),
                  pl.BlockSpec((n, DB), lambda i, j: (i, 2)),
                  pl.BlockSpec((lc, DB), lambda i, j: (i, 0)),
                  pl.BlockSpec((lc, DB), lambda i, j: (i, 0)),
                  pl.BlockSpec((None, HEADS, 2 * WIN_ROWS + 1, GRID_W, 2 * GRID_W),
                               lambda i, j: (l, 0, 0, 0, 0)),
                  pl.BlockSpec((None, tq, NA_KR * GRID_W), lambda i, j: (variant(j), 0, 0))],
        out_specs=pl.BlockSpec((tq, DB), lambda i, j: (i * n_rb + j, 0)),
        out_shape=jax.ShapeDtypeStruct((b * n, DB), F32),
        compiler_params=_cparams(2),
    )(qkv, qkv, qkv, k_ctx, v_ctx, bias, row_masks)


def _cmul(a, b):
    return a[0] * b[0] - a[1] * b[1], a[0] * b[1] + a[1] * b[0]


def _cexp(re, im):
    e = jnp.exp(re)
    return e * jnp.cos(im), e * jnp.sin(im)


def _s5_params(lam_re, lam_im, log_step, b_re, b_im, c_re, c_im):
    hp = dict(precision=HIGHEST)
    J, nq, ug = S5_J, S5_G // S5_UG, S5_UG
    sw = 2 * ug * S5_P

    def block_diag(x2d, row_group, col_inner):
        rows = x2d.shape[-2]
        ci = np.arange(sw)
        src = (ci // (ug * col_inner)) * col_inner + ci % col_inner
        expand = jnp.asarray(np.arange(LANES)[:, None] == src[None, :], F32)
        keep = jnp.asarray(row_group(np.arange(rows))[:, None] == ((ci // col_inner) % ug)[None, :])
        return jnp.where(keep, jnp.matmul(x2d, expand, precision=HIGHEST), 0.0)

    grp16 = lambda r: (r // S5_CH) % ug
    grp64 = lambda r: (r // S5_P) % ug
    step = jnp.exp(log_step)[..., None]
    lam_bar = _cexp(lam_re * step, lam_im * step)
    den = lam_re * lam_re + lam_im * lam_im
    num = (lam_bar[0] - 1.0, lam_bar[1])
    coef = ((num[0] * lam_re + num[1] * lam_im) / den, (num[1] * lam_re - num[0] * lam_im) / den)
    b_bar = _cmul((coef[0][..., None], coef[1][..., None]), (b_re, b_im))
    dd = jnp.arange(J + 1, dtype=F32)[:, None, None, None]
    lam_pow = _cexp(dd * (lam_re * step)[None], dd * (lam_im * step)[None])

    def units(x, g_axis):
        return x.reshape(x.shape[:g_axis] + (nq, ug) + x.shape[g_axis + 1:])

    sel = [jnp.stack([c[:J][::-1, 0], c[:J][:, 1]], axis=0) for c in lam_pow]
    wb = _cmul((sel[0][..., None], sel[1][..., None]), (b_bar[0][:, None], b_bar[1][:, None]))
    wb = units(jnp.stack(wb, axis=0), 3)
    bcat = block_diag(jnp.transpose(wb, (3, 1, 2, 4, 6, 0, 5)).reshape(nq, 2, J * LANES, LANES),
                      grp16, S5_P)
    lp = [jnp.transpose(c[:J], (1, 2, 0, 3))[:, :, :, None, :] for c in lam_pow]
    cl = _cmul((c_re[:, :, None], c_im[:, :, None]), lp)
    kd = (jnp.einsum('dgjcp,dgpk->dgjck', cl[0], b_bar[0], **hp)
          - jnp.einsum('dgjcp,dgpk->dgjck', cl[1], b_bar[1], **hp))
    oi = np.arange(J)[:, None]
    oo = np.arange(J)[None, :]
    lag_hot = np.stack([(oo - oi)[..., None] == np.arange(J), (oi - oo)[..., None] == np.arange(J)])
    t5 = jnp.einsum('dioj,dgjce->dgioce', jnp.asarray(lag_hot, F32), kd, **hp)
    t5 = units(t5[0] + t5[1], 0)
    tsum = block_diag(jnp.transpose(t5, (0, 2, 1, 5, 3, 4)).reshape(nq, J * LANES, LANES),
                      grp16, S5_CH)
    lq = [jnp.transpose(c[1:], (1, 2, 0, 3))[:, :, :, None, :] for c in lam_pow]
    cm = _cmul((c_re[:, :, None], c_im[:, :, None]), lq)
    cc = jnp.stack([cm[0], -cm[1]], axis=0)
    cc = jnp.stack([cc[:, 0], cc[:, 1, :, ::-1]], axis=1)
    ccat = block_diag(jnp.transpose(units(cc, 2), (2, 1, 0, 3, 6, 4, 5)).reshape(nq, 2 * sw, LANES),
                      grp64, S5_CH)
    b1 = jnp.transpose(units(jnp.stack(b_bar, axis=0), 2), (2, 1, 3, 5, 0, 4))
    b1 = block_diag(b1.reshape(nq, 2, LANES, LANES), grp16, S5_P)

    def lanes(c):
        x = jnp.concatenate([c[0].reshape(2, nq, ug * S5_P), c[1].reshape(2, nq, ug * S5_P)], axis=-1)
        return jnp.transpose(x, (1, 0, 2))[:, :, None, :]

    return dict(bcat=bcat.astype(BF16), tsum=tsum.astype(BF16), ccat=ccat.astype(BF16), b1=b1,
                lj=lanes((lam_pow[0][J], lam_pow[1][J])), l1=lanes(lam_bar))


def _s5_kernel(u_ref, bcat_ref, tsum_ref, ccat_ref, lj_ref, l1_ref, b1_ref, x0_ref,
               y_ref, fin_ref, z_scr, *, nseq, n):
    J = S5_J
    nsub = n // J
    ns = nseq * nsub
    hw = S5_UG * S5_P
    if nseq == 1:
        ucat = jnp.concatenate([u_ref[pl.ds(o, ns, stride=J), :] for o in range(J)], axis=1)
    else:
        ucat = jnp.concatenate(
            [jnp.concatenate([u_ref[pl.ds(j * J + o, nseq, stride=n), :] for o in range(J)], axis=1)
             for j in range(nsub)], axis=0)
    ucat = ucat.astype(BF16)
    for d in range(2):
        z_scr[d] = _dot(ucat, bcat_ref[d])

    def cstep(d, x, z):
        lre, lim = lj_ref[d, :, 0:hw], lj_ref[d, :, hw:]
        return jnp.concatenate([lre * x[:, :hw] - lim * x[:, hw:] + z[:, :hw],
                                lre * x[:, hw:] + lim * x[:, :hw] + z[:, hw:]], axis=1)

    for d in range(2):
        tok = 0 if d == 0 else n - 1
        uf = u_ref[pl.ds(tok, nseq, stride=n), :] if nseq > 1 else u_ref[tok:tok + 1, :]
        bu = jnp.dot(uf, b1_ref[d], precision=HIGHEST, preferred_element_type=F32)
        x0 = x0_ref[d]
        lre, lim = l1_ref[d, :, 0:hw], l1_ref[d, :, hw:]
        fin_ref[d] = jnp.concatenate([lre * x0[:, :hw] - lim * x0[:, hw:] + bu[:, :hw],
                                      lre * x0[:, hw:] + lim * x0[:, :hw] + bu[:, hw:]], axis=1)

    if nseq == 1:
        def step(j, carry):
            out = []
            for d in range(2):
                row = pl.ds(j if d == 0 else nsub - 1 - j, 1)
                z = z_scr[d, row, :]
                z_scr[d, row, :] = carry[d]
                out.append(cstep(d, carry[d], z))
            return tuple(out)

        lax.fori_loop(0, nsub, step, (x0_ref[0], x0_ref[1]), unroll=4)
    else:
        xs = [x0_ref[0], x0_ref[1]]
        for j in range(nsub):
            for d in range(2):
                rows = pl.ds((j if d == 0 else nsub - 1 - j) * nseq, nseq)
                z = z_scr[d, rows, :]
                z_scr[d, rows, :] = xs[d]
                xs[d] = cstep(d, xs[d], z)

    xin = jnp.concatenate([z_scr[0], z_scr[1]], axis=1).astype(BF16)
    ycat = _dot(ucat, tsum_ref[...]) + _dot(xin, ccat_ref[...])
    for o in range(J):
        if nseq == 1:
            y_ref[pl.ds(o, ns, stride=J), :] = ycat[:, o * LANES:(o + 1) * LANES]
        else:
            for j in range(nsub):
                y_ref[pl.ds(j * J + o, nseq, stride=n), :] = ycat[j * nseq:(j + 1) * nseq,
                                                                  o * LANES:(o + 1) * LANES]


def _s5_scan(proj, b, n, prm, l, x0, nseq):
    nq = S5_G // S5_UG
    sw = 2 * S5_UG * S5_P
    jl = S5_J * LANES
    r = nseq * n
    ns = r // S5_J
    if nseq == 1:
        x0 = x0.reshape(nq, 2, b, 1, sw)
        st_spec = pl.BlockSpec((None, 2, None, 1, sw), lambda q, i: (q, 0, i, 0, 0))
    else:
        st_spec = pl.BlockSpec((None, 2, nseq, sw), lambda q, i: (q, 0, i, 0))
    wspec = lambda *shp: pl.BlockSpec((None, None) + shp, lambda q, i: (l, q) + (0,) * len(shp))
    y, fin = pl.pallas_call(
        functools.partial(_s5_kernel, nseq=nseq, n=n),
        grid=(nq, b // nseq),
        in_specs=[pl.BlockSpec((r, LANES), lambda q, i: (i, C_U // LANES + q)),
                  wspec(2, jl, sw), wspec(jl, jl), wspec(2 * sw, jl), wspec(2, 1, sw),
                  wspec(2, 1, sw), wspec(2, LANES, sw), st_spec],
        out_specs=[pl.BlockSpec((r, LANES), lambda q, i: (i, q)), st_spec],
        out_shape=[jax.ShapeDtypeStruct((b * n, DB), F32), jax.ShapeDtypeStruct(x0.shape, F32)],
        scratch_shapes=[pltpu.VMEM((2, ns, sw), F32)],
        compiler_params=_cparams(2),
    )(proj, prm['bcat'], prm['tsum'], prm['ccat'], prm['lj'], prm['l1'], prm['b1'], x0)
    return y, fin.reshape(nq, 2, b, sw)


def _s5_post(u, y_scan, d_skip, w_glu):
    y = jax.nn.gelu(d_skip * u + y_scan)
    return y * _sigmoid(_dot(y.astype(BF16), w_glu))


def _s5_state_to_lanes(s_re, s_im):
    b = s_re.shape[0]
    nq = S5_G // S5_UG
    x = jnp.concatenate([s_re.reshape(b, 2, nq, S5_UG * S5_P), s_im.reshape(b, 2, nq, S5_UG * S5_P)],
                        axis=-1)
    return jnp.transpose(x, (2, 1, 0, 3))


def _s5_lanes_to_state(x):
    nq, _, b, _ = x.shape
    hw = S5_UG * S5_P
    x = jnp.transpose(x, (2, 1, 0, 3))
    return x[..., :hw].reshape(b, 2, S5_G, S5_P), x[..., hw:].reshape(b, 2, S5_G, S5_P)


def _s5_mixer(proj, b, n, sp, l, x0):
    nseq = 8 if (n // S5_J <= 64 and b % 8 == 0) else 1
    return _s5_scan(proj, b, n, sp, l, x0, nseq)


def _rwkv_chunk_kernel(rc_ref, kc_ref, vc_ref, lora_ref, w0_ref, wup_ref, a0_ref, aup_ref,
                       kk_ref, ka_ref, s0_ref, y_ref, sfin_ref, z_scr, *, nc, nseq):
    d = pl.program_id(1)
    ci = pl.program_id(2)
    L = RW_L
    rows = nseq * L

    @pl.when(ci == 0)
    def _():
        z_scr[...] = s0_ref[...]

    flat = lambda ref: ref[...].reshape(rows, ref.shape[-1])
    rc, kc, vc, lora = flat(rc_ref), flat(kc_ref), flat(vc_ref), flat(lora_ref)
    wd = lora[:, 0:W_LORA]
    ad = lora[:, W_LORA:W_LORA + A_LORA]
    lora_w = _mm3(jnp.tanh(wd), wup_ref[...])
    log_w = -_softplus(-(w0_ref[...] + lora_w)) - 0.5
    lw = -jnp.exp(log_w)
    a = _sigmoid(a0_ref[...] + _mm3(ad, aup_ref[...]))
    kd = kc * (1.0 + (a - 1.0) * ka_ref[...])
    kk = kc * kk_ref[...]
    kk = kk * lax.rsqrt(_mm(kk * kk, _head_block_matrix(DB, 1.0)) + 1e-12)
    alpha = -kk
    beta = kk * a

    sgn = 1 - 2 * d
    tt = lax.broadcasted_iota(jnp.int32, (rows, rows), 0)
    ss = lax.broadcasted_iota(jnp.int32, (rows, rows), 1)
    tri = jnp.where(((tt % L - ss % L) * sgn >= 0) & (tt // L == ss // L), 1.0, 0.0).astype(BF16)
    c = _dot_lhs_exact(tri, lw)
    c_ex = c - lw
    ctot = jnp.concatenate(
        [jnp.broadcast_to(jnp.sum(lw[s * L:(s + 1) * L], axis=0, keepdims=True), (L, DB))
         for s in range(nseq)], axis=0)
    mid = 0.5 * ctot
    e_in = jnp.exp(c - mid)
    e_ex = jnp.exp(c_ex - mid)
    e_out = jnp.exp(mid - c)
    e_mid = jnp.exp(mid)
    al_t = alpha * e_ex
    r_t = rc * e_in
    be_t = beta * e_out
    k_t = kd * e_out
    a0s = al_t * e_mid
    r0s = r_t * e_mid
    bps = be_t * e_mid
    kps = k_t * e_mid
    p_l = e_mid * e_mid

    m0 = _pair_masks()
    t2 = lax.broadcasted_iota(jnp.int32, (2 * L, 2 * L), 0) % L
    s2 = lax.broadcasted_iota(jnp.int32, (2 * L, 2 * L), 1) % L
    strict = (t2 - s2) * sgn > 0
    incl = (t2 - s2) * sgn >= 0
    eye = (lax.broadcasted_iota(jnp.int32, (LANES, LANES), 0)
           == lax.broadcasted_iota(jnp.int32, (LANES, LANES), 1))
    eye_f = jnp.where(eye, 1.0, 0.0)
    zeros = jnp.zeros((LANES, LANES), F32)

    chains = [(s, p) for s in range(nseq) for p in range(HEADS // 2)]
    pairs = range(len(chains))
    rws = [slice(s * L, (s + 1) * L) for s, _ in chains]
    sls = [slice(p * LANES, (p + 1) * LANES) for _, p in chains]

    def stack(x, p):
        xp = x[rws[p], sls[p]]
        return jnp.concatenate([jnp.where(m0, xp, 0.0), jnp.where(m0, 0.0, xp)], axis=0)

    raws = [_dot_nt(jnp.concatenate([stack(al_t, p), stack(r_t, p)], axis=0).astype(BF16),
                    jnp.concatenate([stack(be_t, p), stack(k_t, p)], axis=0).astype(BF16))
            for p in pairs]
    amat = [jnp.where(strict, r[:LANES, :LANES], 0.0) for r in raws]
    bmat = [jnp.where(strict, r[:LANES, LANES:], 0.0) for r in raws]
    qbk = [jnp.concatenate([jnp.where(incl, r[LANES:, :LANES], 0.0),
                            jnp.where(incl, r[LANES:, LANES:], 0.0)], axis=1) for r in raws]

    smat = [eye_f + a for a in amat]
    pw = [_mm(a, a) for a in amat]
    for _ in range(4):
        xs = [_mm(pw[p], jnp.concatenate([pw[p], smat[p]], axis=1)) for p in pairs]
        smat = [smat[p] + xs[p][:, LANES:] for p in pairs]
        pw = [x[:, :LANES] for x in xs]
    tmat = [smat[p] + _mm(pw[p], smat[p]) for p in pairs]

    vs = [stack(vc, p) for p in pairs]
    bv = [_mm(bmat[p], vs[p]) for p in pairs]
    wu = [_mm(tmat[p], jnp.concatenate([stack(a0s, p), bv[p]], axis=1)) for p in pairs]
    rhs2 = [jnp.concatenate([wu[p], jnp.concatenate([zeros, vs[p]], axis=1)], axis=0).astype(BF16)
            for p in pairs]
    out_a = [_dot(qbk[p].astype(BF16), rhs2[p]) for p in pairs]
    out_b = [_dot_tn(jnp.concatenate([stack(bps, p), stack(kps, p)], axis=0).astype(BF16), rhs2[p])
             for p in pairs]
    hz = [_mm(jnp.concatenate(
        [stack(r0s, p) + out_a[p][:, :LANES],
         jnp.where(eye, p_l[rws[p], sls[p]][0:1], 0.0) + out_b[p][:, :LANES]], axis=0), z_scr[chains[p]])
          for p in pairs]
    for p in pairs:
        y = out_a[p][:, LANES:] + hz[p][:LANES]
        z_scr[chains[p]] = hz[p][LANES:] + out_b[p][:, LANES:]
        y_ref[chains[p][0], :, sls[p]] = y[:L] + y[L:]

    @pl.when(ci == nc - 1)
    def _():
        sfin_ref[...] = z_scr[...]


def _rwkv_post(y, rkv_ref, gd, g_up, u_bonus, ln_w, ln_b):
    avg = _head_block_matrix(DB, 1.0 / DH)
    mu = _dot_rhs_exact(y, avg)
    yc = y - mu
    var = _dot_rhs_exact(yc * yc, avg)
    yn = yc * lax.rsqrt(var + GN_EPS) * ln_w + ln_b
    rc, kc, vc = rkv_ref[:, 0:DB], rkv_ref[:, DB:2 * DB], rkv_ref[:, 2 * DB:3 * DB]
    bonus = _mm(rc * kc * u_bonus, _head_block_matrix(DB, 1.0)) * vc
    g = _dot(_sigmoid(gd).astype(BF16), g_up)
    return (yn + bonus) * g


def _rwkv_mixer(proj, b, n, prm, s0_pairs):
    t = b * n
    nc = n // RW_L
    nseq = 4 if b % 4 == 0 else (2 if b % 2 == 0 else 1)
    hp = HEADS // 2

    def chunk(dd, ci):
        return jnp.where(dd == 0, ci, nc - 1 - ci)

    tok = lambda w, col: pl.BlockSpec((nseq, RW_L, w), lambda i, dd, ci: (i, chunk(dd, ci), col))
    dirp = lambda r: pl.BlockSpec((None, r, DB), lambda i, dd, ci: (dd, 0, 0))
    shared = pl.BlockSpec((1, DB), lambda i, dd, ci: (0, 0))
    state = pl.BlockSpec((nseq, None, hp, LANES, LANES), lambda i, dd, ci: (i, dd, 0, 0, 0))
    main3 = proj.reshape(b, n, W_MAIN)
    y2, sfin = pl.pallas_call(
        functools.partial(_rwkv_chunk_kernel, nc=nc, nseq=nseq),
        grid=(b // nseq, 2, nc),
        in_specs=[tok(DB, C_RKV // DB), tok(DB, C_RKV // DB + 1), tok(DB, C_RKV // DB + 2),
                  tok(LORA_W, C_LORA // LORA_W),
                  dirp(1), dirp(W_LORA), dirp(1), dirp(A_LORA), shared, shared, state],
        out_specs=[pl.BlockSpec((None, nseq, RW_L, DB), lambda i, dd, ci: (dd, i, chunk(dd, ci), 0)),
                   state],
        out_shape=[jax.ShapeDtypeStruct((2, b, n, DB), F32),
                   jax.ShapeDtypeStruct((b, 2, hp, LANES, LANES), F32)],
        scratch_shapes=[pltpu.VMEM((nseq, hp, LANES, LANES), F32)],
        compiler_params=_cparams(3),
    )(main3, main3, main3, main3, prm['w0'], prm['w_up'], prm['a0'], prm['a_up'],
      prm['k_k'], prm['k_a'], s0_pairs)
    return y2.reshape(2, t, DB), sfin


def _state_to_pairs(s):
    b = s.shape[0]
    st = jnp.swapaxes(s, -1, -2).reshape(b, 2, HEADS // 2, 2, DH, DH)
    z = jnp.zeros((b, 2, HEADS // 2, 2, DH, 2, DH), F32)
    z = z.at[:, :, :, 0, :, 0, :].set(st[:, :, :, 0])
    z = z.at[:, :, :, 1, :, 1, :].set(st[:, :, :, 1])
    return z.reshape(b, 2, HEADS // 2, LANES, LANES)


def _pairs_to_state(z):
    b = z.shape[0]
    z = z.reshape(b, 2, HEADS // 2, 2, DH, 2, DH)
    st = jnp.stack([z[:, :, :, 0, :, 0, :], z[:, :, :, 1, :, 1, :]], axis=3)
    return jnp.swapaxes(st.reshape(b, 2, HEADS, DH, DH), -1, -2)


def _merge_kernel(u_ref, ys5_ref, d_ref, wglu_ref,
                  yf_ref, yb_ref, rkv_ref, gd_ref, gup_ref, ub_ref, lnw_ref, lnb_ref,
                  oatt_ref, gs5_ref, grw_ref, gatt_ref, x_ref, ws5_ref, wrw_ref, watt_ref, wmix_ref,
                  gpost_ref, g1_ref, o_ref):
    o_s5 = _s5_post(u_ref[...], ys5_ref[...], d_ref[...], wglu_ref[...])
    o_rw = _rwkv_post(yf_ref[...] + yb_ref[...], rkv_ref, gd_ref[:, W_LORA + A_LORA:LORA_W], gup_ref[...],
                      ub_ref[...], lnw_ref[...], lnb_ref[...])
    gate = lambda ref: _sigmoid(ref[...].astype(F32))
    merged = (gate(gs5_ref) * _dot(o_s5.astype(BF16), ws5_ref[...])
              + gate(grw_ref) * _dot(o_rw.astype(BF16), wrw_ref[...])
              + gate(gatt_ref) * _dot(oatt_ref[...].astype(BF16), watt_ref[...]))
    m = _dot(merged.astype(BF16), wmix_ref[...])
    o_ref[...] = x_ref[...] + g1_ref[...] * _rms(m, gpost_ref[...])


def _merge(main, y_s5, y_rw, rkv, o_att, gates, x, s5_d, w_glu_b, rwp, wts, mod_l, row_of_tile,
           g_post, tm):
    t = x.shape[0]
    tok = lambda w, c: pl.BlockSpec((tm, w), lambda i: (i, c))
    full = lambda r, c: pl.BlockSpec((r, c), lambda i: (0, 0))
    return pl.pallas_call(
        _merge_kernel,
        grid=(t // tm,),
        in_specs=[tok(DB, C_U // DB), tok(DB, 0), full(1, DB), full(DB, DB),
                  pl.BlockSpec((None, tm, DB), lambda i: (0, i, 0)),
                  pl.BlockSpec((None, tm, DB), lambda i: (1, i, 0)),
                  tok(3 * DB, C_RKV // (3 * DB)), tok(LORA_W, C_LORA // LORA_W), full(G_LORA, DB),
                  full(1, DB), full(1, DB),
                  full(1, DB),
                  tok(DB, 0), tok(D, 0), tok(D, 1), tok(D, 2), tok(D, 0),
                  full(DB, D), full(DB, D), full(DB, D), full(D, D), full(1, D),
                  _mod_spec(2, row_of_tile, 1)],
        out_specs=tok(D, 0),
        out_shape=jax.ShapeDtypeStruct((t, D), F32),
        compiler_params=_cparams(1),
    )(main, y_s5, s5_d, w_glu_b, y_rw, y_rw, rkv, main, rwp['g_up'], rwp['u'], rwp['ln_w'],
      rwp['ln_b'], o_att, gates, gates, gates, x, wts['br_s5'], wts['br_rw'], wts['br_att'],
      wts['mix'], g_post, mod_l)


def _ffn_dense_kernel(x_ref, gpre_ref, sc_ref, sh_ref, w1_ref, w3_ref, w2_ref, gpost_ref, g2_ref,
                      o_ref, *, tf):
    x = x_ref[...]
    h = (_rms(x, gpre_ref[...]) * (1.0 + sc_ref[...]) + sh_ref[...]).astype(BF16)
    acc = None
    for f in range(FF_DENSE // tf):
        sl = slice(f * tf, (f + 1) * tf)
        hid = _silu(_dot(h, w1_ref[:, sl])) * _dot(h, w3_ref[:, sl])
        part = _dot(hid.astype(BF16), w2_ref[sl, :])
        acc = part if acc is None else acc + part
    o_ref[...] = x + g2_ref[...] * _rms(acc, gpost_ref[...])


def _ffn_dense(x, mod_l, row_of_tile, g_pre, g_post, w1, w3, w2, tm, tf):
    t = x.shape[0]
    resident = lambda r, c: pl.BlockSpec((r, c), lambda i: (0, 0), pipeline_mode=pl.Buffered(1))
    return pl.pallas_call(
        functools.partial(_ffn_dense_kernel, tf=tf),
        grid=(t // tm,),
        in_specs=[pl.BlockSpec((tm, D), lambda i: (i, 0)),
                  pl.BlockSpec((1, D), lambda i: (0, 0)),
                  _mod_spec(4, row_of_tile, 1), _mod_spec(3, row_of_tile, 1),
                  resident(D, FF_DENSE), resident(D, FF_DENSE), resident(FF_DENSE, D),
                  pl.BlockSpec((1, D), lambda i: (0, 0)),
                  _mod_spec(5, row_of_tile, 1)],
        out_specs=pl.BlockSpec((tm, D), lambda i: (i, 0)),
        out_shape=jax.ShapeDtypeStruct((t, D), F32),
        compiler_params=_cparams(1),
    )(x, g_pre, mod_l, mod_l, w1, w3, w2, g_post, mod_l)


def _ffn_moe_kernel(x_ref, gpre_ref, sc_ref, sh_ref, rw_ref, rb_ref, w1_ref, w3_ref, w2_ref,
                    gpost_ref, g2_ref, o_ref, h_scr, comb_scr, acc_scr):
    e = pl.program_id(1)
    lane = lax.broadcasted_iota(jnp.int32, (1, LANES), 1)

    @pl.when(e == 0)
    def _():
        h = _rms(x_ref[...], gpre_ref[...]) * (1.0 + sc_ref[...]) + sh_ref[...]
        h_scr[...] = h.astype(BF16)
        acc_scr[...] = jnp.zeros_like(acc_scr)
        logits = jnp.dot(h, rw_ref[...], precision=HIGHEST, preferred_element_type=F32) + rb_ref[...]
        ex = jnp.exp(logits - jnp.max(logits, axis=-1, keepdims=True))
        probs = ex / jnp.sum(ex, axis=-1, keepdims=True)
        p1 = jnp.max(probs, axis=-1, keepdims=True)
        i1 = jnp.min(jnp.where(probs == p1, lane, LANES), axis=-1, keepdims=True)
        rest = jnp.where(lane == i1, -1.0, probs)
        p2 = jnp.max(rest, axis=-1, keepdims=True)
        i2 = jnp.min(jnp.where(rest == p2, lane, LANES), axis=-1, keepdims=True)
        den = p1 + p2
        comb_scr[...] = jnp.where(lane == i1, p1 / den, 0.0) + jnp.where(lane == i2, p2 / den, 0.0)

    h = h_scr[...]
    cw = jnp.sum(jnp.where(lane == e, comb_scr[...], 0.0), axis=-1, keepdims=True)
    hid = _silu(_dot(h, w1_ref[...])) * _dot(h, w3_ref[...])
    acc_scr[...] += cw * _dot(hid.astype(BF16), w2_ref[...])

    @pl.when(e == N_EXP - 1)
    def _():
        o_ref[...] = x_ref[...] + g2_ref[...] * _rms(acc_scr[...], gpost_ref[...])


def _ffn_moe(x, mod_l, row_of_tile, g_pre, g_post, rw, rb, w1, w3, w2, tm):
    t = x.shape[0]
    return pl.pallas_call(
        _ffn_moe_kernel,
        grid=(t // tm, N_EXP),
        in_specs=[pl.BlockSpec((tm, D), lambda i, e: (i, 0)),
                  pl.BlockSpec((1, D), lambda i, e: (0, 0)),
                  _mod_spec(4, row_of_tile, 2), _mod_spec(3, row_of_tile, 2),
                  pl.BlockSpec((D, LANES), lambda i, e: (0, 0)),
                  pl.BlockSpec((1, LANES), lambda i, e: (0, 0)),
                  pl.BlockSpec((None, D, FF_EXP), lambda i, e: (e, 0, 0)),
                  pl.BlockSpec((None, D, FF_EXP), lambda i, e: (e, 0, 0)),
                  pl.BlockSpec((None, FF_EXP, D), lambda i, e: (e, 0, 0)),
                  pl.BlockSpec((1, D), lambda i, e: (0, 0)),
                  _mod_spec(5, row_of_tile, 2)],
        out_specs=pl.BlockSpec((tm, D), lambda i, e: (i, 0)),
        out_shape=jax.ShapeDtypeStruct((t, D), F32),
        scratch_shapes=[pltpu.VMEM((tm, D), BF16), pltpu.VMEM((tm, LANES), F32),
                        pltpu.VMEM((tm, D), F32)],
        compiler_params=_cparams(2),
    )(x, g_pre, mod_l, mod_l, rw, rb, w1, w3, w2, g_post, mod_l)


def _reorder_w_in(w):
    qkv, u, rkv, lora, gates = (w[:, 0:1536], w[:, 1536:2048], w[:, 2048:3584], w[:, 3584:3840],
                                w[:, 3840:6912])
    return jnp.concatenate([gates, rkv, u, lora, qkv], axis=1)


def _layer(x, b, n, l, mod_l, row_of_tile, P, cache, tm):
    row = lambda a: a.reshape(1, -1)
    tm_proj = min(2 * tm, x.shape[0], n) if cache is not None else min(2 * tm, x.shape[0])
    row_of_ptile = (lambda i: row_of_tile(i * (tm_proj // tm)))
    outs = _projection(x, mod_l, row_of_ptile, row(P['norm_pre_mix'][l]), P['w_in_b'][l],
                       P['rwkv_conv'][l], n, tm_proj, cache is None)
    gates, proj, qkv = outs[:3]
    if cache is None:
        o_att = _ctx_attention(qkv, b, n)
        x0 = jnp.zeros((S5_G // S5_UG, 2, b, 2 * S5_UG * S5_P), F32)
        s0 = jnp.zeros((b, 2, HEADS // 2, LANES, LANES), F32)
    else:
        k_c, v_c, s5re, s5im, rw0 = cache
        o_att = _na_attention(qkv, k_c.reshape(-1, DB).astype(BF16), v_c.reshape(-1, DB).astype(BF16),
                              P['na_bias'], _na_row_masks(n // GRID_W), l, b, n)
        x0 = _s5_state_to_lanes(s5re, s5im)
        s0 = _state_to_pairs(rw0)
    y_s5, s5fin = _s5_mixer(proj, b, n, P['s5'], l, x0)
    y_rw, sfin = _rwkv_mixer(proj, b, n, P['rwkv'][l], s0)
    x = _merge(proj, y_s5, y_rw, proj, o_att, gates, x, row(P['s5_d'][l]), P['s5_w_glu_b'][l],
               P['rwkv'][l], P['merge'][l], mod_l, row_of_tile, row(P['norm_post_mix'][l]), tm)
    i = l // 2
    if l % 2 == 0:
        x = _ffn_dense(x, mod_l, row_of_tile, row(P['norm_pre_ffn'][l]), row(P['norm_post_ffn'][l]),
                       P['dense_w1_b'][i], P['dense_w3_b'][i], P['dense_w2_b'][i], tm, FF_DENSE // 2)
    else:
        x = _ffn_moe(x, mod_l, row_of_ptile, row(P['norm_pre_ffn'][l]), row(P['norm_post_ffn'][l]),
                     P['moe_rw'][i], P['moe_rb'][i], P['moe_w1_b'][i], P['moe_w3_b'][i],
                     P['moe_w2_b'][i], tm_proj)
    if cache is None:
        k_new = outs[3][:, DB:2 * DB]
        v_new = outs[3][:, 2 * DB:3 * DB]
        fre, fim = _s5_lanes_to_state(s5fin)
        return x, (k_new, v_new, fre, fim, sfin)
    return x, None


def kernel(x_prompt, x_sample, cache_k, cache_v, state_s5_re, state_s5_im, state_rwkv, c, c_ctx,
           w_ada, b_ada, norm_pre_mix, norm_post_mix, norm_pre_ffn, norm_post_ffn, w_in,
           s5_lam_re, s5_lam_im, s5_log_step, s5_b_re, s5_b_im, s5_c_re, s5_c_im, s5_d, s5_w_glu,
           rwkv_conv, rwkv_w0, rwkv_w_up, rwkv_a0, rwkv_a_up, rwkv_g_up, rwkv_k_k, rwkv_k_a,
           rwkv_u, rwkv_ln_w, rwkv_ln_b, att_rpb, w_br_s5, w_br_rwkv, w_br_att, w_mix_out,
           dense_w1, dense_w3, dense_w2, moe_router_w, moe_router_b, moe_w1, moe_w3, moe_w2):
    bc, nc_, _ = x_prompt.shape
    bl, nl, _ = x_sample.shape
    tm_c = min(512, bc * nc_)
    tm_l = min(512, nl)

    P = dict(norm_pre_mix=norm_pre_mix, norm_post_mix=norm_post_mix, norm_pre_ffn=norm_pre_ffn,
             norm_post_ffn=norm_post_ffn, s5_d=s5_d, rwkv_conv=rwkv_conv, att_rpb=att_rpb)
    P['w_in_b'] = [_reorder_w_in(w_in[l]).astype(BF16) for l in range(DEPTH)]
    P['s5_w_glu_b'] = s5_w_glu.astype(BF16)
    P['dense_w1_b'], P['dense_w3_b'], P['dense_w2_b'] = (dense_w1.astype(BF16), dense_w3.astype(BF16),
                                                         dense_w2.astype(BF16))
    P['moe_w1_b'], P['moe_w3_b'], P['moe_w2_b'] = (moe_w1.astype(BF16), moe_w3.astype(BF16),
                                                   moe_w2.astype(BF16))
    n_moe = moe_router_w.shape[0]
    P['moe_rw'] = jnp.pad(moe_router_w, ((0, 0), (0, 0), (0, LANES - N_EXP)))
    P['moe_rb'] = jnp.pad(moe_router_b, ((0, 0), (0, LANES - N_EXP)),
                          constant_values=NEG).reshape(n_moe, 1, LANES)
    P['merge'] = [dict(br_s5=w_br_s5[l].astype(BF16), br_rw=w_br_rwkv[l].astype(BF16),
                       br_att=w_br_att[l].astype(BF16), mix=w_mix_out[l].astype(BF16))
                  for l in range(DEPTH)]
    P['s5'] = jax.vmap(_s5_params)(s5_lam_re, s5_lam_im, s5_log_step, s5_b_re, s5_b_im, s5_c_re,
                                   s5_c_im)
    P['na_bias'] = _na_bias_tables(att_rpb)
    P['rwkv'] = []
    for l in range(DEPTH):
        P['rwkv'].append(dict(
            w0=rwkv_w0[l].reshape(2, 1, DB), w_up=rwkv_w_up[l], a0=rwkv_a0[l].reshape(2, 1, DB),
            a_up=rwkv_a_up[l], k_k=rwkv_k_k[l].reshape(1, DB), k_a=rwkv_k_a[l].reshape(1, DB),
            g_up=rwkv_g_up[l].astype(BF16), u=rwkv_u[l].reshape(1, DB),
            ln_w=rwkv_ln_w[l].reshape(1, DB), ln_b=rwkv_ln_b[l].reshape(1, DB)))

    cvec = jnp.zeros((8, D), F32).at[0].set(c_ctx).at[1:1 + bl].set(c)
    mod = _modulation(cvec, w_ada, b_ada)

    xp = x_prompt.reshape(bc * nc_, D)
    ks, vs, s5re, s5im, rws = [], [], [], [], []
    for l in range(DEPTH):
        xp, (k_n, v_n, fre, fim, sfin) = _layer(xp, bc, nc_, l, mod[l], lambda i: 0, P, None, tm_c)
        ks.append(k_n.reshape(bc, nc_, HEADS, DH))
        vs.append(v_n.reshape(bc, nc_, HEADS, DH))
        s5re.append(fre)
        s5im.append(fim)
        rws.append(_pairs_to_state(sfin))
    new_k = jnp.stack(ks, axis=1)
    new_v = jnp.stack(vs, axis=1)
    new_s5_re = jnp.stack(s5re, axis=1)
    new_s5_im = jnp.stack(s5im, axis=1)
    new_rwkv = jnp.stack(rws, axis=1)

    xs = x_sample.reshape(bl * nl, D)
    tiles_per_seq = nl // tm_l
    for l in range(DEPTH):
        cache = (cache_k[:, l], cache_v[:, l], state_s5_re[:, l], state_s5_im[:, l], state_rwkv[:, l])
        xs, _ = _layer(xs, bl, nl, l, mod[l], lambda i: 1 + i // tiles_per_seq, P, cache, tm_l)

    return (xp.reshape(bc, nc_, D), xs.reshape(bl, nl, D), new_k, new_v, new_s5_re, new_s5_im,
            new_rwkv)
```
